```python
import jax, jax.numpy as jnp
from jax import lax
import numpy as np

D_MODEL = 2048
BATCH = 8
SEQ = 2048
DEPTH = 1
DEC_BATCH = 32
DEC_SEQ = 4
PAST_LEN = 8192
PAGE_SIZE = 128

HEAD_DIM = 128
N_MIX_HEADS = D_MODEL // HEAD_DIM
MEM_HEADS = 4
FOX_HEADS = (N_MIX_HEADS - MEM_HEADS) // 2
HGRN_HEADS = N_MIX_HEADS - MEM_HEADS - FOX_HEADS
FOX_W = FOX_HEADS * HEAD_DIM
HGRN_W = HGRN_HEADS * HEAD_DIM
MEM_W = MEM_HEADS * HEAD_DIM
MIX_W = FOX_W + HGRN_W + MEM_W
IN_W = 3 * FOX_W + FOX_HEADS + 4 * HGRN_W + MEM_W
MEM_TOKENS = 256
Q_BLOCK = 128
HGRN_CHUNK = 64
N_GROUPS = 4
EXPERTS_PER_GROUP = 8
N_EXPERTS = N_GROUPS * EXPERTS_PER_GROUP
EXPERT_FF = D_MODEL // 4
TOP_K = 2
MOE_BLOCK = 128
EPS = 1e-6
FOX_FORGET_BIAS = 2.0

kernel_name = 'hymba_fox_hgrn2_hier_moe_step'


def rms_norm(x, g):
    xf = x.astype(jnp.float32)
    y = xf * lax.rsqrt(jnp.mean(xf * xf, axis=-1, keepdims=True) + EPS)
    return (y * g.astype(jnp.float32)).astype(x.dtype)


def project(xn, w_in, b_f, g_fq, g_fk, lb, g_mq):
    B, L, _ = xn.shape
    sizes = (FOX_W, FOX_W, FOX_W, FOX_HEADS, HGRN_W, HGRN_W, HGRN_W, HGRN_W)
    cuts, acc = [], 0
    for s in sizes:
        acc += s
        cuts.append(acc)
    fq, fk, fv, ff, hq, hf, hi, hg, mq = jnp.split(xn @ w_in, cuts, axis=-1)
    heads = lambda t, n: t.reshape(B, L, n, HEAD_DIM)
    fox_q = rms_norm(heads(fq, FOX_HEADS), g_fq)
    fox_k = rms_norm(heads(fk, FOX_HEADS), g_fk)
    fox_v = heads(fv, FOX_HEADS)
    fox_logf = jax.nn.log_sigmoid((ff + b_f).astype(jnp.float32))
    z = heads(hf, HGRN_HEADS).astype(jnp.float32)
    hgrn_logf = jnp.logaddexp(jnp.log(lb), jnp.log1p(-lb) + jax.nn.log_sigmoid(z))
    hgrn_k = -jnp.expm1(hgrn_logf)
    hgrn_q = jax.nn.silu(heads(hq, HGRN_HEADS)) * HEAD_DIM ** -0.5
    hgrn_v = heads(hi, HGRN_HEADS)
    mem_q = rms_norm(heads(mq, MEM_HEADS), g_mq)
    return fox_q, fox_k, fox_v, fox_logf, hgrn_q, hgrn_k, hgrn_v, hgrn_logf, hg, mem_q


def fox_attend(q, k, v, c_q, c_k, q_pos, k_pos):
    s = jnp.einsum('bqhd,bkhd->bhqk', q, k).astype(jnp.float32) * HEAD_DIM ** -0.5
    s = s + (jnp.swapaxes(c_q, 1, 2)[..., :, None] - jnp.swapaxes(c_k, 1, 2)[..., None, :])
    s = jnp.where(k_pos[None, :] <= q_pos[:, None], s, -jnp.inf)
    p = jax.nn.softmax(s, axis=-1).astype(v.dtype)
    return jnp.einsum('bhqk,bkhd->bqhd', p, v)


def fox_prompt(q, k, v, logf):
    B, S, H, Dh = q.shape
    nb = S // Q_BLOCK
    c = jnp.cumsum(logf, axis=1)
    k_pos = jnp.arange(S)
    qb = jnp.moveaxis(q.reshape(B, nb, Q_BLOCK, H, Dh), 1, 0)
    cb = jnp.moveaxis(c.reshape(B, nb, Q_BLOCK, H), 1, 0)

    def block(args):
        i, q_i, c_i = args
        q_pos = i * Q_BLOCK + jnp.arange(Q_BLOCK)
        return fox_attend(q_i, k, v, c_i, c, q_pos, k_pos)

    o = lax.map(block, (jnp.arange(nb), qb, cb))
    return jnp.moveaxis(o, 0, 1).reshape(B, S, H, Dh)


def hgrn2_chunked(q, k, v, logf, S0):
    B, L, H, Dk = q.shape
    out_dtype, st_dtype = v.dtype, S0.dtype
    q, k, v, logf = (a.astype(jnp.float32) for a in (q, k, v, logf))
    C = HGRN_CHUNK if L % HGRN_CHUNK == 0 else L
    n = L // C
    to_chunks = lambda a: jnp.moveaxis(a.reshape(B, n, C, *a.shape[2:]), 1, 0)
    tri = jnp.tril(jnp.ones((C, C), bool))[None, :, :, None, None]

    def step(S, inp):
        qc, kc, vc, gc = inp
        b = jnp.cumsum(gc, axis=1)
        o_inter = jnp.einsum('bthk,bhkv->bthv', qc * jnp.exp(b), S)
        diff = b[:, :, None] - b[:, None, :]
        decay = jnp.exp(jnp.where(tri, diff, -jnp.inf))
        A = jnp.einsum('bthk,btshk,bshk->bhts', qc, decay, kc)
        o_intra = jnp.einsum('bhts,bshv->bthv', A, vc)
        b_last = b[:, -1]
        S_new = S * jnp.exp(b_last)[..., None] + jnp.einsum(
            'bshk,bshv->bhkv', kc * jnp.exp(b_last[:, None] - b), vc)
        return S_new, o_inter + o_intra

    S_fin, o = lax.scan(step, S0.astype(jnp.float32),
                        (to_chunks(q), to_chunks(k), to_chunks(v), to_chunks(logf)))
    o = jnp.moveaxis(o, 0, 1).reshape(B, L, H, -1)
    return o.astype(out_dtype), S_fin.astype(st_dtype)


def memory_kv(mem, g_mem, w_mkv, g_mk):
    B, M, _ = mem.shape
    mk, mv = jnp.split(rms_norm(mem, g_mem) @ w_mkv, 2, axis=-1)
    mk = rms_norm(mk.reshape(B, M, MEM_HEADS, HEAD_DIM), g_mk)
    return mk, mv.reshape(B, M, MEM_HEADS, HEAD_DIM)


def memory_attend(q, mk, mv):
    s = jnp.einsum('bqhd,bmhd->bhqm', q, mk).astype(jnp.float32) * HEAD_DIM ** -0.5
    p = jax.nn.softmax(s, axis=-1).astype(mv.dtype)
    return jnp.einsum('bhqm,bmhd->bqhd', p, mv)


def merge_heads(fox_o, hgrn_o, hg, mem_o, g_hn, w_out):
    B, L = fox_o.shape[:2]
    hgrn_o = rms_norm(hgrn_o, g_hn).reshape(B, L, HGRN_W) * jax.nn.silu(hg)
    cat = jnp.concatenate([fox_o.reshape(B, L, FOX_W), hgrn_o, mem_o.reshape(B, L, MEM_W)], axis=-1)
    return cat @ w_out


def expert_dispatch(x, expert, gate, w_gate, w_up, w_down):
    T, D = x.shape
    A = T * TOP_K
    n_blocks = -(-(A + N_EXPERTS * (MOE_BLOCK - 1)) // MOE_BLOCK)
    cap = n_blocks * MOE_BLOCK
    flat_e = expert.reshape(A)
    flat_t = jnp.repeat(jnp.arange(T, dtype=jnp.int32), TOP_K)
    flat_g = gate.reshape(A)
    order = jnp.argsort(flat_e)
    e_sorted = flat_e[order]
    counts = jnp.zeros((N_EXPERTS,), jnp.int32).at[flat_e].add(1)
    padded = (counts + MOE_BLOCK - 1) // MOE_BLOCK * MOE_BLOCK
    pad_start = jnp.cumsum(padded) - padded
    start = jnp.cumsum(counts) - counts
    dest = pad_start[e_sorted] + jnp.arange(A, dtype=jnp.int32) - start[e_sorted]
    slot_tok = jnp.full((cap,), T, jnp.int32).at[dest].set(flat_t[order])
    slot_gate = jnp.zeros((cap,), jnp.float32).at[dest].set(flat_g[order])
    block_first = jnp.arange(n_blocks, dtype=jnp.int32) * MOE_BLOCK
    block_e = jnp.minimum(jnp.searchsorted(jnp.cumsum(padded), block_first, side='right'), N_EXPERTS - 1)
    x_pad = jnp.concatenate([x, jnp.zeros((1, D), x.dtype)], axis=0)

    def run(args):
        e, tok = args
        xb = x_pad[tok]
        return (jax.nn.silu(xb @ w_gate[e]) * (xb @ w_up[e])) @ w_down[e]

    y = lax.map(run, (block_e, slot_tok.reshape(n_blocks, MOE_BLOCK))).reshape(cap, D)
    y = y * slot_gate[:, None].astype(y.dtype)
    return jnp.zeros((T + 1, D), y.dtype).at[slot_tok].add(y)[:T]


def hier_moe(xn, w_rg, b_rg, w_re, b_re, w_gate, w_up, w_down):
    B, L, D = xn.shape
    T = B * L
    x = xn.reshape(T, D)
    g_logits = (x @ w_rg + b_rg).astype(jnp.float32)
    g_sel = jnp.argmax(g_logits, axis=-1)
    g_prob = jnp.take_along_axis(jax.nn.softmax(g_logits, axis=-1), g_sel[:, None], axis=1)
    e_logits = (x @ w_re + b_re).astype(jnp.float32).reshape(T, N_GROUPS, EXPERTS_PER_GROUP)
    e_in = jnp.take_along_axis(e_logits, g_sel[:, None, None], axis=1)[:, 0]
    top_v, top_i = lax.top_k(e_in, TOP_K)
    gate = g_prob * jax.nn.softmax(top_v, axis=-1)
    expert = (g_sel[:, None] * EXPERTS_PER_GROUP + top_i).astype(jnp.int32)
    return expert_dispatch(x, expert, gate, w_gate, w_up, w_down).reshape(B, L, D)


def setup_inputs(seed: int = 0) -> dict:
    key = jax.random.key(seed)
    ks = iter(jax.random.split(key, 40))
    nrm = lambda shape, scale=1.0: jax.random.normal(next(ks), shape, jnp.float32) * scale
    gain = lambda shape: 1.0 + 0.02 * nrm(shape)
    n_pages = PAST_LEN // PAGE_SIZE
    n_used = DEC_BATCH * n_pages
    n_pool = n_used + max(1, n_used // 4)
    page_table = jax.random.permutation(next(ks), n_pool)[:n_used].reshape(DEC_BATCH, n_pages).astype(jnp.int32)
    return {
        'x_prompt': nrm((BATCH, SEQ, D_MODEL)),
        'x_sample': nrm((DEC_BATCH, DEC_SEQ, D_MODEL)),
        'cache_fox_k': nrm((DEPTH, n_pool, PAGE_SIZE, FOX_HEADS, HEAD_DIM)),
        'cache_fox_v': nrm((DEPTH, n_pool, PAGE_SIZE, FOX_HEADS, HEAD_DIM)),
        'cache_fox_logf': jax.nn.log_sigmoid(FOX_FORGET_BIAS + nrm((DEPTH, n_pool, PAGE_SIZE, FOX_HEADS))),
        'cache_mem_k': nrm((DEPTH, DEC_BATCH, MEM_TOKENS, MEM_HEADS, HEAD_DIM)),
        'cache_mem_v': nrm((DEPTH, DEC_BATCH, MEM_TOKENS, MEM_HEADS, HEAD_DIM)),
        'state_hgrn': nrm((DEPTH, DEC_BATCH, HGRN_HEADS, HEAD_DIM, HEAD_DIM), 0.5),
        'page_table': page_table,
        'mem_prompt': nrm((BATCH, MEM_TOKENS, D_MODEL)),
        'g_attn_norm': gain((DEPTH, D_MODEL)),
        'w_in': nrm((DEPTH, D_MODEL, IN_W), D_MODEL ** -0.5),
        'b_fox_f': FOX_FORGET_BIAS + 0.1 * nrm((DEPTH, FOX_HEADS)),
        'g_fox_q': gain((DEPTH, HEAD_DIM)),
        'g_fox_k': gain((DEPTH, HEAD_DIM)),
        'lb_logits': nrm((DEPTH + 1, HGRN_W), 0.1),
        'g_hgrn_out': gain((DEPTH, HEAD_DIM)),
        'g_mem_norm': gain((DEPTH, D_MODEL)),
        'w_mem_kv': nrm((DEPTH, D_MODEL, 2 * MEM_W), D_MODEL ** -0.5),
        'g_mem_q': gain((DEPTH, HEAD_DIM)),
        'g_mem_k': gain((DEPTH, HEAD_DIM)),
        'w_out': nrm((DEPTH, MIX_W, D_MODEL), MIX_W ** -0.5),
        'g_ffn_norm': gain((DEPTH, D_MODEL)),
        'w_router_group': nrm((DEPTH, D_MODEL, N_GROUPS), D_MODEL ** -0.5),
        'b_router_group': nrm((DEPTH, N_GROUPS), 0.01),
        'w_router_expert': nrm((DEPTH, D_MODEL, N_EXPERTS), D_MODEL ** -0.5),
        'b_router_expert': nrm((DEPTH, N_EXPERTS), 0.01),
        'w_gate_e': nrm((DEPTH, N_EXPERTS, D_MODEL, EXPERT_FF), D_MODEL ** -0.5),
        'w_up_e': nrm((DEPTH, N_EXPERTS, D_MODEL, EXPERT_FF), D_MODEL ** -0.5),
        'w_down_e': nrm((DEPTH, N_EXPERTS, EXPERT_FF, D_MODEL), EXPERT_FF ** -0.5),
    }


def reference(x_prompt, x_sample, cache_fox_k, cache_fox_v, cache_fox_logf, cache_mem_k, cache_mem_v,
              state_hgrn, page_table, mem_prompt, g_attn_norm, w_in, b_fox_f, g_fox_q, g_fox_k,
              lb_logits, g_hgrn_out, g_mem_norm, w_mem_kv, g_mem_q, g_mem_k, w_out, g_ffn_norm,
              w_router_group, b_router_group, w_router_expert, b_router_expert,
              w_gate_e, w_up_e, w_down_e):
    Bp = x_prompt.shape[0]
    Bd, L = x_sample.shape[:2]
    P = page_table.shape[1] * cache_fox_k.shape[2]
    lbs = jnp.cumsum(jax.nn.softmax(lb_logits.astype(jnp.float32), axis=0), axis=0)
    hp, hs = x_prompt, x_sample
    fkp, fvp, flp, hsp, mkp, mvp, fks, fvs, fls, hss = ([] for _ in range(10))
    for l in range(DEPTH):
        lb = lbs[l].reshape(HGRN_HEADS, HEAD_DIM)
        fq, fk, fv, flf, hq, hk, hv, hlf, hg, mq = project(
            rms_norm(hp, g_attn_norm[l]), w_in[l], b_fox_f[l], g_fox_q[l], g_fox_k[l], lb, g_mem_q[l])
        fox_o = fox_prompt(fq, fk, fv, flf)
        S0 = jnp.zeros((Bp, HGRN_HEADS, HEAD_DIM, HEAD_DIM), jnp.float32)
        hgrn_o, S_p = hgrn2_chunked(hq, hk, hv, hlf, S0)
        mk, mv = memory_kv(mem_prompt, g_mem_norm[l], w_mem_kv[l], g_mem_k[l])
        mem_o = memory_attend(mq, mk, mv)
        hp = hp + merge_heads(fox_o, hgrn_o, hg, mem_o, g_hgrn_out[l], w_out[l])
        hp = hp + hier_moe(rms_norm(hp, g_ffn_norm[l]), w_router_group[l], b_router_group[l],
                           w_router_expert[l], b_router_expert[l], w_gate_e[l], w_up_e[l], w_down_e[l])
        fkp.append(fk); fvp.append(fv); flp.append(flf); hsp.append(S_p); mkp.append(mk); mvp.append(mv)
        fq, fk, fv, flf, hq, hk, hv, hlf, hg, mq = project(
            rms_norm(hs, g_attn_norm[l]), w_in[l], b_fox_f[l], g_fox_q[l], g_fox_k[l], lb, g_mem_q[l])
        k_all = jnp.concatenate([cache_fox_k[l][page_table].reshape(Bd, P, FOX_HEADS, HEAD_DIM), fk], axis=1)
        v_all = jnp.concatenate([cache_fox_v[l][page_table].reshape(Bd, P, FOX_HEADS, HEAD_DIM), fv], axis=1)
        lf_past = cache_fox_logf[l][page_table].reshape(Bd, P, FOX_HEADS).astype(jnp.float32)
        c_all = jnp.cumsum(jnp.concatenate([lf_past, flf], axis=1), axis=1)
        c_all = c_all - c_all[:, -1:]
        fox_o = fox_attend(fq, k_all, v_all, c_all[:, P:], c_all, P + jnp.arange(L), jnp.arange(P + L))
        hgrn_o, S_s = hgrn2_chunked(hq, hk, hv, hlf, state_hgrn[l])
        mem_o = memory_attend(mq, cache_mem_k[l], cache_mem_v[l])
        hs = hs + merge_heads(fox_o, hgrn_o, hg, mem_o, g_hgrn_out[l], w_out[l])
        hs = hs + hier_moe(rms_norm(hs, g_ffn_norm[l]), w_router_group[l], b_router_group[l],
                           w_router_expert[l], b_router_expert[l], w_gate_e[l], w_up_e[l], w_down_e[l])
        fks.append(fk); fvs.append(fv); fls.append(flf); hss.append(S_s)
    return (hp, hs,
            jnp.stack(fkp), jnp.stack(fvp), jnp.stack(flp), jnp.stack(hsp), jnp.stack(mkp), jnp.stack(mvp),
            jnp.stack(fks), jnp.stack(fvs), jnp.stack(fls), jnp.stack(hss))
```

```python
import functools

import jax
import jax.numpy as jnp
from jax import lax
from jax.experimental import pallas as pl
from jax.experimental.pallas import tpu as pltpu

F32 = jnp.float32
BF16 = jnp.bfloat16

HEAD_DIM = 128
FOX_HEADS = 6
HGRN_HEADS = 6
MEM_HEADS = 4
FOX_W = FOX_HEADS * HEAD_DIM
HGRN_W = HGRN_HEADS * HEAD_DIM
MEM_W = MEM_HEADS * HEAD_DIM
N_GROUPS = 4
EXPERTS_PER_GROUP = 8
N_EXPERTS = N_GROUPS * EXPERTS_PER_GROUP
TOP_K = 2
EPS = 1e-6
ATTN_SCALE = HEAD_DIM ** -0.5
LANES = 128
SUBLANES = 8
SEG_W = FOX_W
N_SEG = 8
HEAD_PAD = 8
MOE_BM = 256
VMEM_LIMIT = 52 * 1024 * 1024

NT_DIMS = (((1,), (1,)), ((), ()))
TN_DIMS = (((0,), (0,)), ((), ()))


def _cparams(sem, vmem=VMEM_LIMIT):
    return pltpu.CompilerParams(dimension_semantics=sem, vmem_limit_bytes=vmem)


def _split3(x):
    hi = x.astype(BF16)
    r1 = x - hi.astype(F32)
    mid = r1.astype(BF16)
    lo = (r1 - mid.astype(F32)).astype(BF16)
    return hi, mid, lo


def _dot(a, b):
    return jnp.dot(a, b, preferred_element_type=F32)


def _dot_nt(a, b):
    return lax.dot_general(a, b, NT_DIMS, preferred_element_type=F32)


def _sigmoid(x):
    return 1.0 / (1.0 + jnp.exp(-x))


def _log_sigmoid(x):
    return jnp.minimum(x, 0.0) - jnp.log(1.0 + jnp.exp(-jnp.abs(x)))


def _rms_heads(a, g, n_heads, scale):
    outs = []
    for h in range(n_heads):
        ah = a[:, h * HEAD_DIM:(h + 1) * HEAD_DIM]
        ms = jnp.mean(ah * ah, axis=-1, keepdims=True)
        outs.append(ah * lax.rsqrt(ms + EPS) * (g * scale))
    return jnp.concatenate(outs, axis=-1)


def _in_proj_kernel(x_ref, gn_ref, w_ref, gfq_ref, gfk_ref, gmq_ref, lbl_ref, bf_ref,
                    fq_ref, fk_ref, fv_ref, hq_ref, hlf_ref, hv_ref, hgs_ref, mq_ref, flf_ref, xn_s):
    j = pl.program_id(1)

    @pl.when(j == 0)
    def _():
        x = x_ref[...]
        ms = jnp.mean(x * x, axis=-1, keepdims=True)
        xn_s[...] = (x * lax.rsqrt(ms + EPS) * gn_ref[...]).astype(BF16)

    acc = _dot(xn_s[...], w_ref[...])

    @pl.when(j == 0)
    def _():
        fq_ref[...] = _rms_heads(acc, gfq_ref[...], FOX_HEADS, ATTN_SCALE).astype(fq_ref.dtype)

    @pl.when(j == 1)
    def _():
        fk_ref[...] = _rms_heads(acc, gfk_ref[...], FOX_HEADS, 1.0)

    @pl.when(j == 2)
    def _():
        fv_ref[...] = acc

    @pl.when(j == 3)
    def _():
        hq_ref[...] = acc * _sigmoid(acc) * ATTN_SCALE

    @pl.when(j == 4)
    def _():
        l = lbl_ref[...]
        mx = jnp.max(l, axis=0, keepdims=True)
        ex = jnp.exp(l - mx)
        lb = ex[0:1, :] / jnp.sum(ex, axis=0, keepdims=True)
        hlf_ref[...] = jnp.log(lb + (1.0 - lb) * _sigmoid(acc))

    @pl.when(j == 5)
    def _():
        hv_ref[...] = acc

    @pl.when(j == 6)
    def _():
        hgs_ref[...] = acc * _sigmoid(acc)

    @pl.when(j == 7)
    def _():
        mq_ref[...] = _rms_heads(acc[:, :MEM_W], gmq_ref[...], MEM_HEADS, ATTN_SCALE).astype(mq_ref.dtype)
        flf_ref[...] = _log_sigmoid(acc[:, MEM_W:MEM_W + LANES] + bf_ref[...])


def _in_proj(x, g_norm, w_seg, g_fq, g_fk, g_mq, lb_logits, b_f_pad, tm):
    T, D = x.shape
    row = lambda w: pl.BlockSpec((tm, w), lambda i, j: (i, 0))
    full = lambda a: pl.BlockSpec(a.shape, lambda i, j: (0,) * a.ndim)
    outs = [
        jax.ShapeDtypeStruct((T, FOX_W), BF16),
        jax.ShapeDtypeStruct((T, FOX_W), F32),
        jax.ShapeDtypeStruct((T, FOX_W), F32),
        jax.ShapeDtypeStruct((T, HGRN_W), F32),
        jax.ShapeDtypeStruct((T, HGRN_W), F32),
        jax.ShapeDtypeStruct((T, HGRN_W), F32),
        jax.ShapeDtypeStruct((T, HGRN_W), F32),
        jax.ShapeDtypeStruct((T, MEM_W), BF16),
        jax.ShapeDtypeStruct((T, LANES), F32),
    ]
    return pl.pallas_call(
        _in_proj_kernel,
        grid=(T // tm, N_SEG),
        in_specs=[pl.BlockSpec((tm, D), lambda i, j: (i, 0)), full(g_norm),
                  pl.BlockSpec((D, SEG_W), lambda i, j: (0, j)),
                  full(g_fq), full(g_fk), full(g_mq), full(lb_logits), full(b_f_pad)],
        out_specs=[row(o.shape[1]) for o in outs],
        out_shape=outs,
        scratch_shapes=[pltpu.VMEM((tm, D), BF16)],
        compiler_params=_cparams(("arbitrary", "arbitrary")),
        name="in_proj",
    )(x, g_norm, w_seg, g_fq, g_fk, g_mq, lb_logits, b_f_pad)


CUM_C = 256


def _fox_cum_kernel(lf_ref, cq_ref, ck_ref):
    S = lf_ref.shape[1]
    r = lax.broadcasted_iota(jnp.int32, (CUM_C, CUM_C), 0)
    c = lax.broadcasted_iota(jnp.int32, (CUM_C, CUM_C), 1)
    tri = jnp.where(c <= r, 1.0, 0.0).astype(BF16)
    carry = jnp.zeros((1, LANES), F32)
    for i in range(S // CUM_C):
        sl = slice(i * CUM_C, (i + 1) * CUM_C)
        hi, mid, lo = _split3(lf_ref[0, sl, :])
        cum = (_dot(tri, hi) + _dot(tri, mid)) + _dot(tri, lo) + carry
        carry = cum[CUM_C - 1:CUM_C, :]
        for h in range(FOX_HEADS):
            cq_ref[0, h, sl, :] = jnp.broadcast_to(cum[:, h:h + 1], (CUM_C, LANES))
        ck_ref[0, :, sl] = cum.T[0:HEAD_PAD, :]


def _fox_cum(flf):
    B, S, _ = flf.shape
    return pl.pallas_call(
        _fox_cum_kernel,
        grid=(B,),
        in_specs=[pl.BlockSpec((1, S, LANES), lambda b: (b, 0, 0))],
        out_specs=[pl.BlockSpec((1, FOX_HEADS, S, LANES), lambda b: (b, 0, 0, 0)),
                   pl.BlockSpec((1, HEAD_PAD, S), lambda b: (b, 0, 0))],
        out_shape=[jax.ShapeDtypeStruct((B, FOX_HEADS, S, LANES), F32),
                   jax.ShapeDtypeStruct((B, HEAD_PAD, S), F32)],
        compiler_params=_cparams(("arbitrary",)),
        name="fox_cum",
    )(flf)


FOX_T = 256


def _softmax_step(s, v_bf, m, l, acc):
    m_new = jnp.maximum(m, jnp.max(s, axis=-1, keepdims=True))
    alpha = jnp.exp(m - m_new)
    p = jnp.exp(s - m_new)
    l = alpha * l + jnp.sum(p, axis=-1, keepdims=True)
    acc = alpha * acc + _dot(p.astype(BF16), v_bf)
    return m_new, l, acc


def _fox_attn_kernel(q_ref, k_ref, v_ref, cq_ref, ck_ref, o_ref, kb, vb):
    qi = pl.program_id(1)
    T = FOX_T

    @pl.when(qi == 0)
    def _():
        kb[...] = k_ref[0].astype(BF16)
        vb[...] = v_ref[0].astype(BF16)

    row = lax.broadcasted_iota(jnp.int32, (T, T), 0)
    col = lax.broadcasted_iota(jnp.int32, (T, T), 1)
    causal = col <= row
    for h in range(FOX_HEADS):
        hs = slice(h * HEAD_DIM, (h + 1) * HEAD_DIM)
        qh = q_ref[0, :, hs]
        cqh = cq_ref[0, h, :, 0:1]

        def tile(ks, carry, masked):
            s = _dot_nt(qh, kb[pl.ds(ks, T), hs])
            s = s + (cqh - ck_ref[0, h:h + 1, pl.ds(ks, T)])
            if masked:
                s = jnp.where(causal, s, -jnp.inf)
            return _softmax_step(s, vb[pl.ds(ks, T), hs], *carry)

        init = (jnp.full((T, 1), -jnp.inf, F32), jnp.zeros((T, 1), F32), jnp.zeros((T, HEAD_DIM), F32))
        carry = lax.fori_loop(0, qi, lambda kt, c: tile(pl.multiple_of(kt * T, T), c, False), init)
        m, l, acc = tile(pl.multiple_of(qi * T, T), carry, True)
        o_ref[0, :, hs] = (acc / l).astype(o_ref.dtype)


def _fox_attn(fq, fk, fv, cq, ck):
    B, S, _ = fq.shape
    T = FOX_T
    return pl.pallas_call(
        _fox_attn_kernel,
        grid=(B, S // T),
        in_specs=[pl.BlockSpec((1, T, FOX_W), lambda b, i: (b, i, 0)),
                  pl.BlockSpec((1, S, FOX_W), lambda b, i: (b, 0, 0)),
                  pl.BlockSpec((1, S, FOX_W), lambda b, i: (b, 0, 0)),
                  pl.BlockSpec((1, FOX_HEADS, T, LANES), lambda b, i: (b, 0, i, 0)),
                  pl.BlockSpec((1, HEAD_PAD, S), lambda b, i: (b, 0, 0))],
        out_specs=pl.BlockSpec((1, T, FOX_W), lambda b, i: (b, i, 0)),
        out_shape=jax.ShapeDtypeStruct((B, S, FOX_W), BF16),
        scratch_shapes=[pltpu.VMEM((S, FOX_W), BF16), pltpu.VMEM((S, FOX_W), BF16)],
        compiler_params=_cparams(("arbitrary", "arbitrary")),
        name="fox_attn",
    )(fq, fk, fv, cq, ck)


def _level_ref(b, m):
    C = b.shape[0]
    if 2 * m >= SUBLANES:
        b3 = b.reshape(C // (2 * m), 2 * m, HEAD_DIM)
        r = jnp.broadcast_to(b3[:, m - 1:m, :], b3.shape)
        return r.reshape(C, HEAD_DIM)
    b3 = b.reshape(C // SUBLANES, SUBLANES, HEAD_DIM)
    sub = lax.broadcasted_iota(jnp.int32, b3.shape, 1)
    pick = lambda i: jnp.broadcast_to(b3[:, i:i + 1, :], b3.shape)
    if m == 2:
        r = jnp.where(sub < 4, pick(1), pick(5))
    else:
        r = jnp.where(sub < 2, pick(0), jnp.where(sub < 4, pick(2), jnp.where(sub < 6, pick(4), pick(6))))
    return r.reshape(C, HEAD_DIM)


def _level_index(C):
    t = lax.broadcasted_iota(jnp.int32, (C, C), 0)
    s = lax.broadcasted_iota(jnp.int32, (C, C), 1)
    x = jnp.bitwise_xor(t, s)
    lvl = jnp.full((C, C), -1, jnp.int32)
    j, m = 0, 1
    while m < C:
        lvl = jnp.where(x >= m, j, lvl)
        j, m = j + 1, 2 * m
    return jnp.where(t > s, lvl, -1)


def _hgrn_chunk(q, g, v, st, tri, lvl, n_valid):
    C = q.shape[0]
    hi, mid, lo = _split3(g)
    b = (_dot(tri, hi) + _dot(tri, mid)) + _dot(tri, lo)
    k = 1.0 - jnp.exp(g)
    a = jnp.zeros((C, C), F32)
    j, m = 0, 1
    while m < C:
        e = jnp.exp(-jnp.abs(b - _level_ref(b, m)))
        a_l = _dot_nt((q * e).astype(BF16), (k * e).astype(BF16))
        a = jnp.where(lvl == j, a_l, a)
        j, m = j + 1, 2 * m
    v_bf = v.astype(BF16)
    diag = jnp.sum(q * k, axis=-1, keepdims=True)
    o = _dot_nt((q * jnp.exp(b)).astype(BF16), st.astype(BF16)) + _dot(a.astype(BF16), v_bf) + diag * v
    b_last = b[n_valid - 1:n_valid, :]
    kt = k * jnp.exp(jnp.minimum(b_last - b, 0.0))
    if n_valid < C:
        rows = lax.broadcasted_iota(jnp.int32, (C, HEAD_DIM), 0)
        kt = jnp.where(rows < n_valid, kt, 0.0)
    st_new = st * jnp.exp(b_last) + lax.dot_general(v_bf, kt.astype(BF16), TN_DIMS, preferred_element_type=F32)
    return o, st_new


def _hgrn_kernel(*refs, C, n_chunks, n_heads, n_valid, has_s0):
    if has_s0:
        q_ref, g_ref, v_ref, gs_ref, gn_ref, s0_ref, o_ref, sf_ref = refs
    else:
        q_ref, g_ref, v_ref, gs_ref, gn_ref, o_ref, sf_ref = refs
    r = lax.broadcasted_iota(jnp.int32, (C, C), 0)
    c = lax.broadcasted_iota(jnp.int32, (C, C), 1)
    tri = jnp.where(c <= r, 1.0, 0.0).astype(BF16)
    lvl = _level_index(C)
    gn = gn_ref[...]
    for h in range(n_heads):
        hs = slice(h * HEAD_DIM, (h + 1) * HEAD_DIM)
        st0 = s0_ref[0, h].T if has_s0 else jnp.zeros((HEAD_DIM, HEAD_DIM), F32)

        def body(ci, st):
            rs = pl.ds(pl.multiple_of(ci * C, C), C)
            o, st = _hgrn_chunk(q_ref[0, rs, hs], g_ref[0, rs, hs], v_ref[0, rs, hs], st, tri, lvl, n_valid)
            ms = jnp.mean(o * o, axis=-1, keepdims=True)
            o_ref[0, rs, hs] = (o * lax.rsqrt(ms + EPS) * gn * gs_ref[0, rs, hs]).astype(o_ref.dtype)
            return st

        st = lax.fori_loop(0, n_chunks, body, st0) if n_chunks > 1 else body(0, st0)
        sf_ref[0, h] = st.T


def _hgrn(hq, hlf, hv, hgs, g_hn, s0, C, n_heads, n_valid):
    B, L, _ = hq.shape
    hp = HGRN_HEADS // n_heads
    w = n_heads * HEAD_DIM
    seq = pl.BlockSpec((1, L, w), lambda b, h: (b, 0, h))
    st_spec = pl.BlockSpec((1, n_heads, HEAD_DIM, HEAD_DIM), lambda b, h: (b, h, 0, 0))
    in_specs = [seq, seq, seq, seq, pl.BlockSpec((1, HEAD_DIM), lambda b, h: (0, 0))]
    args = [hq, hlf, hv, hgs, g_hn]
    if s0 is not None:
        in_specs.append(st_spec)
        args.append(s0)
    kern = functools.partial(_hgrn_kernel, C=C, n_chunks=L // C, n_heads=n_heads, n_valid=n_valid,
                             has_s0=s0 is not None)
    return pl.pallas_call(
        kern,
        grid=(B, hp),
        in_specs=in_specs,
        out_specs=[seq, st_spec],
        out_shape=[jax.ShapeDtypeStruct((B, L, HGRN_W), BF16),
                   jax.ShapeDtypeStruct((B, HGRN_HEADS, HEAD_DIM, HEAD_DIM), F32)],
        compiler_params=_cparams(("arbitrary", "arbitrary")),
        name="hgrn",
    )(*args)


def _mem_kv_kernel(x_ref, gn_ref, w_ref, gk_ref, mk_ref, mv_ref, xn_s):
    j = pl.program_id(1)

    @pl.when(j == 0)
    def _():
        x = x_ref[...]
        ms = jnp.mean(x * x, axis=-1, keepdims=True)
        xn_s[...] = (x * lax.rsqrt(ms + EPS) * gn_ref[...]).astype(BF16)

    acc = _dot(xn_s[...], w_ref[...])

    @pl.when(j == 0)
    def _():
        mk_ref[...] = _rms_heads(acc, gk_ref[...], MEM_HEADS, 1.0)

    @pl.when(j == 1)
    def _():
        mv_ref[...] = acc


def _mem_kv(mem, g_norm, w, g_mk, tm):
    T, D = mem.shape
    out = jax.ShapeDtypeStruct((T, MEM_W), F32)
    return pl.pallas_call(
        _mem_kv_kernel,
        grid=(T // tm, 2),
        in_specs=[pl.BlockSpec((tm, D), lambda i, j: (i, 0)), pl.BlockSpec((1, D), lambda i, j: (0, 0)),
                  pl.BlockSpec((D, MEM_W), lambda i, j: (0, j)), pl.BlockSpec((1, HEAD_DIM), lambda i, j: (0, 0))],
        out_specs=[pl.BlockSpec((tm, MEM_W), lambda i, j: (i, 0))] * 2,
        out_shape=[out, out],
        scratch_shapes=[pltpu.VMEM((tm, D), BF16)],
        compiler_params=_cparams(("arbitrary", "arbitrary")),
        name="mem_kv",
    )(mem, g_norm, w, g_mk)


def _mem_attn_kernel(q_ref, k_ref, v_ref, o_ref):
    for h in range(MEM_HEADS):
        hs = slice(h * HEAD_DIM, (h + 1) * HEAD_DIM)
        s = _dot_nt(q_ref[0, :, hs], k_ref[0, :, hs].astype(BF16))
        p = jnp.exp(s - jnp.max(s, axis=-1, keepdims=True))
        l = jnp.sum(p, axis=-1, keepdims=True)
        o_ref[0, :, hs] = (_dot(p.astype(BF16), v_ref[0, :, hs].astype(BF16)) / l).astype(o_ref.dtype)


def _mem_attn(mq, mk, mv, tq):
    B, L, _ = mq.shape
    M = mk.shape[1]
    kv = pl.BlockSpec((1, M, MEM_W), lambda b, i: (b, 0, 0))
    qo = pl.BlockSpec((1, tq, MEM_W), lambda b, i: (b, i, 0))
    return pl.pallas_call(
        _mem_attn_kernel,
        grid=(B, L // tq),
        in_specs=[qo, kv, kv],
        out_specs=qo,
        out_shape=jax.ShapeDtypeStruct((B, L, MEM_W), BF16),
        compiler_params=_cparams(("arbitrary", "arbitrary")),
        name="mem_attn",
    )(mq, mk, mv)


DEC_G = 4
DEC_ROWS = 4 * HEAD_PAD


def _suffix_sum_lanes(x):
    lane = lax.broadcasted_iota(jnp.int32, x.shape, 1)
    s = 1
    while s < LANES:
        x = x + jnp.where(lane + s < LANES, pltpu.roll(x, LANES - s, 1), 0.0)
        s *= 2
    return x


def _prefix_sum_lanes(x):
    lane = lax.broadcasted_iota(jnp.int32, x.shape, 1)
    s = 1
    while s < LANES:
        x = x + jnp.where(lane >= s, pltpu.roll(x, s, 1), 0.0)
        s *= 2
    return x


def _fox_dec_kernel(pt_ref, q_ref, kn_ref, vn_ref, lfn_ref, *refs, n_tok):
    G = DEC_G
    k_refs, v_refs, lf_refs = refs[0:G], refs[G:2 * G], refs[2 * G:3 * G]
    o_ref = refs[3 * G]
    qbd, kn_s, vn_s, m_s, l_s, acc_s, e_s, car_s = refs[3 * G + 1:]
    st = pl.program_id(1)

    def attend(k_bf, v_bf, bias):
        s = _dot_nt(qbd[...].astype(BF16), k_bf) + bias
        m_new, l_new, acc_new = _softmax_step(s, v_bf, m_s[...], l_s[...], acc_s[...])
        m_s[...] = m_new
        l_s[...] = l_new
        acc_s[...] = acc_new

    @pl.when(st == 0)
    def _():
        qbd[...] = jnp.zeros_like(qbd)
        q = q_ref[0].astype(F32)
        for t in range(n_tok):
            for h in range(FOX_HEADS):
                hs = slice(h * HEAD_DIM, (h + 1) * HEAD_DIM)
                qbd[t * HEAD_PAD + h:t * HEAD_PAD + h + 1, hs] = q[t:t + 1, hs]
        kn_s[...] = jnp.zeros_like(kn_s)
        vn_s[...] = jnp.zeros_like(vn_s)
        kn_s[0:SUBLANES, :] = kn_ref[0]
        vn_s[0:SUBLANES, :] = vn_ref[0]
        m_s[...] = jnp.full_like(m_s, -jnp.inf)
        l_s[...] = jnp.zeros_like(l_s)
        acc_s[...] = jnp.zeros_like(acc_s)
        car_s[...] = jnp.zeros_like(car_s)
        ecum = _prefix_sum_lanes(lfn_ref[0])
        lane = lax.broadcasted_iota(jnp.int32, (HEAD_PAD, LANES), 1)
        for t in range(n_tok):
            rows = slice(t * HEAD_PAD, (t + 1) * HEAD_PAD)
            e_t = ecum[:, t:t + 1]
            e_s[rows, :] = e_t
        bias = jnp.concatenate(
            [jnp.where(lane <= t, ecum[:, t:t + 1] - ecum, -jnp.inf) for t in range(n_tok)], axis=0)
        attend(kn_s[...].astype(BF16), vn_s[...].astype(BF16), bias)

    for i in range(G):
        lf = lf_refs[i][0]
        incl = _suffix_sum_lanes(lf)
        d = car_s[...] + (incl - lf)
        car_s[...] = car_s[...] + incl[:, 0:1]
        bias = jnp.concatenate([d + e_s[t * HEAD_PAD:(t + 1) * HEAD_PAD, :] for t in range(n_tok)], axis=0)
        attend(k_refs[i][0].astype(BF16), v_refs[i][0].astype(BF16), bias)

    @pl.when(st == pl.num_programs(1) - 1)
    def _():
        res = acc_s[...] / l_s[...]
        o_ref[...] = jnp.zeros_like(o_ref)
        for t in range(n_tok):
            for h in range(FOX_HEADS):
                hs = slice(h * HEAD_DIM, (h + 1) * HEAD_DIM)
                r = t * HEAD_PAD + h
                o_ref[0, t:t + 1, hs] = res[r:r + 1, hs].astype(o_ref.dtype)


def _fox_dec(page_table, fq, fk, fv, lf_new_t, cache_k, cache_v, cache_lf_t, n_tok):
    B, n_pages = page_table.shape
    G = DEC_G
    n_steps = n_pages // G
    tok = lambda w: pl.BlockSpec((1, SUBLANES, w), lambda b, s, pt: (b, 0, 0))

    def page_spec(shape, i):
        return pl.BlockSpec((1,) + shape, lambda b, s, pt: (pt[b * n_pages + (n_pages - 1 - (s * G + i))], 0, 0))

    in_specs = ([tok(FOX_W), tok(FOX_W), tok(FOX_W), tok(LANES)]
                + [page_spec((LANES, FOX_W), i) for i in range(G)]
                + [page_spec((LANES, FOX_W), i) for i in range(G)]
                + [page_spec((HEAD_PAD, LANES), i) for i in range(G)])
    grid_spec = pltpu.PrefetchScalarGridSpec(
        num_scalar_prefetch=1,
        grid=(B, n_steps),
        in_specs=in_specs,
        out_specs=pl.BlockSpec((1, SUBLANES, FOX_W), lambda b, s, pt: (b, 0, 0)),
        scratch_shapes=[pltpu.VMEM((DEC_ROWS, FOX_W), F32),
                        pltpu.VMEM((LANES, FOX_W), F32), pltpu.VMEM((LANES, FOX_W), F32),
                        pltpu.VMEM((DEC_ROWS, 1), F32), pltpu.VMEM((DEC_ROWS, 1), F32),
                        pltpu.VMEM((DEC_ROWS, FOX_W), F32),
                        pltpu.VMEM((DEC_ROWS, 1), F32), pltpu.VMEM((HEAD_PAD, 1), F32)],
    )
    return pl.pallas_call(
        functools.partial(_fox_dec_kernel, n_tok=n_tok),
        grid_spec=grid_spec,
        out_shape=jax.ShapeDtypeStruct((B, SUBLANES, FOX_W), BF16),
        compiler_params=_cparams(("arbitrary", "arbitrary")),
        name="fox_dec",
    )(page_table.reshape(-1), fq, fk, fv, lf_new_t, *([cache_k] * G), *([cache_v] * G), *([cache_lf_t] * G))


def _merge_kernel(fo_ref, ho_ref, mo_ref, x_ref, w_ref, gf_ref, wrh_ref, wrl_ref, br_ref,
                  h_ref, xn_ref, eid_ref, gate_ref):
    h = x_ref[...] + (_dot(fo_ref[...], w_ref[0:FOX_W, :])
                      + _dot(ho_ref[...], w_ref[FOX_W:FOX_W + HGRN_W, :])
                      + _dot(mo_ref[...], w_ref[FOX_W + HGRN_W:, :]))
    h_ref[...] = h
    ms = jnp.mean(h * h, axis=-1, keepdims=True)
    xn = h * lax.rsqrt(ms + EPS) * gf_ref[...]
    xn_ref[...] = xn
    x_hi = xn.astype(BF16)
    x_lo = (xn - x_hi.astype(F32)).astype(BF16)
    logits = (_dot(x_hi, wrh_ref[...]) + (_dot(x_hi, wrl_ref[...]) + _dot(x_lo, wrh_ref[...]))) + br_ref[...]
    lane = lax.broadcasted_iota(jnp.int32, logits.shape, 1)
    big = jnp.int32(LANES)
    ninf = -jnp.inf
    gl = jnp.where(lane < N_GROUPS, logits, ninf)
    gmax = jnp.max(gl, axis=-1, keepdims=True)
    g_sel = jnp.min(jnp.where(gl == gmax, lane, big), axis=-1, keepdims=True)
    g_prob = 1.0 / jnp.sum(jnp.exp(gl - gmax), axis=-1, keepdims=True)
    lo = N_GROUPS + EXPERTS_PER_GROUP * g_sel
    el = jnp.where((lane >= lo) & (lane < lo + EXPERTS_PER_GROUP), logits, ninf)
    v1 = jnp.max(el, axis=-1, keepdims=True)
    i1 = jnp.min(jnp.where(el == v1, lane, big), axis=-1, keepdims=True)
    el2 = jnp.where(lane == i1, ninf, el)
    v2 = jnp.max(el2, axis=-1, keepdims=True)
    i2 = jnp.min(jnp.where(el2 == v2, lane, big), axis=-1, keepdims=True)
    t = jnp.exp(v2 - v1)
    w1 = g_prob / (1.0 + t)
    w2 = g_prob * t / (1.0 + t)
    eid_ref[...] = jnp.where(lane == 0, i1 - N_GROUPS, jnp.where(lane == 1, i2 - N_GROUPS, 0))
    gate_ref[...] = jnp.where(lane == 0, w1, jnp.where(lane == 1, w2, 0.0))


def _merge(fo, ho, mo, x, w_out, g_ffn, wr_hi, wr_lo, b_r, tm):
    T, D = x.shape
    row = lambda w: pl.BlockSpec((tm, w), lambda i: (i, 0))
    full = lambda a: pl.BlockSpec(a.shape, lambda i: (0,) * a.ndim)
    return pl.pallas_call(
        _merge_kernel,
        grid=(T // tm,),
        in_specs=[row(FOX_W), row(HGRN_W), row(MEM_W), row(D), full(w_out), full(g_ffn),
                  full(wr_hi), full(wr_lo), full(b_r)],
        out_specs=[row(D), row(D), row(LANES), row(LANES)],
        out_shape=[jax.ShapeDtypeStruct((T, D), F32), jax.ShapeDtypeStruct((T, D), F32),
                   jax.ShapeDtypeStruct((T, LANES), jnp.int32), jax.ShapeDtypeStruct((T, LANES), F32)],
        compiler_params=_cparams(("arbitrary",)),
        name="merge",
    )(fo, ho, mo, x, w_out, g_ffn, wr_hi, wr_lo, b_r)


def _experts_kernel(be_ref, nb_ref, tok_ref, x_hbm, wg_ref, wu_ref, wd_ref, y_ref, xbuf, sem):
    b = pl.program_id(0)
    n_used = nb_ref[0]
    BM = MOE_BM

    def gather(blk, slot):
        base = blk * BM
        for r in range(BM):
            pltpu.make_async_copy(x_hbm.at[pl.ds(tok_ref[base + r], 1)], xbuf.at[slot, pl.ds(r, 1)],
                                  sem.at[slot]).start()

    def wait(slot):
        pltpu.make_async_copy(x_hbm.at[pl.ds(0, BM)], xbuf.at[slot], sem.at[slot]).wait()

    slot = lax.rem(b, 2)

    @pl.when(b == 0)
    def _():
        gather(0, 0)

    @pl.when(b + 1 < n_used)
    def _():
        gather(b + 1, 1 - slot)

    @pl.when(b < n_used)
    def _():
        wait(slot)
        x = xbuf[slot].astype(BF16)
        hmid = _dot(x, wg_ref[0])
        hmid = hmid * _sigmoid(hmid) * _dot(x, wu_ref[0])
        y_ref[...] = _dot(hmid.astype(BF16), wd_ref[0])

    @pl.when(b >= n_used)
    def _():
        y_ref[...] = jnp.zeros_like(y_ref)


def _experts(block_e, n_used, slot_tok, xn, w_gate, w_up, w_down):
    n_blocks = block_e.shape[0]
    T, D = xn.shape
    FF = w_gate.shape[2]
    BM = MOE_BM
    grid_spec = pltpu.PrefetchScalarGridSpec(
        num_scalar_prefetch=3,
        grid=(n_blocks,),
        in_specs=[pl.BlockSpec(memory_space=pl.ANY),
                  pl.BlockSpec((1, D, FF), lambda b, be, nb, tk: (be[b], 0, 0)),
                  pl.BlockSpec((1, D, FF), lambda b, be, nb, tk: (be[b], 0, 0)),
                  pl.BlockSpec((1, FF, D), lambda b, be, nb, tk: (be[b], 0, 0))],
        out_specs=pl.BlockSpec((BM, D), lambda b, be, nb, tk: (b, 0)),
        scratch_shapes=[pltpu.VMEM((2, BM, D), F32), pltpu.SemaphoreType.DMA((2,))],
    )
    return pl.pallas_call(
        _experts_kernel,
        grid_spec=grid_spec,
        out_shape=jax.ShapeDtypeStruct((n_blocks * BM, D), F32),
        compiler_params=_cparams(("arbitrary",)),
        name="experts",
    )(block_e, n_used, slot_tok, xn, w_gate, w_up, w_down)


COMB_TM = 128


def _combine_kernel(pos_ref, h_ref, gate_ref, y_hbm, o_ref, ybuf, sem):
    i = pl.program_id(0)
    n = pl.num_programs(0)
    TM = COMB_TM

    def gather(blk, slot):
        base = blk * (2 * TM)
        for r in range(2 * TM):
            pltpu.make_async_copy(y_hbm.at[pl.ds(pos_ref[base + r], 1)], ybuf.at[slot, pl.ds(r, 1)],
                                  sem.at[slot]).start()

    def wait(slot):
        pltpu.make_async_copy(y_hbm.at[pl.ds(0, 2 * TM)], ybuf.at[slot], sem.at[slot]).wait()

    slot = lax.rem(i, 2)

    @pl.when(i == 0)
    def _():
        gather(0, 0)

    @pl.when(i + 1 < n)
    def _():
        gather(i + 1, 1 - slot)

    wait(slot)
    g = gate_ref[...]
    o_ref[...] = h_ref[...] + (g[:, 0:1] * ybuf[slot, 0:TM, :] + g[:, 1:2] * ybuf[slot, TM:2 * TM, :])


def _combine(pos, h, gate, y_slots):
    T, D = h.shape
    TM = COMB_TM
    grid_spec = pltpu.PrefetchScalarGridSpec(
        num_scalar_prefetch=1,
        grid=(T // TM,),
        in_specs=[pl.BlockSpec((TM, D), lambda i, p: (i, 0)), pl.BlockSpec((TM, LANES), lambda i, p: (i, 0)),
                  pl.BlockSpec(memory_space=pl.ANY)],
        out_specs=pl.BlockSpec((TM, D), lambda i, p: (i, 0)),
        scratch_shapes=[pltpu.VMEM((2, 2 * TM, D), F32), pltpu.SemaphoreType.DMA((2,))],
    )
    return pl.pallas_call(
        _combine_kernel,
        grid_spec=grid_spec,
        out_shape=jax.ShapeDtypeStruct((T, D), F32),
        compiler_params=_cparams(("arbitrary",)),
        name="combine",
    )(pos, h, gate, y_slots)


def _moe(h, xn, eid, gate, w_gate, w_up, w_down):
    T = h.shape[0]
    A = T * TOP_K
    BM = MOE_BM
    n_blocks = -(-(A + N_EXPERTS * (BM - 1)) // BM)
    e_flat = eid[:, :TOP_K].reshape(A)
    onehot = (e_flat[:, None] == jnp.arange(N_EXPERTS, dtype=jnp.int32)[None, :]).astype(jnp.int32)
    csum = jnp.cumsum(onehot, axis=0)
    counts = csum[-1]
    rank = jnp.take_along_axis(csum, e_flat[:, None], axis=1)[:, 0] - 1
    padded = (counts + BM - 1) // BM * BM
    pad_end = jnp.cumsum(padded)
    pad_start = pad_end - padded
    pos = (pad_start[e_flat] + rank).astype(jnp.int32)
    tok = jnp.arange(A, dtype=jnp.int32) // TOP_K
    slot_tok = jnp.zeros((n_blocks * BM,), jnp.int32).at[pos].set(tok)
    block_first = jnp.arange(n_blocks, dtype=jnp.int32) * BM
    block_e = jnp.minimum(jnp.searchsorted(pad_end, block_first, side="right"), N_EXPERTS - 1).astype(jnp.int32)
    n_used = (pad_end[-1] // BM).astype(jnp.int32).reshape(1)
    y_slots = _experts(block_e, n_used, slot_tok, xn, w_gate, w_up, w_down)
    pos_tiles = pos.reshape(T // COMB_TM, COMB_TM, TOP_K).transpose(0, 2, 1).reshape(-1)
    return _combine(pos_tiles, h, gate, y_slots)


def _prep_weights(w_in, b_fox_f, w_router_group, b_router_group, w_router_expert, b_router_expert):
    D = w_in.shape[0]
    c = [0]
    for s in (FOX_W, FOX_W, FOX_W, FOX_HEADS, HGRN_W, HGRN_W, HGRN_W, HGRN_W, MEM_W):
        c.append(c[-1] + s)
    seg = lambda i: w_in[:, c[i]:c[i + 1]]
    last = jnp.concatenate([seg(8), seg(3), jnp.zeros((D, SEG_W - MEM_W - FOX_HEADS), w_in.dtype)], axis=1)
    w_seg = jnp.concatenate([seg(0), seg(1), seg(2), seg(4), seg(5), seg(6), seg(7), last], axis=1).astype(BF16)
    b_f_pad = jnp.zeros((1, LANES), F32).at[0, :FOX_HEADS].set(b_fox_f)
    n_r = N_GROUPS + N_EXPERTS
    w_r = jnp.zeros((D, LANES), F32).at[:, :N_GROUPS].set(w_router_group).at[:, N_GROUPS:n_r].set(w_router_expert)
    b_r = jnp.zeros((1, LANES), F32).at[0, :N_GROUPS].set(b_router_group).at[0, N_GROUPS:n_r].set(b_router_expert)
    wr_hi = w_r.astype(BF16)
    wr_lo = (w_r - wr_hi.astype(F32)).astype(BF16)
    return w_seg, b_f_pad, wr_hi, wr_lo, b_r


def kernel(x_prompt, x_sample, cache_fox_k, cache_fox_v, cache_fox_logf, cache_mem_k, cache_mem_v, state_hgrn, page_table, mem_prompt, g_attn_norm, w_in, b_fox_f, g_fox_q, g_fox_k, lb_logits, g_hgrn_out, g_mem_norm, w_mem_kv, g_mem_q, g_mem_k, w_out, g_ffn_norm, w_router_group, b_router_group, w_router_expert, b_router_expert, w_gate_e, w_up_e, w_down_e):
    assert w_in.shape[0] == 1, "single-layer step"
    Bp, S, D = x_prompt.shape
    Bd, L, _ = x_sample.shape
    n_pool, page = cache_fox_k.shape[1], cache_fox_k.shape[2]
    assert page == LANES and L <= SUBLANES
    M = mem_prompt.shape[1]
    l = 0
    row = lambda a: a[l].reshape(1, -1)

    w_seg, b_f_pad, wr_hi, wr_lo, b_r = _prep_weights(
        w_in[l], b_fox_f[l], w_router_group[l], b_router_group[l], w_router_expert[l], b_router_expert[l])
    w_out_bf = w_out[l].astype(BF16)
    w_mkv_bf = w_mem_kv[l].astype(BF16)
    wg_bf, wu_bf, wd_bf = w_gate_e[l].astype(BF16), w_up_e[l].astype(BF16), w_down_e[l].astype(BF16)
    proj_args = (row(g_attn_norm), w_seg, row(g_fox_q), row(g_fox_k), row(g_mem_q), lb_logits, b_f_pad)

    Tp = Bp * S
    xp = x_prompt.reshape(Tp, D)
    fq, fk_p, fv_p, hq, hlf, hv, hgs, mq, flf_p = _in_proj(xp, *proj_args, tm=512)
    seq = lambda a: a.reshape(Bp, S, a.shape[-1])
    cq, ck = _fox_cum(seq(flf_p))
    fox_o = _fox_attn(seq(fq), seq(fk_p), seq(fv_p), cq, ck)
    hg_o, s_p = _hgrn(seq(hq), seq(hlf), seq(hv), seq(hgs), row(g_hgrn_out), None, C=128, n_heads=1, n_valid=128)
    mk, mv = _mem_kv(mem_prompt.reshape(Bp * M, D), row(g_mem_norm), w_mkv_bf, row(g_mem_k), tm=256)
    mem_o = _mem_attn(seq(mq), mk.reshape(Bp, M, MEM_W), mv.reshape(Bp, M, MEM_W), tq=512)
    h, xn, eid, gate = _merge(fox_o.reshape(Tp, FOX_W), hg_o.reshape(Tp, HGRN_W), mem_o.reshape(Tp, MEM_W),
                              xp, w_out_bf, row(g_ffn_norm), wr_hi, wr_lo, b_r, tm=256)
    y_p = _moe(h, xn, eid, gate, wg_bf, wu_bf, wd_bf)

    R = SUBLANES
    Ts = Bd * R
    xs = jnp.pad(x_sample, ((0, 0), (0, R - L), (0, 0))).reshape(Ts, D)
    fq, fk_s, fv_s, hq, hlf, hv, hgs, mq, flf_s = _in_proj(xs, *proj_args, tm=Ts)
    seqs = lambda a: a.reshape(Bd, R, a.shape[-1])
    lf_new_t = jnp.swapaxes(seqs(flf_s)[:, :, :HEAD_PAD], 1, 2)
    lf_new_t = jnp.pad(lf_new_t, ((0, 0), (0, 0), (0, LANES - R)))
    cache_lf_t = jnp.pad(jnp.swapaxes(cache_fox_logf[l].astype(F32), 1, 2), ((0, 0), (0, HEAD_PAD - FOX_HEADS), (0, 0)))
    fox_o = _fox_dec(page_table, seqs(fq), seqs(fk_s), seqs(fv_s), lf_new_t,
                     cache_fox_k[l].reshape(n_pool, page, FOX_W), cache_fox_v[l].reshape(n_pool, page, FOX_W),
                     cache_lf_t, n_tok=L)
    hg_o, s_s = _hgrn(seqs(hq), seqs(hlf), seqs(hv), seqs(hgs), row(g_hgrn_out), state_hgrn[l],
                      C=R, n_heads=HGRN_HEADS, n_valid=L)
    mem_o = _mem_attn(seqs(mq), cache_mem_k[l].reshape(Bd, M, MEM_W), cache_mem_v[l].reshape(Bd, M, MEM_W), tq=R)
    h, xn, eid, gate = _merge(fox_o.reshape(Ts, FOX_W), hg_o.reshape(Ts, HGRN_W), mem_o.reshape(Ts, MEM_W),
                              xs, w_out_bf, row(g_ffn_norm), wr_hi, wr_lo, b_r, tm=Ts)
    y_s = _moe(h, xn, eid, gate, wg_bf, wu_bf, wd_bf)

    cut = lambda a: a.reshape(Bd, R, a.shape[-1])[:, :L]
    return (y_p.reshape(Bp, S, D), cut(y_s),
            fk_p.reshape(1, Bp, S, FOX_HEADS, HEAD_DIM), fv_p.reshape(1, Bp, S, FOX_HEADS, HEAD_DIM),
            flf_p[:, :FOX_HEADS].reshape(1, Bp, S, FOX_HEADS), s_p[None],
            mk.reshape(1, Bp, M, MEM_HEADS, HEAD_DIM), mv.reshape(1, Bp, M, MEM_HEADS, HEAD_DIM),
            cut(fk_s).reshape(1, Bd, L, FOX_HEADS, HEAD_DIM), cut(fv_s).reshape(1, Bd, L, FOX_HEADS, HEAD_DIM),
            cut(flf_s)[:, :, :FOX_HEADS][None], s_s[None])
```

```python
import functools

import jax
import jax.numpy as jnp
from jax import lax
from jax.experimental import pallas as pl
from jax.experimental.pallas import tpu as pltpu

F32 = jnp.float32
BF16 = jnp.bfloat16

HEAD_DIM = 128
FOX_HEADS = 6
HGRN_HEADS = 6
MEM_HEADS = 4
FOX_W = FOX_HEADS * HEAD_DIM
HGRN_W = HGRN_HEADS * HEAD_DIM
MEM_W = MEM_HEADS * HEAD_DIM
N_GROUPS = 4
EXPERTS_PER_GROUP = 8
N_EXPERTS = N_GROUPS * EXPERTS_PER_GROUP
TOP_K = 2
EPS = 1e-6
ATTN_SCALE = HEAD_DIM ** -0.5
LANES = 128
SUBLANES = 8
SEG_W = FOX_W
N_SEG = 8
HEAD_PAD = 8
MOE_BM = 256
VMEM_LIMIT = 52 * 1024 * 1024

NT_DIMS = (((1,), (1,)), ((), ()))
TN_DIMS = (((0,), (0,)), ((), ()))


def _cparams(sem, vmem=VMEM_LIMIT):
    return pltpu.CompilerParams(dimension_semantics=sem, vmem_limit_bytes=vmem)


def _split3(x):
    hi = x.astype(BF16)
    r1 = x - hi.astype(F32)
    mid = r1.astype(BF16)
    lo = (r1 - mid.astype(F32)).astype(BF16)
    return hi, mid, lo


def _dot(a, b):
    return jnp.dot(a, b, preferred_element_type=F32)


def _dot_nt(a, b):
    return lax.dot_general(a, b, NT_DIMS, preferred_element_type=F32)


def _sigmoid(x):
    return 1.0 / (1.0 + jnp.exp(-x))


def _log_sigmoid(x):
    return jnp.minimum(x, 0.0) - jnp.log(1.0 + jnp.exp(-jnp.abs(x)))


def _rms_heads(a, g, n_heads, scale):
    outs = []
    for h in range(n_heads):
        ah = a[:, h * HEAD_DIM:(h + 1) * HEAD_DIM]
        ms = jnp.mean(ah * ah, axis=-1, keepdims=True)
        outs.append(ah * lax.rsqrt(ms + EPS) * (g * scale))
    return jnp.concatenate(outs, axis=-1)


def _store_heads(ref, a, n_heads):
    for h in range(n_heads):
        ref[:, h, :] = a[:, h * HEAD_DIM:(h + 1) * HEAD_DIM]


def _in_proj_kernel(x_ref, gn_ref, w_ref, gfq_ref, gfk_ref, gmq_ref, lbl_ref, bf_ref,
                    fq_ref, fk_ref, fkb_ref, fv_ref, fvb_ref, hq_ref, hlf_ref, hv_ref, hgs_ref, mq_ref, flf_ref,
                    xn_s):
    j = pl.program_id(1)

    @pl.when(j == 0)
    def _():
        x = x_ref[...]
        ms = jnp.mean(x * x, axis=-1, keepdims=True)
        xn_s[...] = (x * lax.rsqrt(ms + EPS) * gn_ref[...]).astype(BF16)

    acc = _dot(xn_s[...], w_ref[...])

    @pl.when(j == 0)
    def _():
        fq_ref[...] = _rms_heads(acc, gfq_ref[...], FOX_HEADS, ATTN_SCALE).astype(fq_ref.dtype)

    @pl.when(j == 1)
    def _():
        fk = _rms_heads(acc, gfk_ref[...], FOX_HEADS, 1.0)
        _store_heads(fk_ref, fk, FOX_HEADS)
        fkb_ref[...] = fk.astype(BF16)

    @pl.when(j == 2)
    def _():
        _store_heads(fv_ref, acc, FOX_HEADS)
        fvb_ref[...] = acc.astype(BF16)

    @pl.when(j == 3)
    def _():
        hq_ref[...] = acc * _sigmoid(acc) * ATTN_SCALE

    @pl.when(j == 4)
    def _():
        l = lbl_ref[...]
        mx = jnp.max(l, axis=0, keepdims=True)
        ex = jnp.exp(l - mx)
        lb = ex[0:1, :] / jnp.sum(ex, axis=0, keepdims=True)
        hlf_ref[...] = jnp.log(lb + (1.0 - lb) * _sigmoid(acc))

    @pl.when(j == 5)
    def _():
        hv_ref[...] = acc

    @pl.when(j == 6)
    def _():
        hgs_ref[...] = acc * _sigmoid(acc)

    @pl.when(j == 7)
    def _():
        mq_ref[...] = _rms_heads(acc[:, :MEM_W], gmq_ref[...], MEM_HEADS, ATTN_SCALE).astype(mq_ref.dtype)
        flf_ref[...] = _log_sigmoid(acc[:, MEM_W:MEM_W + LANES] + bf_ref[...])


def _in_proj(x, g_norm, w_seg, g_fq, g_fk, g_mq, lb_logits, b_f_pad, tm):
    T, D = x.shape
    full = lambda a: pl.BlockSpec(a.shape, lambda i, j: (0,) * a.ndim)
    outs = [
        jax.ShapeDtypeStruct((T, FOX_W), BF16),
        jax.ShapeDtypeStruct((T, FOX_HEADS, HEAD_DIM), F32),
        jax.ShapeDtypeStruct((T, FOX_W), BF16),
        jax.ShapeDtypeStruct((T, FOX_HEADS, HEAD_DIM), F32),
        jax.ShapeDtypeStruct((T, FOX_W), BF16),
        jax.ShapeDtypeStruct((T, HGRN_W), F32),
        jax.ShapeDtypeStruct((T, HGRN_W), F32),
        jax.ShapeDtypeStruct((T, HGRN_W), F32),
        jax.ShapeDtypeStruct((T, HGRN_W), F32),
        jax.ShapeDtypeStruct((T, MEM_W), BF16),
        jax.ShapeDtypeStruct((T, LANES), F32),
    ]
    return pl.pallas_call(
        _in_proj_kernel,
        grid=(T // tm, N_SEG),
        in_specs=[pl.BlockSpec((tm, D), lambda i, j: (i, 0)), full(g_norm),
                  pl.BlockSpec((D, SEG_W), lambda i, j: (0, j)),
                  full(g_fq), full(g_fk), full(g_mq), full(lb_logits), full(b_f_pad)],
        out_specs=[pl.BlockSpec((tm,) + o.shape[1:], lambda i, j, nd=len(o.shape): (i,) + (0,) * (nd - 1))
                   for o in outs],
        out_shape=outs,
        scratch_shapes=[pltpu.VMEM((tm, D), BF16)],
        compiler_params=_cparams(("arbitrary", "arbitrary")),
        name="in_proj",
    )(x, g_norm, w_seg, g_fq, g_fk, g_mq, lb_logits, b_f_pad)


CUM_C = 256


def _fox_cum_kernel(lf_ref, ck_ref):
    S = lf_ref.shape[1]
    r = lax.broadcasted_iota(jnp.int32, (CUM_C, CUM_C), 0)
    c = lax.broadcasted_iota(jnp.int32, (CUM_C, CUM_C), 1)
    tri = jnp.where(c <= r, 1.0, 0.0).astype(BF16)
    carry = jnp.zeros((1, LANES), F32)
    for i in range(S // CUM_C):
        sl = slice(i * CUM_C, (i + 1) * CUM_C)
        hi, mid, lo = _split3(lf_ref[0, sl, :])
        cum = (_dot(tri, hi) + _dot(tri, mid)) + _dot(tri, lo) + carry
        carry = cum[CUM_C - 1:CUM_C, :]
        ck_ref[0, :, sl] = cum.T[0:HEAD_PAD, :]


def _fox_cum(flf):
    B, S, _ = flf.shape
    return pl.pallas_call(
        _fox_cum_kernel,
        grid=(B,),
        in_specs=[pl.BlockSpec((1, S, LANES), lambda b: (b, 0, 0))],
        out_specs=pl.BlockSpec((1, HEAD_PAD, S), lambda b: (b, 0, 0)),
        out_shape=jax.ShapeDtypeStruct((B, HEAD_PAD, S), F32),
        compiler_params=_cparams(("arbitrary",)),
        name="fox_cum",
    )(flf)


FOX_T = 256


def _softmax_step(s, v_bf, m, l, acc):
    m_new = jnp.maximum(m, jnp.max(s, axis=-1, keepdims=True))
    alpha = jnp.exp(m - m_new)
    p = jnp.exp(s - m_new)
    l = alpha * l + jnp.sum(p, axis=-1, keepdims=True)
    acc = alpha * acc + _dot(p.astype(BF16), v_bf)
    return m_new, l, acc


def _fox_attn_kernel(q_ref, k_ref, v_ref, ck_ref, o_ref, m_s, l_s, acc_s):
    qi = pl.program_id(1)
    T = FOX_T
    m_s[...] = jnp.full_like(m_s, -jnp.inf)
    l_s[...] = jnp.zeros_like(l_s)
    acc_s[...] = jnp.zeros_like(acc_s)
    row = lax.broadcasted_iota(jnp.int32, (T, T), 0)
    col = lax.broadcasted_iota(jnp.int32, (T, T), 1)
    causal = col <= row

    def tile(ks, masked):
        for h in range(FOX_HEADS):
            hs = slice(h * HEAD_DIM, (h + 1) * HEAD_DIM)
            s = _dot_nt(q_ref[0, :, hs], k_ref[0, pl.ds(ks, T), hs]) - ck_ref[0, h:h + 1, pl.ds(ks, T)]
            if masked:
                s = jnp.where(causal, s, -jnp.inf)
            m, l, acc = _softmax_step(s, v_ref[0, pl.ds(ks, T), hs], m_s[:, h:h + 1], l_s[:, h:h + 1], acc_s[:, hs])
            m_s[:, h:h + 1] = m
            l_s[:, h:h + 1] = l
            acc_s[:, hs] = acc

    def body(kt, c):
        tile(pl.multiple_of(kt * T, T), False)
        return c

    lax.fori_loop(0, qi, body, 0)
    tile(pl.multiple_of(qi * T, T), True)
    for h in range(FOX_HEADS):
        hs = slice(h * HEAD_DIM, (h + 1) * HEAD_DIM)
        o_ref[0, :, hs] = (acc_s[:, hs] / l_s[:, h:h + 1]).astype(o_ref.dtype)


def _fox_attn(fq, fk, fv, ck):
    B, S, _ = fq.shape
    T = FOX_T
    return pl.pallas_call(
        _fox_attn_kernel,
        grid=(B, S // T),
        in_specs=[pl.BlockSpec((1, T, FOX_W), lambda b, i: (b, i, 0)),
                  pl.BlockSpec((1, S, FOX_W), lambda b, i: (b, 0, 0)),
                  pl.BlockSpec((1, S, FOX_W), lambda b, i: (b, 0, 0)),
                  pl.BlockSpec((1, HEAD_PAD, S), lambda b, i: (b, 0, 0))],
        out_specs=pl.BlockSpec((1, T, FOX_W), lambda b, i: (b, i, 0)),
        out_shape=jax.ShapeDtypeStruct((B, S, FOX_W), BF16),
        scratch_shapes=[pltpu.VMEM((T, LANES), F32), pltpu.VMEM((T, LANES), F32), pltpu.VMEM((T, FOX_W), F32)],
        compiler_params=_cparams(("arbitrary", "arbitrary")),
        name="fox_attn",
    )(fq, fk, fv, ck)


def _level_ref(b, m):
    C = b.shape[0]
    if 2 * m >= SUBLANES:
        b3 = b.reshape(C // (2 * m), 2 * m, HEAD_DIM)
        r = jnp.broadcast_to(b3[:, m - 1:m, :], b3.shape)
        return r.reshape(C, HEAD_DIM)
    b3 = b.reshape(C // SUBLANES, SUBLANES, HEAD_DIM)
    sub = lax.broadcasted_iota(jnp.int32, b3.shape, 1)
    pick = lambda i: jnp.broadcast_to(b3[:, i:i + 1, :], b3.shape)
    if m == 2:
        r = jnp.where(sub < 4, pick(1), pick(5))
    else:
        r = jnp.where(sub < 2, pick(0), jnp.where(sub < 4, pick(2), jnp.where(sub < 6, pick(4), pick(6))))
    return r.reshape(C, HEAD_DIM)


def _level_index(C):
    t = lax.broadcasted_iota(jnp.int32, (C, C), 0)
    s = lax.broadcasted_iota(jnp.int32, (C, C), 1)
    x = jnp.bitwise_xor(t, s)
    lvl = jnp.full((C, C), -1, jnp.int32)
    j, m = 0, 1
    while m < C:
        lvl = jnp.where(x >= m, j, lvl)
        j, m = j + 1, 2 * m
    return jnp.where(t > s, lvl, -1)


def _hgrn_chunk(q, g, v, st, tri, lvl, n_valid):
    C = q.shape[0]
    hi, mid, lo = _split3(g)
    b = (_dot(tri, hi) + _dot(tri, mid)) + _dot(tri, lo)
    k = 1.0 - jnp.exp(g)
    a = jnp.zeros((C, C), F32)
    j, m = 0, 1
    while m < C:
        e = jnp.exp(-jnp.abs(b - _level_ref(b, m)))
        a_l = _dot_nt((q * e).astype(BF16), (k * e).astype(BF16))
        a = jnp.where(lvl == j, a_l, a)
        j, m = j + 1, 2 * m
    v_bf = v.astype(BF16)
    diag = jnp.sum(q * k, axis=-1, keepdims=True)
    o = _dot_nt((q * jnp.exp(b)).astype(BF16), st.astype(BF16)) + _dot(a.astype(BF16), v_bf) + diag * v
    b_last = b[n_valid - 1:n_valid, :]
    kt = k * jnp.exp(jnp.minimum(b_last - b, 0.0))
    if n_valid < C:
        rows = lax.broadcasted_iota(jnp.int32, (C, HEAD_DIM), 0)
        kt = jnp.where(rows < n_valid, kt, 0.0)
    st_new = st * jnp.exp(b_last) + lax.dot_general(v_bf, kt.astype(BF16), TN_DIMS, preferred_element_type=F32)
    return o, st_new


def _hgrn_kernel(*refs, C, n_chunks, n_heads, n_valid, has_s0):
    if has_s0:
        q_ref, g_ref, v_ref, gs_ref, gn_ref, s0_ref, o_ref, sf_ref = refs
    else:
        q_ref, g_ref, v_ref, gs_ref, gn_ref, o_ref, sf_ref = refs
    r = lax.broadcasted_iota(jnp.int32, (C, C), 0)
    c = lax.broadcasted_iota(jnp.int32, (C, C), 1)
    tri = jnp.where(c <= r, 1.0, 0.0).astype(BF16)
    lvl = _level_index(C)
    gn = gn_ref[...]
    for h in range(n_heads):
        hs = slice(h * HEAD_DIM, (h + 1) * HEAD_DIM)
        st0 = s0_ref[0, h].T if has_s0 else jnp.zeros((HEAD_DIM, HEAD_DIM), F32)

        def body(ci, st):
            rs = pl.ds(pl.multiple_of(ci * C, C), C)
            o, st = _hgrn_chunk(q_ref[0, rs, hs], g_ref[0, rs, hs], v_ref[0, rs, hs], st, tri, lvl, n_valid)
            ms = jnp.mean(o * o, axis=-1, keepdims=True)
            o_ref[0, rs, hs] = (o * lax.rsqrt(ms + EPS) * gn * gs_ref[0, rs, hs]).astype(o_ref.dtype)
            return st

        st = lax.fori_loop(0, n_chunks, body, st0) if n_chunks > 1 else body(0, st0)
        sf_ref[0, h] = st.T


def _hgrn(hq, hlf, hv, hgs, g_hn, s0, C, n_heads, n_valid):
    B, L, _ = hq.shape
    hp = HGRN_HEADS // n_heads
    w = n_heads * HEAD_DIM
    seq = pl.BlockSpec((1, L, w), lambda b, h: (b, 0, h))
    st_spec = pl.BlockSpec((1, n_heads, HEAD_DIM, HEAD_DIM), lambda b, h: (b, h, 0, 0))
    in_specs = [seq, seq, seq, seq, pl.BlockSpec((1, HEAD_DIM), lambda b, h: (0, 0))]
    args = [hq, hlf, hv, hgs, g_hn]
    if s0 is not None:
        in_specs.append(st_spec)
        args.append(s0)
    kern = functools.partial(_hgrn_kernel, C=C, n_chunks=L // C, n_heads=n_heads, n_valid=n_valid,
                             has_s0=s0 is not None)
    return pl.pallas_call(
        kern,
        grid=(B, hp),
        in_specs=in_specs,
        out_specs=[seq, st_spec],
        out_shape=[jax.ShapeDtypeStruct((B, L, HGRN_W), BF16),
                   jax.ShapeDtypeStruct((B, HGRN_HEADS, HEAD_DIM, HEAD_DIM), F32)],
        compiler_params=_cparams(("arbitrary", "arbitrary")),
        name="hgrn",
    )(*args)


def _mem_kv_kernel(x_ref, gn_ref, w_ref, gk_ref, mk_ref, mv_ref, xn_s):
    j = pl.program_id(1)

    @pl.when(j == 0)
    def _():
        x = x_ref[...]
        ms = jnp.mean(x * x, axis=-1, keepdims=True)
        xn_s[...] = (x * lax.rsqrt(ms + EPS) * gn_ref[...]).astype(BF16)

    acc = _dot(xn_s[...], w_ref[...])

    @pl.when(j == 0)
    def _():
        _store_heads(mk_ref, _rms_heads(acc, gk_ref[...], MEM_HEADS, 1.0), MEM_HEADS)

    @pl.when(j == 1)
    def _():
        _store_heads(mv_ref, acc, MEM_HEADS)


def _mem_kv(mem, g_norm, w, g_mk, tm):
    T, D = mem.shape
    out = jax.ShapeDtypeStruct((T, MEM_HEADS, HEAD_DIM), F32)
    return pl.pallas_call(
        _mem_kv_kernel,
        grid=(T // tm, 2),
        in_specs=[pl.BlockSpec((tm, D), lambda i, j: (i, 0)), pl.BlockSpec((1, D), lambda i, j: (0, 0)),
                  pl.BlockSpec((D, MEM_W), lambda i, j: (0, j)), pl.BlockSpec((1, HEAD_DIM), lambda i, j: (0, 0))],
        out_specs=[pl.BlockSpec((tm, MEM_HEADS, HEAD_DIM), lambda i, j: (i, 0, 0))] * 2,
        out_shape=[out, out],
        scratch_shapes=[pltpu.VMEM((tm, D), BF16)],
        compiler_params=_cparams(("arbitrary", "arbitrary")),
        name="mem_kv",
    )(mem, g_norm, w, g_mk)


def _mem_attn_kernel(q_ref, k_ref, v_ref, o_ref):
    for h in range(MEM_HEADS):
        hs = slice(h * HEAD_DIM, (h + 1) * HEAD_DIM)
        s = _dot_nt(q_ref[0, :, hs], k_ref[0, :, h, :].astype(BF16))
        p = jnp.exp(s - jnp.max(s, axis=-1, keepdims=True))
        l = jnp.sum(p, axis=-1, keepdims=True)
        o_ref[0, :, hs] = (_dot(p.astype(BF16), v_ref[0, :, h, :].astype(BF16)) / l).astype(o_ref.dtype)


def _mem_attn(mq, mk, mv, tq):
    B, L, _ = mq.shape
    M = mk.shape[1]
    kv = pl.BlockSpec((1, M, MEM_HEADS, HEAD_DIM), lambda b, i: (b, 0, 0, 0))
    qo = pl.BlockSpec((1, tq, MEM_W), lambda b, i: (b, i, 0))
    return pl.pallas_call(
        _mem_attn_kernel,
        grid=(B, L // tq),
        in_specs=[qo, kv, kv],
        out_specs=qo,
        out_shape=jax.ShapeDtypeStruct((B, L, MEM_W), BF16),
        compiler_params=_cparams(("arbitrary", "arbitrary")),
        name="mem_attn",
    )(mq, mk, mv)


DEC_G = 8
DEC_ROWS = 4 * HEAD_PAD


def _suffix_sum_lanes(x):
    lane = lax.broadcasted_iota(jnp.int32, x.shape, 1)
    s = 1
    while s < LANES:
        x = x + jnp.where(lane + s < LANES, pltpu.roll(x, LANES - s, 1), 0.0)
        s *= 2
    return x


def _prefix_sum_lanes(x):
    lane = lax.broadcasted_iota(jnp.int32, x.shape, 1)
    s = 1
    while s < LANES:
        x = x + jnp.where(lane >= s, pltpu.roll(x, s, 1), 0.0)
        s *= 2
    return x


def _fox_dec_kernel(pt_ref, q_ref, kn_ref, vn_ref, lfn_ref, *refs, n_tok):
    G = DEC_G
    k_refs, v_refs, lf_refs = refs[0:G], refs[G:2 * G], refs[2 * G:3 * G]
    o_ref = refs[3 * G]
    qbd, kn_s, vn_s, m_s, l_s, acc_s, car_s = refs[3 * G + 1:]
    st = pl.program_id(1)

    def attend(k_bf, v_bf, bias):
        s = _dot_nt(qbd[...].astype(BF16), k_bf) + bias
        m_new, l_new, acc_new = _softmax_step(s, v_bf, m_s[...], l_s[...], acc_s[...])
        m_s[...] = m_new
        l_s[...] = l_new
        acc_s[...] = acc_new

    @pl.when(st == 0)
    def _():
        qbd[...] = jnp.zeros_like(qbd)
        q = q_ref[0].astype(F32)
        for t in range(n_tok):
            for h in range(FOX_HEADS):
                hs = slice(h * HEAD_DIM, (h + 1) * HEAD_DIM)
                qbd[t * HEAD_PAD + h:t * HEAD_PAD + h + 1, hs] = q[t:t + 1, hs]
        kn_s[...] = jnp.zeros_like(kn_s)
        vn_s[...] = jnp.zeros_like(vn_s)
        kn_s[0:SUBLANES, :] = kn_ref[0].astype(F32)
        vn_s[0:SUBLANES, :] = vn_ref[0].astype(F32)
        m_s[...] = jnp.full_like(m_s, -jnp.inf)
        l_s[...] = jnp.zeros_like(l_s)
        acc_s[...] = jnp.zeros_like(acc_s)
        car_s[...] = jnp.zeros_like(car_s)
        ecum = _prefix_sum_lanes(lfn_ref[0])
        lane = lax.broadcasted_iota(jnp.int32, (HEAD_PAD, LANES), 1)
        bias = jnp.concatenate([jnp.where(lane <= t, -ecum, -jnp.inf) for t in range(n_tok)], axis=0)
        attend(kn_s[...].astype(BF16), vn_s[...].astype(BF16), bias)

    heads = lambda ref: jnp.concatenate([ref[0, :, h, :] for h in range(FOX_HEADS)], axis=-1).astype(BF16)
    carry = car_s[...]
    ds = []
    for i in range(G):
        lf = lf_refs[i][0]
        incl = _suffix_sum_lanes(lf)
        ds.append(carry + (incl - lf))
        carry = carry + incl[:, 0:1]
    car_s[...] = carry
    d_all = jnp.concatenate(ds, axis=-1)
    attend(jnp.concatenate([heads(r) for r in k_refs], axis=0),
           jnp.concatenate([heads(r) for r in v_refs], axis=0),
           jnp.concatenate([d_all] * n_tok, axis=0))

    @pl.when(st == pl.num_programs(1) - 1)
    def _():
        res = acc_s[...] / l_s[...]
        o_ref[...] = jnp.zeros_like(o_ref)
        for t in range(n_tok):
            for h in range(FOX_HEADS):
                hs = slice(h * HEAD_DIM, (h + 1) * HEAD_DIM)
                r = t * HEAD_PAD + h
                o_ref[0, t:t + 1, hs] = res[r:r + 1, hs].astype(o_ref.dtype)


def _fox_dec(page_table, fq, fk, fv, lf_new_t, cache_k, cache_v, cache_lf_t, n_tok):
    B, n_pages = page_table.shape
    G = DEC_G
    assert n_pages % G == 0 and n_tok * HEAD_PAD == DEC_ROWS
    n_steps = n_pages // G
    tok = lambda w: pl.BlockSpec((1, SUBLANES, w), lambda b, s, pt: (b, 0, 0))

    def page_spec(shape, i):
        return pl.BlockSpec((1,) + shape,
                            lambda b, s, pt: (pt[b * n_pages + (n_pages - 1 - (s * G + i))],) + (0,) * len(shape))

    kv_page = (LANES, FOX_HEADS, HEAD_DIM)
    in_specs = ([tok(FOX_W), tok(FOX_W), tok(FOX_W), tok(LANES)]
                + [page_spec(kv_page, i) for i in range(G)]
                + [page_spec(kv_page, i) for i in range(G)]
                + [page_spec((HEAD_PAD, LANES), i) for i in range(G)])
    grid_spec = pltpu.PrefetchScalarGridSpec(
        num_scalar_prefetch=1,
        grid=(B, n_steps),
        in_specs=in_specs,
        out_specs=pl.BlockSpec((1, SUBLANES, FOX_W), lambda b, s, pt: (b, 0, 0)),
        scratch_shapes=[pltpu.VMEM((DEC_ROWS, FOX_W), F32),
                        pltpu.VMEM((LANES, FOX_W), F32), pltpu.VMEM((LANES, FOX_W), F32),
                        pltpu.VMEM((DEC_ROWS, 1), F32), pltpu.VMEM((DEC_ROWS, 1), F32),
                        pltpu.VMEM((DEC_ROWS, FOX_W), F32), pltpu.VMEM((HEAD_PAD, 1), F32)],
    )
    return pl.pallas_call(
        functools.partial(_fox_dec_kernel, n_tok=n_tok),
        grid_spec=grid_spec,
        out_shape=jax.ShapeDtypeStruct((B, SUBLANES, FOX_W), BF16),
        compiler_params=_cparams(("arbitrary", "arbitrary")),
        name="fox_dec",
    )(page_table.reshape(-1), fq, fk, fv, lf_new_t, *([cache_k] * G), *([cache_v] * G), *([cache_lf_t] * G))


def _merge_kernel(fo_ref, ho_ref, mo_ref, x_ref, w_ref, gf_ref, wrh_ref, wrl_ref, br_ref,
                  h_ref, xn_ref, eid_ref, gate_ref):
    h = x_ref[...] + (_dot(fo_ref[...], w_ref[0:FOX_W, :])
                      + _dot(ho_ref[...], w_ref[FOX_W:FOX_W + HGRN_W, :])
                      + _dot(mo_ref[...], w_ref[FOX_W + HGRN_W:, :]))
    h_ref[...] = h
    ms = jnp.mean(h * h, axis=-1, keepdims=True)
    xn = h * lax.rsqrt(ms + EPS) * gf_ref[...]
    xn_ref[...] = xn
    x_hi = xn.astype(BF16)
    x_lo = (xn - x_hi.astype(F32)).astype(BF16)
    logits = (_dot(x_hi, wrh_ref[...]) + (_dot(x_hi, wrl_ref[...]) + _dot(x_lo, wrh_ref[...]))) + br_ref[...]
    lane = lax.broadcasted_iota(jnp.int32, logits.shape, 1)
    big = jnp.int32(LANES)
    ninf = -jnp.inf
    gl = jnp.where(lane < N_GROUPS, logits, ninf)
    gmax = jnp.max(gl, axis=-1, keepdims=True)
    g_sel = jnp.min(jnp.where(gl == gmax, lane, big), axis=-1, keepdims=True)
    g_prob = 1.0 / jnp.sum(jnp.exp(gl - gmax), axis=-1, keepdims=True)
    lo = N_GROUPS + EXPERTS_PER_GROUP * g_sel
    el = jnp.where((lane >= lo) & (lane < lo + EXPERTS_PER_GROUP), logits, ninf)
    v1 = jnp.max(el, axis=-1, keepdims=True)
    i1 = jnp.min(jnp.where(el == v1, lane, big), axis=-1, keepdims=True)
    el2 = jnp.where(lane == i1, ninf, el)
    v2 = jnp.max(el2, axis=-1, keepdims=True)
    i2 = jnp.min(jnp.where(el2 == v2, lane, big), axis=-1, keepdims=True)
    t = jnp.exp(v2 - v1)
    w1 = g_prob / (1.0 + t)
    w2 = g_prob * t / (1.0 + t)
    eid_ref[...] = jnp.where(lane == 0, i1 - N_GROUPS, jnp.where(lane == 1, i2 - N_GROUPS, 0))
    gate_ref[...] = jnp.where(lane == 0, w1, jnp.where(lane == 1, w2, 0.0))


def _merge(fo, ho, mo, x, w_out, g_ffn, wr_hi, wr_lo, b_r, tm):
    T, D = x.shape
    row = lambda w: pl.BlockSpec((tm, w), lambda i: (i, 0))
    full = lambda a: pl.BlockSpec(a.shape, lambda i: (0,) * a.ndim)
    return pl.pallas_call(
        _merge_kernel,
        grid=(T // tm,),
        in_specs=[row(FOX_W), row(HGRN_W), row(MEM_W), row(D), full(w_out), full(g_ffn),
                  full(wr_hi), full(wr_lo), full(b_r)],
        out_specs=[row(D), row(D), row(LANES), row(LANES)],
        out_shape=[jax.ShapeDtypeStruct((T, D), F32), jax.ShapeDtypeStruct((T, D), F32),
                   jax.ShapeDtypeStruct((T, LANES), jnp.int32), jax.ShapeDtypeStruct((T, LANES), F32)],
        compiler_params=_cparams(("arbitrary",)),
        name="merge",
    )(fo, ho, mo, x, w_out, g_ffn, wr_hi, wr_lo, b_r)


def _experts_kernel(be_ref, nb_ref, tok_ref, x_hbm, wg_ref, wu_ref, wd_ref, y_ref, xbuf, wg_s, wu_s, wd_s, sem):
    b = pl.program_id(0)

    @pl.when((b == 0) | (be_ref[b] != be_ref[jnp.maximum(b - 1, 0)]))
    def _():
        wg_s[...] = wg_ref[0].astype(BF16)
        wu_s[...] = wu_ref[0].astype(BF16)
        wd_s[...] = wd_ref[0].astype(BF16)

    n_used = nb_ref[0]
    BM = MOE_BM

    def gather(blk, slot):
        base = blk * BM
        for r in range(BM):
            pltpu.make_async_copy(x_hbm.at[pl.ds(tok_ref[base + r], 1)], xbuf.at[slot, pl.ds(r, 1)],
                                  sem.at[slot]).start()

    def wait(slot):
        pltpu.make_async_copy(x_hbm.at[pl.ds(0, BM)], xbuf.at[slot], sem.at[slot]).wait()

    slot = lax.rem(b, 2)

    @pl.when(b == 0)
    def _():
        gather(0, 0)

    @pl.when(b + 1 < n_used)
    def _():
        gather(b + 1, 1 - slot)

    @pl.when(b < n_used)
    def _():
        wait(slot)
        x = xbuf[slot].astype(BF16)
        hmid = _dot(x, wg_s[...])
        hmid = hmid * _sigmoid(hmid) * _dot(x, wu_s[...])
        y_ref[...] = _dot(hmid.astype(BF16), wd_s[...])

    @pl.when(b >= n_used)
    def _():
        y_ref[...] = jnp.zeros_like(y_ref)


def _experts(block_e, n_used, slot_tok, xn, w_gate, w_up, w_down):
    n_blocks = block_e.shape[0]
    T, D = xn.shape
    FF = w_gate.shape[2]
    BM = MOE_BM
    grid_spec = pltpu.PrefetchScalarGridSpec(
        num_scalar_prefetch=3,
        grid=(n_blocks,),
        in_specs=[pl.BlockSpec(memory_space=pl.ANY),
                  pl.BlockSpec((1, D, FF), lambda b, be, nb, tk: (be[b], 0, 0)),
                  pl.BlockSpec((1, D, FF), lambda b, be, nb, tk: (be[b], 0, 0)),
                  pl.BlockSpec((1, FF, D), lambda b, be, nb, tk: (be[b], 0, 0))],
        out_specs=pl.BlockSpec((BM, D), lambda b, be, nb, tk: (b, 0)),
        scratch_shapes=[pltpu.VMEM((2, BM, D), F32), pltpu.VMEM((D, FF), BF16), pltpu.VMEM((D, FF), BF16),
                        pltpu.VMEM((FF, D), BF16), pltpu.SemaphoreType.DMA((2,))],
    )
    return pl.pallas_call(
        _experts_kernel,
        grid_spec=grid_spec,
        out_shape=jax.ShapeDtypeStruct((n_blocks * BM, D), F32),
        compiler_params=_cparams(("arbitrary",)),
        name="experts",
    )(block_e, n_used, slot_tok, xn, w_gate, w_up, w_down)


COMB_TM = 128


def _combine_kernel(pos_ref, h_ref, gate_ref, y_hbm, o_ref, ybuf, sem):
    i = pl.program_id(0)
    n = pl.num_programs(0)
    TM = COMB_TM

    def gather(blk, slot):
        base = blk * (2 * TM)
        for r in range(2 * TM):
            pltpu.make_async_copy(y_hbm.at[pl.ds(pos_ref[base + r], 1)], ybuf.at[slot, pl.ds(r, 1)],
                                  sem.at[slot]).start()

    def wait(slot):
        pltpu.make_async_copy(y_hbm.at[pl.ds(0, 2 * TM)], ybuf.at[slot], sem.at[slot]).wait()

    slot = lax.rem(i, 2)

    @pl.when(i == 0)
    def _():
        gather(0, 0)

    @pl.when(i + 1 < n)
    def _():
        gather(i + 1, 1 - slot)

    wait(slot)
    g = gate_ref[...]
    o_ref[...] = h_ref[...] + (g[:, 0:1] * ybuf[slot, 0:TM, :] + g[:, 1:2] * ybuf[slot, TM:2 * TM, :])


def _combine(pos, h, gate, y_slots):
    T, D = h.shape
    TM = COMB_TM
    grid_spec = pltpu.PrefetchScalarGridSpec(
        num_scalar_prefetch=1,
        grid=(T // TM,),
        in_specs=[pl.BlockSpec((TM, D), lambda i, p: (i, 0)), pl.BlockSpec((TM, LANES), lambda i, p: (i, 0)),
                  pl.BlockSpec(memory_space=pl.ANY)],
        out_specs=pl.BlockSpec((TM, D), lambda i, p: (i, 0)),
        scratch_shapes=[pltpu.VMEM((2, 2 * TM, D), F32), pltpu.SemaphoreType.DMA((2,))],
    )
    return pl.pallas_call(
        _combine_kernel,
        grid_spec=grid_spec,
        out_shape=jax.ShapeDtypeStruct((T, D), F32),
        compiler_params=_cparams(("arbitrary",)),
        name="combine",
    )(pos, h, gate, y_slots)


def _moe(h, xn, eid, gate, w_gate, w_up, w_down):
    T = h.shape[0]
    A = T * TOP_K
    BM = MOE_BM
    n_blocks = -(-(A + N_EXPERTS * (BM - 1)) // BM)
    e_flat = eid[:, :TOP_K].reshape(A)
    onehot = (e_flat[:, None] == jnp.arange(N_EXPERTS, dtype=jnp.int32)[None, :]).astype(jnp.int32)
    csum = jnp.cumsum(onehot, axis=0)
    counts = csum[-1]
    rank = jnp.take_along_axis(csum, e_flat[:, None], axis=1)[:, 0] - 1
    padded = (counts + BM - 1) // BM * BM
    pad_end = jnp.cumsum(padded)
    pad_start = pad_end - padded
    pos = (pad_start[e_flat] + rank).astype(jnp.int32)
    tok = jnp.arange(A, dtype=jnp.int32) // TOP_K
    filler = jnp.arange(n_blocks * BM, dtype=jnp.int32) % T
    slot_tok = filler.at[pos].set(tok)
    block_first = jnp.arange(n_blocks, dtype=jnp.int32) * BM
    block_e = jnp.minimum(jnp.searchsorted(pad_end, block_first, side="right"), N_EXPERTS - 1).astype(jnp.int32)
    n_used = (pad_end[-1] // BM).astype(jnp.int32).reshape(1)
    y_slots = _experts(block_e, n_used, slot_tok, xn, w_gate, w_up, w_down)
    pos_tiles = pos.reshape(T // COMB_TM, COMB_TM, TOP_K).transpose(0, 2, 1).reshape(-1)
    return _combine(pos_tiles, h, gate, y_slots)


def _prep_weights(w_in, b_fox_f, w_router_group, b_router_group, w_router_expert, b_router_expert):
    D = w_in.shape[0]
    c = [0]
    for s in (FOX_W, FOX_W, FOX_W, FOX_HEADS, HGRN_W, HGRN_W, HGRN_W, HGRN_W, MEM_W):
        c.append(c[-1] + s)
    seg = lambda i: w_in[:, c[i]:c[i + 1]]
    last = jnp.concatenate([seg(8), seg(3), jnp.zeros((D, SEG_W - MEM_W - FOX_HEADS), w_in.dtype)], axis=1)
    w_seg = jnp.concatenate([seg(0), seg(1), seg(2), seg(4), seg(5), seg(6), seg(7), last], axis=1).astype(BF16)
    b_f_pad = jnp.zeros((1, LANES), F32).at[0, :FOX_HEADS].set(b_fox_f)
    n_r = N_GROUPS + N_EXPERTS
    w_r = jnp.zeros((D, LANES), F32).at[:, :N_GROUPS].set(w_router_group).at[:, N_GROUPS:n_r].set(w_router_expert)
    b_r = jnp.zeros((1, LANES), F32).at[0, :N_GROUPS].set(b_router_group).at[0, N_GROUPS:n_r].set(b_router_expert)
    wr_hi = w_r.astype(BF16)
    wr_lo = (w_r - wr_hi.astype(F32)).astype(BF16)
    return w_seg, b_f_pad, wr_hi, wr_lo, b_r


def kernel(x_prompt, x_sample, cache_fox_k, cache_fox_v, cache_fox_logf, cache_mem_k, cache_mem_v, state_hgrn, page_table, mem_prompt, g_attn_norm, w_in, b_fox_f, g_fox_q, g_fox_k, lb_logits, g_hgrn_out, g_mem_norm, w_mem_kv, g_mem_q, g_mem_k, w_out, g_ffn_norm, w_router_group, b_router_group, w_router_expert, b_router_expert, w_gate_e, w_up_e, w_down_e):
    assert w_in.shape[0] == 1, "single-layer step"
    Bp, S, D = x_prompt.shape
    Bd, L, _ = x_sample.shape
    n_pool, page = cache_fox_k.shape[1], cache_fox_k.shape[2]
    assert page == LANES and L <= SUBLANES
    M = mem_prompt.shape[1]
    l = 0
    row = lambda a: a[l].reshape(1, -1)

    w_seg, b_f_pad, wr_hi, wr_lo, b_r = _prep_weights(
        w_in[l], b_fox_f[l], w_router_group[l], b_router_group[l], w_router_expert[l], b_router_expert[l])
    w_out_bf = w_out[l].astype(BF16)
    w_mkv_bf = w_mem_kv[l].astype(BF16)
    experts_w = (w_gate_e[l], w_up_e[l], w_down_e[l])
    proj_args = (row(g_attn_norm), w_seg, row(g_fox_q), row(g_fox_k), row(g_mem_q), lb_logits, b_f_pad)

    Tp = Bp * S
    xp = x_prompt.reshape(Tp, D)
    fq, fk_p, fkb, fv_p, fvb, hq, hlf, hv, hgs, mq, flf_p = _in_proj(xp, *proj_args, tm=512)
    seq = lambda a: a.reshape(Bp, S, a.shape[-1])
    ck = _fox_cum(seq(flf_p))
    fox_o = _fox_attn(seq(fq), seq(fkb), seq(fvb), ck)
    hg_o, s_p = _hgrn(seq(hq), seq(hlf), seq(hv), seq(hgs), row(g_hgrn_out), None, C=128, n_heads=1, n_valid=128)
    mk, mv = _mem_kv(mem_prompt.reshape(Bp * M, D), row(g_mem_norm), w_mkv_bf, row(g_mem_k), tm=256)
    mem4 = lambda a, b: a.reshape(b, M, MEM_HEADS, HEAD_DIM)
    mem_o = _mem_attn(seq(mq), mem4(mk, Bp), mem4(mv, Bp), tq=512)
    h, xn, eid, gate = _merge(fox_o.reshape(Tp, FOX_W), hg_o.reshape(Tp, HGRN_W), mem_o.reshape(Tp, MEM_W),
                              xp, w_out_bf, row(g_ffn_norm), wr_hi, wr_lo, b_r, tm=256)
    y_p = _moe(h, xn, eid, gate, *experts_w)

    R = SUBLANES
    Ts = Bd * R
    xs = jnp.pad(x_sample, ((0, 0), (0, R - L), (0, 0))).reshape(Ts, D)
    fq, fk_s, fkb, fv_s, fvb, hq, hlf, hv, hgs, mq, flf_s = _in_proj(xs, *proj_args, tm=Ts)
    seqs = lambda a: a.reshape(Bd, R, a.shape[-1])
    lf_new_t = jnp.swapaxes(seqs(flf_s)[:, :, :HEAD_PAD], 1, 2)
    lf_new_t = jnp.pad(lf_new_t, ((0, 0), (0, 0), (0, LANES - R)))
    cache_lf_t = jnp.pad(jnp.swapaxes(cache_fox_logf[l].astype(F32), 1, 2), ((0, 0), (0, HEAD_PAD - FOX_HEADS), (0, 0)))
    fox_o = _fox_dec(page_table, seqs(fq), seqs(fkb), seqs(fvb), lf_new_t, cache_fox_k[l], cache_fox_v[l],
                     cache_lf_t, n_tok=L)
    hg_o, s_s = _hgrn(seqs(hq), seqs(hlf), seqs(hv), seqs(hgs), row(g_hgrn_out), state_hgrn[l],
                      C=R, n_heads=HGRN_HEADS, n_valid=L)
    mem_o = _mem_attn(seqs(mq), cache_mem_k[l], cache_mem_v[l], tq=R)
    h, xn, eid, gate = _merge(fox_o.reshape(Ts, FOX_W), hg_o.reshape(Ts, HGRN_W), mem_o.reshape(Ts, MEM_W),
                              xs, w_out_bf, row(g_ffn_norm), wr_hi, wr_lo, b_r, tm=Ts)
    y_s = _moe(h, xn, eid, gate, *experts_w)

    cut = lambda a: a.reshape((Bd, R) + a.shape[1:])[:, :L]
    return (y_p.reshape(Bp, S, D), cut(y_s),
            fk_p.reshape(1, Bp, S, FOX_HEADS, HEAD_DIM), fv_p.reshape(1, Bp, S, FOX_HEADS, HEAD_DIM),
            flf_p[:, :FOX_HEADS].reshape(1, Bp, S, FOX_HEADS), s_p[None],
            mem4(mk, Bp)[None], mem4(mv, Bp)[None],
            cut(fk_s)[None], cut(fv_s)[None], cut(flf_s)[:, :, :FOX_HEADS][None], s_s[None])
```

```python
import functools

import jax
import jax.numpy as jnp
from jax import lax
from jax.experimental import pallas as pl
from jax.experimental.pallas import tpu as pltpu

F32 = jnp.float32
BF16 = jnp.bfloat16

HEAD_DIM = 128
FOX_HEADS = 6
HGRN_HEADS = 6
MEM_HEADS = 4
FOX_W = FOX_HEADS * HEAD_DIM
HGRN_W = HGRN_HEADS * HEAD_DIM
MEM_W = MEM_HEADS * HEAD_DIM
N_GROUPS = 4
EXPERTS_PER_GROUP = 8
N_EXPERTS = N_GROUPS * EXPERTS_PER_GROUP
TOP_K = 2
EPS = 1e-6
ATTN_SCALE = HEAD_DIM ** -0.5
LANES = 128
SUBLANES = 8
SEG_W = FOX_W
N_SEG = 8
HEAD_PAD = 8
MOE_BM = 256
VMEM_LIMIT = 52 * 1024 * 1024

NT_DIMS = (((1,), (1,)), ((), ()))
TN_DIMS = (((0,), (0,)), ((), ()))


def _cparams(sem, vmem=VMEM_LIMIT):
    return pltpu.CompilerParams(dimension_semantics=sem, vmem_limit_bytes=vmem)


def _split3(x):
    hi = x.astype(BF16)
    r1 = x - hi.astype(F32)
    mid = r1.astype(BF16)
    lo = (r1 - mid.astype(F32)).astype(BF16)
    return hi, mid, lo


def _dot(a, b):
    return jnp.dot(a, b, preferred_element_type=F32)


def _dot_nt(a, b):
    return lax.dot_general(a, b, NT_DIMS, preferred_element_type=F32)


def _sigmoid(x):
    return 1.0 / (1.0 + jnp.exp(-x))


def _log_sigmoid(x):
    return jnp.minimum(x, 0.0) - jnp.log(1.0 + jnp.exp(-jnp.abs(x)))


def _rms_heads(a, g, n_heads, scale):
    outs = []
    for h in range(n_heads):
        ah = a[:, h * HEAD_DIM:(h + 1) * HEAD_DIM]
        ms = jnp.mean(ah * ah, axis=-1, keepdims=True)
        outs.append(ah * lax.rsqrt(ms + EPS) * (g * scale))
    return jnp.concatenate(outs, axis=-1)


def _store_heads(ref, a, n_heads):
    for h in range(n_heads):
        ref[:, h, :] = a[:, h * HEAD_DIM:(h + 1) * HEAD_DIM]


def _store_head_major(ref, a, n_heads):
    nb, _, rows, _ = ref.shape
    for h in range(n_heads):
        ref[:, h] = a[:, h * HEAD_DIM:(h + 1) * HEAD_DIM].reshape(nb, rows, HEAD_DIM)


def _in_proj_kernel(x_ref, gn_ref, w_ref, gfq_ref, gfk_ref, gmq_ref, lbl_ref, bf_ref,
                    fq_ref, fk_ref, fkb_ref, fv_ref, fvb_ref, hq_ref, hlf_ref, hv_ref, hgs_ref, mq_ref, flf_ref,
                    xn_s):
    j = pl.program_id(1)

    @pl.when(j == 0)
    def _():
        x = x_ref[...]
        ms = jnp.mean(x * x, axis=-1, keepdims=True)
        xn_s[...] = (x * lax.rsqrt(ms + EPS) * gn_ref[...]).astype(BF16)

    acc = _dot(xn_s[...], w_ref[...])

    @pl.when(j == 0)
    def _():
        fq_ref[...] = _rms_heads(acc, gfq_ref[...], FOX_HEADS, ATTN_SCALE).astype(fq_ref.dtype)

    @pl.when(j == 1)
    def _():
        fk = _rms_heads(acc, gfk_ref[...], FOX_HEADS, 1.0)
        _store_head_major(fk_ref, fk, FOX_HEADS)
        fkb_ref[...] = fk.astype(BF16)

    @pl.when(j == 2)
    def _():
        _store_head_major(fv_ref, acc, FOX_HEADS)
        fvb_ref[...] = acc.astype(BF16)

    @pl.when(j == 3)
    def _():
        hq_ref[...] = acc * _sigmoid(acc) * ATTN_SCALE

    @pl.when(j == 4)
    def _():
        l = lbl_ref[...]
        mx = jnp.max(l, axis=0, keepdims=True)
        ex = jnp.exp(l - mx)
        lb = ex[0:1, :] / jnp.sum(ex, axis=0, keepdims=True)
        hlf_ref[...] = jnp.log(lb + (1.0 - lb) * _sigmoid(acc))

    @pl.when(j == 5)
    def _():
        hv_ref[...] = acc

    @pl.when(j == 6)
    def _():
        hgs_ref[...] = acc * _sigmoid(acc)

    @pl.when(j == 7)
    def _():
        mq_ref[...] = _rms_heads(acc[:, :MEM_W], gmq_ref[...], MEM_HEADS, ATTN_SCALE).astype(mq_ref.dtype)
        flf_ref[...] = _log_sigmoid(acc[:, MEM_W:MEM_W + LANES] + bf_ref[...])


def _in_proj(x, g_norm, w_seg, g_fq, g_fk, g_mq, lb_logits, b_f_pad, tm, seq):
    T, D = x.shape
    full = lambda a: pl.BlockSpec(a.shape, lambda i, j: (0,) * a.ndim)
    nb, rows = max(1, tm // seq), min(tm, seq)
    assert nb * rows == tm and seq % rows == 0
    tiles_per_seq = seq // rows
    kv_cache = jax.ShapeDtypeStruct((T // seq, FOX_HEADS, seq, HEAD_DIM), F32)
    kv_spec = pl.BlockSpec((nb, FOX_HEADS, rows, HEAD_DIM),
                           lambda i, j: (i // tiles_per_seq, 0, i % tiles_per_seq, 0))
    outs = [
        jax.ShapeDtypeStruct((T, FOX_W), BF16),
        kv_cache,
        jax.ShapeDtypeStruct((T, FOX_W), BF16),
        kv_cache,
        jax.ShapeDtypeStruct((T, FOX_W), BF16),
        jax.ShapeDtypeStruct((T, HGRN_W), F32),
        jax.ShapeDtypeStruct((T, HGRN_W), F32),
        jax.ShapeDtypeStruct((T, HGRN_W), F32),
        jax.ShapeDtypeStruct((T, HGRN_W), F32),
        jax.ShapeDtypeStruct((T, MEM_W), BF16),
        jax.ShapeDtypeStruct((T, LANES), F32),
    ]
    return pl.pallas_call(
        _in_proj_kernel,
        grid=(T // tm, N_SEG),
        in_specs=[pl.BlockSpec((tm, D), lambda i, j: (i, 0)), full(g_norm),
                  pl.BlockSpec((D, SEG_W), lambda i, j: (0, j)),
                  full(g_fq), full(g_fk), full(g_mq), full(lb_logits), full(b_f_pad)],
        out_specs=[kv_spec if o is kv_cache else pl.BlockSpec((tm, o.shape[1]), lambda i, j: (i, 0)) for o in outs],
        out_shape=outs,
        scratch_shapes=[pltpu.VMEM((tm, D), BF16)],
        compiler_params=_cparams(("arbitrary", "arbitrary")),
        name="in_proj",
    )(x, g_norm, w_seg, g_fq, g_fk, g_mq, lb_logits, b_f_pad)


CUM_C = 256


def _fox_cum_kernel(lf_ref, ck_ref):
    S = lf_ref.shape[1]
    r = lax.broadcasted_iota(jnp.int32, (CUM_C, CUM_C), 0)
    c = lax.broadcasted_iota(jnp.int32, (CUM_C, CUM_C), 1)
    tri = jnp.where(c <= r, 1.0, 0.0).astype(BF16)
    carry = jnp.zeros((1, LANES), F32)
    for i in range(S // CUM_C):
        sl = slice(i * CUM_C, (i + 1) * CUM_C)
        hi, mid, lo = _split3(lf_ref[0, sl, :])
        cum = (_dot(tri, hi) + _dot(tri, mid)) + _dot(tri, lo) + carry
        carry = cum[CUM_C - 1:CUM_C, :]
        ck_ref[0, :, sl] = cum.T[0:HEAD_PAD, :]


def _fox_cum(flf):
    B, S, _ = flf.shape
    return pl.pallas_call(
        _fox_cum_kernel,
        grid=(B,),
        in_specs=[pl.BlockSpec((1, S, LANES), lambda b: (b, 0, 0))],
        out_specs=pl.BlockSpec((1, HEAD_PAD, S), lambda b: (b, 0, 0)),
        out_shape=jax.ShapeDtypeStruct((B, HEAD_PAD, S), F32),
        compiler_params=_cparams(("arbitrary",)),
        name="fox_cum",
    )(flf)


FOX_T = 256


def _softmax_step(s, v_bf, m, l, acc):
    m_new = jnp.maximum(m, jnp.max(s, axis=-1, keepdims=True))
    alpha = jnp.exp(m - m_new)
    p = jnp.exp(s - m_new)
    l = alpha * l + jnp.sum(p, axis=-1, keepdims=True)
    acc = alpha * acc + _dot(p.astype(BF16), v_bf)
    return m_new, l, acc


def _fox_attn_kernel(q_ref, k_ref, v_ref, ck_ref, o_ref, m_s, l_s, acc_s):
    qi = pl.program_id(1)
    T = FOX_T
    m_s[...] = jnp.full_like(m_s, -jnp.inf)
    l_s[...] = jnp.zeros_like(l_s)
    acc_s[...] = jnp.zeros_like(acc_s)
    row = lax.broadcasted_iota(jnp.int32, (T, T), 0)
    col = lax.broadcasted_iota(jnp.int32, (T, T), 1)
    causal = col <= row

    def tile(ks, masked):
        for h in range(FOX_HEADS):
            hs = slice(h * HEAD_DIM, (h + 1) * HEAD_DIM)
            s = _dot_nt(q_ref[0, :, hs], k_ref[0, pl.ds(ks, T), hs]) - ck_ref[0, h:h + 1, pl.ds(ks, T)]
            if masked:
                s = jnp.where(causal, s, -jnp.inf)
            m_old = m_s[:, hs]
            m_new = jnp.maximum(m_old, jnp.max(s, axis=-1, keepdims=True))
            alpha = jnp.exp(m_old - m_new)
            p = jnp.exp(s - jnp.concatenate([m_new] * (T // HEAD_DIM), axis=-1))
            m_s[:, hs] = m_new
            l_s[:, hs] = alpha * l_s[:, hs] + jnp.sum(p, axis=-1, keepdims=True)
            acc_s[:, hs] = alpha * acc_s[:, hs] + _dot(p.astype(BF16), v_ref[0, pl.ds(ks, T), hs])

    def body(kt, c):
        tile(pl.multiple_of(kt * T, T), False)
        return c

    lax.fori_loop(0, qi, body, 0)
    tile(pl.multiple_of(qi * T, T), True)
    o_ref[0] = (acc_s[...] / l_s[...]).astype(o_ref.dtype)


def _fox_attn(fq, fk, fv, ck):
    B, S, _ = fq.shape
    T = FOX_T
    return pl.pallas_call(
        _fox_attn_kernel,
        grid=(B, S // T),
        in_specs=[pl.BlockSpec((1, T, FOX_W), lambda b, i: (b, i, 0)),
                  pl.BlockSpec((1, S, FOX_W), lambda b, i: (b, 0, 0)),
                  pl.BlockSpec((1, S, FOX_W), lambda b, i: (b, 0, 0)),
                  pl.BlockSpec((1, HEAD_PAD, S), lambda b, i: (b, 0, 0))],
        out_specs=pl.BlockSpec((1, T, FOX_W), lambda b, i: (b, i, 0)),
        out_shape=jax.ShapeDtypeStruct((B, S, FOX_W), BF16),
        scratch_shapes=[pltpu.VMEM((T, FOX_W), F32)] * 3,
        compiler_params=_cparams(("arbitrary", "arbitrary")),
        name="fox_attn",
    )(fq, fk, fv, ck)


def _level_ref(b, m):
    C = b.shape[0]
    if 2 * m >= SUBLANES:
        b3 = b.reshape(C // (2 * m), 2 * m, HEAD_DIM)
        r = jnp.broadcast_to(b3[:, m - 1:m, :], b3.shape)
        return r.reshape(C, HEAD_DIM)
    b3 = b.reshape(C // SUBLANES, SUBLANES, HEAD_DIM)
    sub = lax.broadcasted_iota(jnp.int32, b3.shape, 1)
    pick = lambda i: jnp.broadcast_to(b3[:, i:i + 1, :], b3.shape)
    if m == 2:
        r = jnp.where(sub < 4, pick(1), pick(5))
    else:
        r = jnp.where(sub < 2, pick(0), jnp.where(sub < 4, pick(2), jnp.where(sub < 6, pick(4), pick(6))))
    return r.reshape(C, HEAD_DIM)


def _level_index(C):
    t = lax.broadcasted_iota(jnp.int32, (C, C), 0)
    s = lax.broadcasted_iota(jnp.int32, (C, C), 1)
    x = jnp.bitwise_xor(t, s)
    lvl = jnp.full((C, C), -1, jnp.int32)
    j, m = 0, 1
    while m < C:
        lvl = jnp.where(x >= m, j, lvl)
        j, m = j + 1, 2 * m
    return jnp.where(t > s, lvl, -1)


def _hgrn_chunk(q, g, v, st, tri, lvl, n_valid):
    C = q.shape[0]
    hi, mid, lo = _split3(g)
    b = (_dot(tri, hi) + _dot(tri, mid)) + _dot(tri, lo)
    k = 1.0 - jnp.exp(g)
    a = jnp.zeros((C, C), F32)
    j, m = 0, 1
    while m < C:
        e = jnp.exp(-jnp.abs(b - _level_ref(b, m)))
        a_l = _dot_nt((q * e).astype(BF16), (k * e).astype(BF16))
        a = jnp.where(lvl == j, a_l, a)
        j, m = j + 1, 2 * m
    v_bf = v.astype(BF16)
    diag = jnp.sum(q * k, axis=-1, keepdims=True)
    o = _dot_nt((q * jnp.exp(b)).astype(BF16), st.astype(BF16)) + _dot(a.astype(BF16), v_bf) + diag * v
    b_last = b[n_valid - 1:n_valid, :]
    kt = k * jnp.exp(jnp.minimum(b_last - b, 0.0))
    if n_valid < C:
        rows = lax.broadcasted_iota(jnp.int32, (C, HEAD_DIM), 0)
        kt = jnp.where(rows < n_valid, kt, 0.0)
    st_new = st * jnp.exp(b_last) + lax.dot_general(v_bf, kt.astype(BF16), TN_DIMS, preferred_element_type=F32)
    return o, st_new


def _hgrn_kernel(*refs, C, n_chunks, n_heads, n_valid, has_s0):
    if has_s0:
        q_ref, g_ref, v_ref, gs_ref, gn_ref, s0_ref, o_ref, sf_ref = refs
    else:
        q_ref, g_ref, v_ref, gs_ref, gn_ref, o_ref, sf_ref = refs
    r = lax.broadcasted_iota(jnp.int32, (C, C), 0)
    c = lax.broadcasted_iota(jnp.int32, (C, C), 1)
    tri = jnp.where(c <= r, 1.0, 0.0).astype(BF16)
    lvl = _level_index(C)
    gn = gn_ref[...]
    sts0 = tuple(s0_ref[0, h].T if has_s0 else jnp.zeros((HEAD_DIM, HEAD_DIM), F32) for h in range(n_heads))

    def body(ci, sts):
        rs = pl.ds(pl.multiple_of(ci * C, C), C)
        out = []
        for h in range(n_heads):
            hs = slice(h * HEAD_DIM, (h + 1) * HEAD_DIM)
            o, st = _hgrn_chunk(q_ref[0, rs, hs], g_ref[0, rs, hs], v_ref[0, rs, hs], sts[h], tri, lvl, n_valid)
            ms = jnp.mean(o * o, axis=-1, keepdims=True)
            o_ref[0, rs, hs] = (o * lax.rsqrt(ms + EPS) * gn * gs_ref[0, rs, hs]).astype(o_ref.dtype)
            out.append(st)
        return tuple(out)

    sts = lax.fori_loop(0, n_chunks, body, sts0) if n_chunks > 1 else body(0, sts0)
    for h in range(n_heads):
        sf_ref[0, h] = sts[h].T


def _hgrn(hq, hlf, hv, hgs, g_hn, s0, C, n_heads, n_valid):
    B, L, _ = hq.shape
    hp = HGRN_HEADS // n_heads
    w = n_heads * HEAD_DIM
    seq = pl.BlockSpec((1, L, w), lambda b, h: (b, 0, h))
    st_spec = pl.BlockSpec((1, n_heads, HEAD_DIM, HEAD_DIM), lambda b, h: (b, h, 0, 0))
    in_specs = [seq, seq, seq, seq, pl.BlockSpec((1, HEAD_DIM), lambda b, h: (0, 0))]
    args = [hq, hlf, hv, hgs, g_hn]
    if s0 is not None:
        in_specs.append(st_spec)
        args.append(s0)
    kern = functools.partial(_hgrn_kernel, C=C, n_chunks=L // C, n_heads=n_heads, n_valid=n_valid,
                             has_s0=s0 is not None)
    return pl.pallas_call(
        kern,
        grid=(B, hp),
        in_specs=in_specs,
        out_specs=[seq, st_spec],
        out_shape=[jax.ShapeDtypeStruct((B, L, HGRN_W), BF16),
                   jax.ShapeDtypeStruct((B, HGRN_HEADS, HEAD_DIM, HEAD_DIM), F32)],
        compiler_params=_cparams(("arbitrary", "arbitrary")),
        name="hgrn",
    )(*args)


def _mem_kv_kernel(x_ref, gn_ref, w_ref, gk_ref, mk_ref, mv_ref, xn_s):
    j = pl.program_id(1)

    @pl.when(j == 0)
    def _():
        x = x_ref[...]
        ms = jnp.mean(x * x, axis=-1, keepdims=True)
        xn_s[...] = (x * lax.rsqrt(ms + EPS) * gn_ref[...]).astype(BF16)

    acc = _dot(xn_s[...], w_ref[...])

    @pl.when(j == 0)
    def _():
        _store_heads(mk_ref, _rms_heads(acc, gk_ref[...], MEM_HEADS, 1.0), MEM_HEADS)

    @pl.when(j == 1)
    def _():
        _store_heads(mv_ref, acc, MEM_HEADS)


def _mem_kv(mem, g_norm, w, g_mk, tm):
    T, D = mem.shape
    out = jax.ShapeDtypeStruct((T, MEM_HEADS, HEAD_DIM), F32)
    return pl.pallas_call(
        _mem_kv_kernel,
        grid=(T // tm, 2),
        in_specs=[pl.BlockSpec((tm, D), lambda i, j: (i, 0)), pl.BlockSpec((1, D), lambda i, j: (0, 0)),
                  pl.BlockSpec((D, MEM_W), lambda i, j: (0, j)), pl.BlockSpec((1, HEAD_DIM), lambda i, j: (0, 0))],
        out_specs=[pl.BlockSpec((tm, MEM_HEADS, HEAD_DIM), lambda i, j: (i, 0, 0))] * 2,
        out_shape=[out, out],
        scratch_shapes=[pltpu.VMEM((tm, D), BF16)],
        compiler_params=_cparams(("arbitrary", "arbitrary")),
        name="mem_kv",
    )(mem, g_norm, w, g_mk)


def _mem_attn_kernel(q_ref, k_ref, v_ref, o_ref):
    for h in range(MEM_HEADS):
        hs = slice(h * HEAD_DIM, (h + 1) * HEAD_DIM)
        s = _dot_nt(q_ref[0, :, hs], k_ref[0, :, h, :].astype(BF16))
        p = jnp.exp(s - jnp.max(s, axis=-1, keepdims=True))
        l = jnp.sum(p, axis=-1, keepdims=True)
        o_ref[0, :, hs] = (_dot(p.astype(BF16), v_ref[0, :, h, :].astype(BF16)) / l).astype(o_ref.dtype)


def _mem_attn(mq, mk, mv, tq):
    B, L, _ = mq.shape
    M = mk.shape[1]
    kv = pl.BlockSpec((1, M, MEM_HEADS, HEAD_DIM), lambda b, i: (b, 0, 0, 0))
    qo = pl.BlockSpec((1, tq, MEM_W), lambda b, i: (b, i, 0))
    return pl.pallas_call(
        _mem_attn_kernel,
        grid=(B, L // tq),
        in_specs=[qo, kv, kv],
        out_specs=qo,
        out_shape=jax.ShapeDtypeStruct((B, L, MEM_W), BF16),
        compiler_params=_cparams(("arbitrary", "arbitrary")),
        name="mem_attn",
    )(mq, mk, mv)


DEC_G = 8
DEC_ROWS = 4 * HEAD_PAD


def _suffix_sum_lanes(x):
    lane = lax.broadcasted_iota(jnp.int32, x.shape, 1)
    s = 1
    while s < LANES:
        x = x + jnp.where(lane + s < LANES, pltpu.roll(x, LANES - s, 1), 0.0)
        s *= 2
    return x


def _prefix_sum_lanes(x):
    lane = lax.broadcasted_iota(jnp.int32, x.shape, 1)
    s = 1
    while s < LANES:
        x = x + jnp.where(lane >= s, pltpu.roll(x, s, 1), 0.0)
        s *= 2
    return x


def _fox_dec_kernel(pt_ref, q_ref, kn_ref, vn_ref, lfn_ref, *refs, n_tok):
    G = DEC_G
    k_refs, v_refs, lf_refs = refs[0:G], refs[G:2 * G], refs[2 * G:3 * G]
    o_ref = refs[3 * G]
    qbd, kn_s, vn_s, m_s, l_s, acc_s, car_s = refs[3 * G + 1:]
    st = pl.program_id(1)

    def attend(k_bf, v_bf, bias):
        s = _dot_nt(qbd[...].astype(BF16), k_bf) + bias
        m_new, l_new, acc_new = _softmax_step(s, v_bf, m_s[...], l_s[...], acc_s[...])
        m_s[...] = m_new
        l_s[...] = l_new
        acc_s[...] = acc_new

    @pl.when(st == 0)
    def _():
        qbd[...] = jnp.zeros_like(qbd)
        q = q_ref[0].astype(F32)
        for t in range(n_tok):
            for h in range(FOX_HEADS):
                hs = slice(h * HEAD_DIM, (h + 1) * HEAD_DIM)
                qbd[t * HEAD_PAD + h:t * HEAD_PAD + h + 1, hs] = q[t:t + 1, hs]
        kn_s[...] = jnp.zeros_like(kn_s)
        vn_s[...] = jnp.zeros_like(vn_s)
        kn_s[0:SUBLANES, :] = kn_ref[0].astype(F32)
        vn_s[0:SUBLANES, :] = vn_ref[0].astype(F32)
        m_s[...] = jnp.full_like(m_s, -jnp.inf)
        l_s[...] = jnp.zeros_like(l_s)
        acc_s[...] = jnp.zeros_like(acc_s)
        car_s[...] = jnp.zeros_like(car_s)
        ecum = _prefix_sum_lanes(lfn_ref[0])
        lane = lax.broadcasted_iota(jnp.int32, (HEAD_PAD, LANES), 1)
        bias = jnp.concatenate([jnp.where(lane <= t, -ecum, -jnp.inf) for t in range(n_tok)], axis=0)
        attend(kn_s[...].astype(BF16), vn_s[...].astype(BF16), bias)

    heads = lambda ref: jnp.concatenate([ref[0, h] for h in range(FOX_HEADS)], axis=-1).astype(BF16)
    carry = car_s[...]
    ds = []
    for i in range(G):
        lf = lf_refs[i][0]
        incl = _suffix_sum_lanes(lf)
        ds.append(carry + (incl - lf))
        carry = carry + incl[:, 0:1]
    car_s[...] = carry
    d_all = jnp.concatenate(ds, axis=-1)
    attend(jnp.concatenate([heads(r) for r in k_refs], axis=0),
           jnp.concatenate([heads(r) for r in v_refs], axis=0),
           jnp.concatenate([d_all] * n_tok, axis=0))

    @pl.when(st == pl.num_programs(1) - 1)
    def _():
        res = acc_s[...] / l_s[...]
        o_ref[...] = jnp.zeros_like(o_ref)
        for t in range(n_tok):
            for h in range(FOX_HEADS):
                hs = slice(h * HEAD_DIM, (h + 1) * HEAD_DIM)
                r = t * HEAD_PAD + h
                o_ref[0, t:t + 1, hs] = res[r:r + 1, hs].astype(o_ref.dtype)


def _fox_dec(page_table, fq, fk, fv, lf_new_t, cache_k, cache_v, cache_lf_t, n_tok):
    B, n_pages = page_table.shape
    G = DEC_G
    assert n_pages % G == 0 and n_tok * HEAD_PAD == DEC_ROWS
    n_steps = n_pages // G
    tok = lambda w: pl.BlockSpec((1, SUBLANES, w), lambda b, s, pt: (b, 0, 0))

    def page_spec(shape, i):
        return pl.BlockSpec((1,) + shape,
                            lambda b, s, pt: (pt[b * n_pages + (n_pages - 1 - (s * G + i))],) + (0,) * len(shape))

    kv_page = (FOX_HEADS, LANES, HEAD_DIM)
    in_specs = ([tok(FOX_W), tok(FOX_W), tok(FOX_W), tok(LANES)]
                + [page_spec(kv_page, i) for i in range(G)]
                + [page_spec(kv_page, i) for i in range(G)]
                + [page_spec((HEAD_PAD, LANES), i) for i in range(G)])
    grid_spec = pltpu.PrefetchScalarGridSpec(
        num_scalar_prefetch=1,
        grid=(B, n_steps),
        in_specs=in_specs,
        out_specs=pl.BlockSpec((1, SUBLANES, FOX_W), lambda b, s, pt: (b, 0, 0)),
        scratch_shapes=[pltpu.VMEM((DEC_ROWS, FOX_W), F32),
                        pltpu.VMEM((LANES, FOX_W), F32), pltpu.VMEM((LANES, FOX_W), F32),
                        pltpu.VMEM((DEC_ROWS, 1), F32), pltpu.VMEM((DEC_ROWS, 1), F32),
                        pltpu.VMEM((DEC_ROWS, FOX_W), F32), pltpu.VMEM((HEAD_PAD, 1), F32)],
    )
    return pl.pallas_call(
        functools.partial(_fox_dec_kernel, n_tok=n_tok),
        grid_spec=grid_spec,
        out_shape=jax.ShapeDtypeStruct((B, SUBLANES, FOX_W), BF16),
        compiler_params=_cparams(("arbitrary", "arbitrary")),
        name="fox_dec",
    )(page_table.reshape(-1), fq, fk, fv, lf_new_t, *([cache_k] * G), *([cache_v] * G), *([cache_lf_t] * G))


def _merge_kernel(fo_ref, ho_ref, mo_ref, x_ref, w_ref, gf_ref, wrh_ref, wrl_ref, br_ref,
                  h_ref, xn_ref, eid_ref, gate_ref):
    h = x_ref[...] + (_dot(fo_ref[...], w_ref[0:FOX_W, :])
                      + _dot(ho_ref[...], w_ref[FOX_W:FOX_W + HGRN_W, :])
                      + _dot(mo_ref[...], w_ref[FOX_W + HGRN_W:, :]))
    h_ref[...] = h
    ms = jnp.mean(h * h, axis=-1, keepdims=True)
    xn = h * lax.rsqrt(ms + EPS) * gf_ref[...]
    xn_ref[...] = xn
    x_hi = xn.astype(BF16)
    x_lo = (xn - x_hi.astype(F32)).astype(BF16)
    logits = (_dot(x_hi, wrh_ref[...]) + (_dot(x_hi, wrl_ref[...]) + _dot(x_lo, wrh_ref[...]))) + br_ref[...]
    lane = lax.broadcasted_iota(jnp.int32, logits.shape, 1)
    big = jnp.int32(LANES)
    ninf = -jnp.inf
    gl = jnp.where(lane < N_GROUPS, logits, ninf)
    gmax = jnp.max(gl, axis=-1, keepdims=True)
    g_sel = jnp.min(jnp.where(gl == gmax, lane, big), axis=-1, keepdims=True)
    g_prob = 1.0 / jnp.sum(jnp.exp(gl - gmax), axis=-1, keepdims=True)
    lo = N_GROUPS + EXPERTS_PER_GROUP * g_sel
    el = jnp.where((lane >= lo) & (lane < lo + EXPERTS_PER_GROUP), logits, ninf)
    v1 = jnp.max(el, axis=-1, keepdims=True)
    i1 = jnp.min(jnp.where(el == v1, lane, big), axis=-1, keepdims=True)
    el2 = jnp.where(lane == i1, ninf, el)
    v2 = jnp.max(el2, axis=-1, keepdims=True)
    i2 = jnp.min(jnp.where(el2 == v2, lane, big), axis=-1, keepdims=True)
    t = jnp.exp(v2 - v1)
    w1 = g_prob / (1.0 + t)
    w2 = g_prob * t / (1.0 + t)
    eid_ref[...] = jnp.where(lane == 0, i1 - N_GROUPS, jnp.where(lane == 1, i2 - N_GROUPS, 0))
    gate_ref[...] = jnp.where(lane == 0, w1, jnp.where(lane == 1, w2, 0.0))


def _merge(fo, ho, mo, x, w_out, g_ffn, wr_hi, wr_lo, b_r, tm):
    T, D = x.shape
    row = lambda w: pl.BlockSpec((tm, w), lambda i: (i, 0))
    full = lambda a: pl.BlockSpec(a.shape, lambda i: (0,) * a.ndim)
    return pl.pallas_call(
        _merge_kernel,
        grid=(T // tm,),
        in_specs=[row(FOX_W), row(HGRN_W), row(MEM_W), row(D), full(w_out), full(g_ffn),
                  full(wr_hi), full(wr_lo), full(b_r)],
        out_specs=[row(D), row(D), row(LANES), row(LANES)],
        out_shape=[jax.ShapeDtypeStruct((T, D), F32), jax.ShapeDtypeStruct((T, D), F32),
                   jax.ShapeDtypeStruct((T, LANES), jnp.int32), jax.ShapeDtypeStruct((T, LANES), F32)],
        compiler_params=_cparams(("arbitrary",)),
        name="merge",
    )(fo, ho, mo, x, w_out, g_ffn, wr_hi, wr_lo, b_r)


def _experts_kernel(be_ref, nb_ref, tok_ref, x_hbm, wg_ref, wu_ref, wd_ref, y_ref, xbuf, wg_s, wu_s, wd_s, sem):
    b = pl.program_id(0)

    @pl.when((b == 0) | (be_ref[b] != be_ref[jnp.maximum(b - 1, 0)]))
    def _():
        wg_s[...] = wg_ref[0].astype(BF16)
        wu_s[...] = wu_ref[0].astype(BF16)
        wd_s[...] = wd_ref[0].astype(BF16)

    n_used = nb_ref[0]
    BM = MOE_BM

    def gather(blk, slot):
        base = blk * BM
        for r in range(BM):
            pltpu.make_async_copy(x_hbm.at[pl.ds(tok_ref[base + r], 1)], xbuf.at[slot, pl.ds(r, 1)],
                                  sem.at[slot]).start()

    def wait(slot):
        pltpu.make_async_copy(x_hbm.at[pl.ds(0, BM)], xbuf.at[slot], sem.at[slot]).wait()

    slot = lax.rem(b, 2)

    @pl.when(b == 0)
    def _():
        gather(0, 0)

    @pl.when(b + 1 < n_used)
    def _():
        gather(b + 1, 1 - slot)

    @pl.when(b < n_used)
    def _():
        wait(slot)
        x = xbuf[slot].astype(BF16)
        hmid = _dot(x, wg_s[...])
        hmid = hmid * _sigmoid(hmid) * _dot(x, wu_s[...])
        y_ref[...] = _dot(hmid.astype(BF16), wd_s[...])

    @pl.when(b >= n_used)
    def _():
        y_ref[...] = jnp.zeros_like(y_ref)


def _experts(block_e, n_used, slot_tok, xn, w_gate, w_up, w_down):
    n_blocks = block_e.shape[0]
    T, D = xn.shape
    FF = w_gate.shape[2]
    BM = MOE_BM
    grid_spec = pltpu.PrefetchScalarGridSpec(
        num_scalar_prefetch=3,
        grid=(n_blocks,),
        in_specs=[pl.BlockSpec(memory_space=pl.ANY),
                  pl.BlockSpec((1, D, FF), lambda b, be, nb, tk: (be[b], 0, 0)),
                  pl.BlockSpec((1, D, FF), lambda b, be, nb, tk: (be[b], 0, 0)),
                  pl.BlockSpec((1, FF, D), lambda b, be, nb, tk: (be[b], 0, 0))],
        out_specs=pl.BlockSpec((BM, D), lambda b, be, nb, tk: (b, 0)),
        scratch_shapes=[pltpu.VMEM((2, BM, D), F32), pltpu.VMEM((D, FF), BF16), pltpu.VMEM((D, FF), BF16),
                        pltpu.VMEM((FF, D), BF16), pltpu.SemaphoreType.DMA((2,))],
    )
    return pl.pallas_call(
        _experts_kernel,
        grid_spec=grid_spec,
        out_shape=jax.ShapeDtypeStruct((n_blocks * BM, D), F32),
        compiler_params=_cparams(("arbitrary",)),
        name="experts",
    )(block_e, n_used, slot_tok, xn, w_gate, w_up, w_down)


COMB_TM = 128


def _combine_kernel(pos_ref, h_ref, gate_ref, y_hbm, o_ref, ybuf, sem):
    i = pl.program_id(0)
    n = pl.num_programs(0)
    TM = COMB_TM

    def gather(blk, slot):
        base = blk * (2 * TM)
        for r in range(2 * TM):
            pltpu.make_async_copy(y_hbm.at[pl.ds(pos_ref[base + r], 1)], ybuf.at[slot, pl.ds(r, 1)],
                                  sem.at[slot]).start()

    def wait(slot):
        pltpu.make_async_copy(y_hbm.at[pl.ds(0, 2 * TM)], ybuf.at[slot], sem.at[slot]).wait()

    slot = lax.rem(i, 2)

    @pl.when(i == 0)
    def _():
        gather(0, 0)

    @pl.when(i + 1 < n)
    def _():
        gather(i + 1, 1 - slot)

    wait(slot)
    g = gate_ref[...]
    o_ref[...] = h_ref[...] + (g[:, 0:1] * ybuf[slot, 0:TM, :] + g[:, 1:2] * ybuf[slot, TM:2 * TM, :])


def _combine(pos, h, gate, y_slots):
    T, D = h.shape
    TM = COMB_TM
    grid_spec = pltpu.PrefetchScalarGridSpec(
        num_scalar_prefetch=1,
        grid=(T // TM,),
        in_specs=[pl.BlockSpec((TM, D), lambda i, p: (i, 0)), pl.BlockSpec((TM, LANES), lambda i, p: (i, 0)),
                  pl.BlockSpec(memory_space=pl.ANY)],
        out_specs=pl.BlockSpec((TM, D), lambda i, p: (i, 0)),
        scratch_shapes=[pltpu.VMEM((2, 2 * TM, D), F32), pltpu.SemaphoreType.DMA((2,))],
    )
    return pl.pallas_call(
        _combine_kernel,
        grid_spec=grid_spec,
        out_shape=jax.ShapeDtypeStruct((T, D), F32),
        compiler_params=_cparams(("arbitrary",)),
        name="combine",
    )(pos, h, gate, y_slots)


def _moe(h, xn, eid, gate, w_gate, w_up, w_down):
    T = h.shape[0]
    A = T * TOP_K
    BM = MOE_BM
    n_blocks = -(-(A + N_EXPERTS * (BM - 1)) // BM)
    e_flat = eid[:, :TOP_K].reshape(A)
    onehot = (e_flat[:, None] == jnp.arange(N_EXPERTS, dtype=jnp.int32)[None, :]).astype(jnp.int32)
    csum = jnp.cumsum(onehot, axis=0)
    counts = csum[-1]
    rank = jnp.take_along_axis(csum, e_flat[:, None], axis=1)[:, 0] - 1
    padded = (counts + BM - 1) // BM * BM
    pad_end = jnp.cumsum(padded)
    pad_start = pad_end - padded
    pos = (pad_start[e_flat] + rank).astype(jnp.int32)
    tok = jnp.arange(A, dtype=jnp.int32) // TOP_K
    filler = jnp.arange(n_blocks * BM, dtype=jnp.int32) % T
    slot_tok = filler.at[pos].set(tok)
    block_first = jnp.arange(n_blocks, dtype=jnp.int32) * BM
    block_e = jnp.minimum(jnp.sum(pad_end[None, :] <= block_first[:, None], axis=1), N_EXPERTS - 1).astype(jnp.int32)
    n_used = (pad_end[-1] // BM).astype(jnp.int32).reshape(1)
    y_slots = _experts(block_e, n_used, slot_tok, xn, w_gate, w_up, w_down)
    pos_tiles = pos.reshape(T // COMB_TM, COMB_TM, TOP_K).transpose(0, 2, 1).reshape(-1)
    return _combine(pos_tiles, h, gate, y_slots)


def _prep_weights(w_in, b_fox_f, w_router_group, b_router_group, w_router_expert, b_router_expert):
    D = w_in.shape[0]
    c = [0]
    for s in (FOX_W, FOX_W, FOX_W, FOX_HEADS, HGRN_W, HGRN_W, HGRN_W, HGRN_W, MEM_W):
        c.append(c[-1] + s)
    seg = lambda i: w_in[:, c[i]:c[i + 1]]
    last = jnp.concatenate([seg(8), seg(3), jnp.zeros((D, SEG_W - MEM_W - FOX_HEADS), w_in.dtype)], axis=1)
    w_seg = jnp.concatenate([seg(0), seg(1), seg(2), seg(4), seg(5), seg(6), seg(7), last], axis=1).astype(BF16)
    b_f_pad = jnp.zeros((1, LANES), F32).at[0, :FOX_HEADS].set(b_fox_f)
    n_r = N_GROUPS + N_EXPERTS
    w_r = jnp.zeros((D, LANES), F32).at[:, :N_GROUPS].set(w_router_group).at[:, N_GROUPS:n_r].set(w_router_expert)
    b_r = jnp.zeros((1, LANES), F32).at[0, :N_GROUPS].set(b_router_group).at[0, N_GROUPS:n_r].set(b_router_expert)
    wr_hi = w_r.astype(BF16)
    wr_lo = (w_r - wr_hi.astype(F32)).astype(BF16)
    return w_seg, b_f_pad, wr_hi, wr_lo, b_r


def kernel(x_prompt, x_sample, cache_fox_k, cache_fox_v, cache_fox_logf, cache_mem_k, cache_mem_v, state_hgrn, page_table, mem_prompt, g_attn_norm, w_in, b_fox_f, g_fox_q, g_fox_k, lb_logits, g_hgrn_out, g_mem_norm, w_mem_kv, g_mem_q, g_mem_k, w_out, g_ffn_norm, w_router_group, b_router_group, w_router_expert, b_router_expert, w_gate_e, w_up_e, w_down_e):
    assert w_in.shape[0] == 1, "single-layer step"
    Bp, S, D = x_prompt.shape
    Bd, L, _ = x_sample.shape
    n_pool, page = cache_fox_k.shape[1], cache_fox_k.shape[2]
    assert page == LANES and L <= SUBLANES
    M = mem_prompt.shape[1]
    l = 0
    row = lambda a: a[l].reshape(1, -1)
    head_major = lambda a: jnp.swapaxes(a, -3, -2)

    w_seg, b_f_pad, wr_hi, wr_lo, b_r = _prep_weights(
        w_in[l], b_fox_f[l], w_router_group[l], b_router_group[l], w_router_expert[l], b_router_expert[l])
    w_out_bf = w_out[l].astype(BF16)
    w_mkv_bf = w_mem_kv[l].astype(BF16)
    experts_w = (w_gate_e[l], w_up_e[l], w_down_e[l])
    proj_args = (row(g_attn_norm), w_seg, row(g_fox_q), row(g_fox_k), row(g_mem_q), lb_logits, b_f_pad)

    Tp = Bp * S
    xp = x_prompt.reshape(Tp, D)
    fq, fk_p, fkb, fv_p, fvb, hq, hlf, hv, hgs, mq, flf_p = _in_proj(xp, *proj_args, tm=512, seq=S)
    seq = lambda a: a.reshape(Bp, S, a.shape[-1])
    ck = _fox_cum(seq(flf_p))
    fox_o = _fox_attn(seq(fq), seq(fkb), seq(fvb), ck)
    hg_o, s_p = _hgrn(seq(hq), seq(hlf), seq(hv), seq(hgs), row(g_hgrn_out), None, C=128, n_heads=3, n_valid=128)
    mk, mv = _mem_kv(mem_prompt.reshape(Bp * M, D), row(g_mem_norm), w_mkv_bf, row(g_mem_k), tm=256)
    mem4 = lambda a, b: a.reshape(b, M, MEM_HEADS, HEAD_DIM)
    mem_o = _mem_attn(seq(mq), mem4(mk, Bp), mem4(mv, Bp), tq=512)
    h, xn, eid, gate = _merge(fox_o.reshape(Tp, FOX_W), hg_o.reshape(Tp, HGRN_W), mem_o.reshape(Tp, MEM_W),
                              xp, w_out_bf, row(g_ffn_norm), wr_hi, wr_lo, b_r, tm=256)
    y_p = _moe(h, xn, eid, gate, *experts_w)

    R = SUBLANES
    Ts = Bd * R
    xs = jnp.pad(x_sample, ((0, 0), (0, R - L), (0, 0))).reshape(Ts, D)
    fq, fk_s, fkb, fv_s, fvb, hq, hlf, hv, hgs, mq, flf_s = _in_proj(xs, *proj_args, tm=Ts, seq=R)
    seqs = lambda a: a.reshape(Bd, R, a.shape[-1])
    lf_new_t = jnp.swapaxes(seqs(flf_s)[:, :, :HEAD_PAD], 1, 2)
    lf_new_t = jnp.pad(lf_new_t, ((0, 0), (0, 0), (0, LANES - R)))
    cache_lf_t = jnp.pad(jnp.swapaxes(cache_fox_logf[l].astype(F32), 1, 2), ((0, 0), (0, HEAD_PAD - FOX_HEADS), (0, 0)))
    fox_o = _fox_dec(page_table, seqs(fq), seqs(fkb), seqs(fvb), lf_new_t, head_major(cache_fox_k[l]), head_major(cache_fox_v[l]),
                     cache_lf_t, n_tok=L)
    hg_o, s_s = _hgrn(seqs(hq), seqs(hlf), seqs(hv), seqs(hgs), row(g_hgrn_out), state_hgrn[l],
                      C=R, n_heads=HGRN_HEADS, n_valid=L)
    mem_o = _mem_attn(seqs(mq), cache_mem_k[l], cache_mem_v[l], tq=R)
    h, xn, eid, gate = _merge(fox_o.reshape(Ts, FOX_W), hg_o.reshape(Ts, HGRN_W), mem_o.reshape(Ts, MEM_W),
                              xs, w_out_bf, row(g_ffn_norm), wr_hi, wr_lo, b_r, tm=Ts)
    y_s = _moe(h, xn, eid, gate, *experts_w)

    cut = lambda a: a.reshape((Bd, R) + a.shape[1:])[:, :L]
    return (y_p.reshape(Bp, S, D), cut(y_s),
            head_major(fk_p)[None], head_major(fv_p)[None],
            flf_p[:, :FOX_HEADS].reshape(1, Bp, S, FOX_HEADS), s_p[None],
            mem4(mk, Bp)[None], mem4(mv, Bp)[None],
            head_major(fk_s)[None, :, :L], head_major(fv_s)[None, :, :L], cut(flf_s)[:, :, :FOX_HEADS][None], s_s[None])
```

```python
import functools

import jax
import jax.numpy as jnp
from jax import lax
from jax.experimental import pallas as pl
from jax.experimental.pallas import tpu as pltpu

F32 = jnp.float32
BF16 = jnp.bfloat16

HEAD_DIM = 128
FOX_HEADS = 6
HGRN_HEADS = 6
MEM_HEADS = 4
FOX_W = FOX_HEADS * HEAD_DIM
HGRN_W = HGRN_HEADS * HEAD_DIM
MEM_W = MEM_HEADS * HEAD_DIM
N_GROUPS = 4
EXPERTS_PER_GROUP = 8
N_EXPERTS = N_GROUPS * EXPERTS_PER_GROUP
TOP_K = 2
EPS = 1e-6
ATTN_SCALE = HEAD_DIM ** -0.5
LANES = 128
SUBLANES = 8
SEG_W = FOX_W
N_SEG = 8
PROJ_CHUNK = 256
HEAD_PAD = 8
MOE_BM = 256
VMEM_LIMIT = 52 * 1024 * 1024

NT_DIMS = (((1,), (1,)), ((), ()))
TN_DIMS = (((0,), (0,)), ((), ()))


def _cparams(sem, vmem=VMEM_LIMIT):
    return pltpu.CompilerParams(dimension_semantics=sem, vmem_limit_bytes=vmem)


def _split3(x):
    hi = x.astype(BF16)
    r1 = x - hi.astype(F32)
    mid = r1.astype(BF16)
    lo = (r1 - mid.astype(F32)).astype(BF16)
    return hi, mid, lo


def _dot(a, b):
    return jnp.dot(a, b, preferred_element_type=F32)


def _dot_nt(a, b):
    return lax.dot_general(a, b, NT_DIMS, preferred_element_type=F32)


def _sigmoid(x):
    return 1.0 / (1.0 + jnp.exp(-x))


def _log_sigmoid(x):
    return jnp.minimum(x, 0.0) - jnp.log(1.0 + jnp.exp(-jnp.abs(x)))


def _rms_heads(a, g, n_heads, scale):
    outs = []
    for h in range(n_heads):
        ah = a[:, h * HEAD_DIM:(h + 1) * HEAD_DIM]
        ms = jnp.mean(ah * ah, axis=-1, keepdims=True)
        outs.append(ah * lax.rsqrt(ms + EPS) * (g * scale))
    return jnp.concatenate(outs, axis=-1)


def _store_heads(ref, a, n_heads):
    for h in range(n_heads):
        ref[:, h, :] = a[:, h * HEAD_DIM:(h + 1) * HEAD_DIM]


def _store_head_major(ref, a, h0):
    nb, _, rows, _ = ref.shape
    for h in range(a.shape[1] // HEAD_DIM):
        ref[:, h0 + h] = a[:, h * HEAD_DIM:(h + 1) * HEAD_DIM].reshape(nb, rows, HEAD_DIM)


def _in_proj_kernel(x_ref, gn_ref, w_ref, gfq_ref, gfk_ref, gmq_ref, lbl_ref, bf_ref,
                    fq_ref, fk_ref, fkb_ref, fv_ref, fvb_ref, hq_ref, hlf_ref, hv_ref, hgs_ref, mq_ref, flf_ref,
                    xn_s):
    j = pl.program_id(1)

    @pl.when(j == 0)
    def _():
        x = x_ref[...]
        ms = jnp.mean(x * x, axis=-1, keepdims=True)
        xn_s[...] = (x * lax.rsqrt(ms + EPS) * gn_ref[...]).astype(BF16)

    def segment(post):
        for c in range(SEG_W // PROJ_CHUNK):
            cs = slice(c * PROJ_CHUNK, (c + 1) * PROJ_CHUNK)
            post(c, cs, _dot_nt(xn_s[...], w_ref[cs, :]))

    heads_per_chunk = PROJ_CHUNK // HEAD_DIM

    @pl.when(j == 0)
    def _():
        def post(c, cs, acc):
            fq_ref[:, cs] = _rms_heads(acc, gfq_ref[...], heads_per_chunk, ATTN_SCALE).astype(fq_ref.dtype)
        segment(post)

    @pl.when(j == 1)
    def _():
        def post(c, cs, acc):
            fk = _rms_heads(acc, gfk_ref[...], heads_per_chunk, 1.0)
            _store_head_major(fk_ref, fk, c * heads_per_chunk)
            fkb_ref[:, cs] = fk.astype(BF16)
        segment(post)

    @pl.when(j == 2)
    def _():
        def post(c, cs, acc):
            _store_head_major(fv_ref, acc, c * heads_per_chunk)
            fvb_ref[:, cs] = acc.astype(BF16)
        segment(post)

    @pl.when(j == 3)
    def _():
        def post(c, cs, acc):
            hq_ref[:, cs] = acc * _sigmoid(acc) * ATTN_SCALE
        segment(post)

    @pl.when(j == 4)
    def _():
        l = lbl_ref[...]
        mx = jnp.max(l, axis=0, keepdims=True)
        ex = jnp.exp(l - mx)
        lb = ex[0:1, :] / jnp.sum(ex, axis=0, keepdims=True)

        def post(c, cs, acc):
            hlf_ref[:, cs] = jnp.log(lb[:, cs] + (1.0 - lb[:, cs]) * _sigmoid(acc))
        segment(post)

    @pl.when(j == 5)
    def _():
        def post(c, cs, acc):
            hv_ref[:, cs] = acc
        segment(post)

    @pl.when(j == 6)
    def _():
        def post(c, cs, acc):
            hgs_ref[:, cs] = acc * _sigmoid(acc)
        segment(post)

    @pl.when(j == 7)
    def _():
        def post(c, cs, acc):
            if c < MEM_W // PROJ_CHUNK:
                mq_ref[:, cs] = _rms_heads(acc, gmq_ref[...], heads_per_chunk, ATTN_SCALE).astype(mq_ref.dtype)
            else:
                flf_ref[...] = _log_sigmoid(acc[:, :LANES] + bf_ref[...])
        segment(post)


def _in_proj(x, g_norm, w_seg, g_fq, g_fk, g_mq, lb_logits, b_f_pad, tm, seq):
    T, D = x.shape
    full = lambda a: pl.BlockSpec(a.shape, lambda i, j: (0,) * a.ndim)
    nb, rows = max(1, tm // seq), min(tm, seq)
    assert nb * rows == tm and seq % rows == 0
    tiles_per_seq = seq // rows
    kv_cache = jax.ShapeDtypeStruct((T // seq, FOX_HEADS, seq, HEAD_DIM), F32)
    kv_spec = pl.BlockSpec((nb, FOX_HEADS, rows, HEAD_DIM),
                           lambda i, j: (i // tiles_per_seq, 0, i % tiles_per_seq, 0))
    outs = [
        jax.ShapeDtypeStruct((T, FOX_W), BF16),
        kv_cache,
        jax.ShapeDtypeStruct((T, FOX_W), BF16),
        kv_cache,
        jax.ShapeDtypeStruct((T, FOX_W), BF16),
        jax.ShapeDtypeStruct((T, HGRN_W), F32),
        jax.ShapeDtypeStruct((T, HGRN_W), F32),
        jax.ShapeDtypeStruct((T, HGRN_W), F32),
        jax.ShapeDtypeStruct((T, HGRN_W), F32),
        jax.ShapeDtypeStruct((T, MEM_W), BF16),
        jax.ShapeDtypeStruct((T, LANES), F32),
    ]
    return pl.pallas_call(
        _in_proj_kernel,
        grid=(T // tm, N_SEG),
        in_specs=[pl.BlockSpec((tm, D), lambda i, j: (i, 0)), full(g_norm),
                  pl.BlockSpec((SEG_W, D), lambda i, j: (j, 0)),
                  full(g_fq), full(g_fk), full(g_mq), full(lb_logits), full(b_f_pad)],
        out_specs=[kv_spec if o is kv_cache else pl.BlockSpec((tm, o.shape[1]), lambda i, j: (i, 0)) for o in outs],
        out_shape=outs,
        scratch_shapes=[pltpu.VMEM((tm, D), BF16)],
        compiler_params=_cparams(("arbitrary", "arbitrary")),
        name="in_proj",
    )(x, g_norm, w_seg, g_fq, g_fk, g_mq, lb_logits, b_f_pad)


CUM_C = 256


def _fox_cum_kernel(lf_ref, ck_ref):
    S = lf_ref.shape[1]
    r = lax.broadcasted_iota(jnp.int32, (CUM_C, CUM_C), 0)
    c = lax.broadcasted_iota(jnp.int32, (CUM_C, CUM_C), 1)
    tri = jnp.where(c <= r, 1.0, 0.0).astype(BF16)
    carry = jnp.zeros((1, LANES), F32)
    for i in range(S // CUM_C):
        sl = slice(i * CUM_C, (i + 1) * CUM_C)
        hi, mid, lo = _split3(lf_ref[0, sl, :])
        cum = (_dot(tri, hi) + _dot(tri, mid)) + _dot(tri, lo) + carry
        carry = cum[CUM_C - 1:CUM_C, :]
        ck_ref[0, :, sl] = cum.T[0:HEAD_PAD, :]


def _fox_cum(flf):
    B, S, _ = flf.shape
    return pl.pallas_call(
        _fox_cum_kernel,
        grid=(B,),
        in_specs=[pl.BlockSpec((1, S, LANES), lambda b: (b, 0, 0))],
        out_specs=pl.BlockSpec((1, HEAD_PAD, S), lambda b: (b, 0, 0)),
        out_shape=jax.ShapeDtypeStruct((B, HEAD_PAD, S), F32),
        compiler_params=_cparams(("arbitrary",)),
        name="fox_cum",
    )(flf)


FOX_T = 256


def _softmax_step(s, v_bf, m, l, acc):
    m_new = jnp.maximum(m, jnp.max(s, axis=-1, keepdims=True))
    alpha = jnp.exp(m - m_new)
    p = jnp.exp(s - m_new)
    l = alpha * l + jnp.sum(p, axis=-1, keepdims=True)
    acc = alpha * acc + _dot(p.astype(BF16), v_bf)
    return m_new, l, acc


def _fox_attn_kernel(q_ref, k_ref, v_ref, ck_ref, o_ref, m_s, l_s, acc_s):
    qi = pl.program_id(1)
    T = FOX_T
    m_s[...] = jnp.full_like(m_s, -jnp.inf)
    l_s[...] = jnp.zeros_like(l_s)
    acc_s[...] = jnp.zeros_like(acc_s)
    row = lax.broadcasted_iota(jnp.int32, (T, T), 0)
    col = lax.broadcasted_iota(jnp.int32, (T, T), 1)
    causal = col <= row

    def tile(ks, masked):
        for h in range(FOX_HEADS):
            hs = slice(h * HEAD_DIM, (h + 1) * HEAD_DIM)
            s = _dot_nt(q_ref[0, :, hs], k_ref[0, pl.ds(ks, T), hs]) - ck_ref[0, h:h + 1, pl.ds(ks, T)]
            if masked:
                s = jnp.where(causal, s, -jnp.inf)
            m_old = m_s[:, hs]
            m_new = jnp.maximum(m_old, jnp.max(s, axis=-1, keepdims=True))
            alpha = jnp.exp(m_old - m_new)
            p = jnp.exp(s - jnp.concatenate([m_new] * (T // HEAD_DIM), axis=-1))
            m_s[:, hs] = m_new
            l_s[:, hs] = alpha * l_s[:, hs] + jnp.sum(p, axis=-1, keepdims=True)
            acc_s[:, hs] = alpha * acc_s[:, hs] + _dot(p.astype(BF16), v_ref[0, pl.ds(ks, T), hs])

    def body(kt, c):
        tile(pl.multiple_of(kt * T, T), False)
        return c

    lax.fori_loop(0, qi, body, 0)
    tile(pl.multiple_of(qi * T, T), True)
    o_ref[0] = (acc_s[...] / l_s[...]).astype(o_ref.dtype)


def _fox_attn(fq, fk, fv, ck):
    B, S, _ = fq.shape
    T = FOX_T
    return pl.pallas_call(
        _fox_attn_kernel,
        grid=(B, S // T),
        in_specs=[pl.BlockSpec((1, T, FOX_W), lambda b, i: (b, i, 0)),
                  pl.BlockSpec((1, S, FOX_W), lambda b, i: (b, 0, 0)),
                  pl.BlockSpec((1, S, FOX_W), lambda b, i: (b, 0, 0)),
                  pl.BlockSpec((1, HEAD_PAD, S), lambda b, i: (b, 0, 0))],
        out_specs=pl.BlockSpec((1, T, FOX_W), lambda b, i: (b, i, 0)),
        out_shape=jax.ShapeDtypeStruct((B, S, FOX_W), BF16),
        scratch_shapes=[pltpu.VMEM((T, FOX_W), F32)] * 3,
        compiler_params=_cparams(("arbitrary", "arbitrary")),
        name="fox_attn",
    )(fq, fk, fv, ck)


def _level_ref(b, m):
    C = b.shape[0]
    if 2 * m >= SUBLANES:
        b3 = b.reshape(C // (2 * m), 2 * m, HEAD_DIM)
        r = jnp.broadcast_to(b3[:, m - 1:m, :], b3.shape)
        return r.reshape(C, HEAD_DIM)
    b3 = b.reshape(C // SUBLANES, SUBLANES, HEAD_DIM)
    sub = lax.broadcasted_iota(jnp.int32, b3.shape, 1)
    pick = lambda i: jnp.broadcast_to(b3[:, i:i + 1, :], b3.shape)
    if m == 2:
        r = jnp.where(sub < 4, pick(1), pick(5))
    else:
        r = jnp.where(sub < 2, pick(0), jnp.where(sub < 4, pick(2), jnp.where(sub < 6, pick(4), pick(6))))
    return r.reshape(C, HEAD_DIM)


def _level_index(C):
    t = lax.broadcasted_iota(jnp.int32, (C, C), 0)
    s = lax.broadcasted_iota(jnp.int32, (C, C), 1)
    x = jnp.bitwise_xor(t, s)
    lvl = jnp.full((C, C), -1, jnp.int32)
    j, m = 0, 1
    while m < C:
        lvl = jnp.where(x >= m, j, lvl)
        j, m = j + 1, 2 * m
    return jnp.where(t > s, lvl, -1)


def _hgrn_chunk(q, g, v, st, tri, lvl, n_valid):
    C = q.shape[0]
    hi, mid, lo = _split3(g)
    b = (_dot(tri, hi) + _dot(tri, mid)) + _dot(tri, lo)
    k = 1.0 - jnp.exp(g)
    a = jnp.zeros((C, C), F32)
    j, m = 0, 1
    while m < C:
        e = jnp.exp(-jnp.abs(b - _level_ref(b, m)))
        a_l = _dot_nt((q * e).astype(BF16), (k * e).astype(BF16))
        a = jnp.where(lvl == j, a_l, a)
        j, m = j + 1, 2 * m
    v_bf = v.astype(BF16)
    diag = jnp.sum(q * k, axis=-1, keepdims=True)
    o = _dot_nt((q * jnp.exp(b)).astype(BF16), st.astype(BF16)) + _dot(a.astype(BF16), v_bf) + diag * v
    b_last = b[n_valid - 1:n_valid, :]
    kt = k * jnp.exp(jnp.minimum(b_last - b, 0.0))
    if n_valid < C:
        rows = lax.broadcasted_iota(jnp.int32, (C, HEAD_DIM), 0)
        kt = jnp.where(rows < n_valid, kt, 0.0)
    st_new = st * jnp.exp(b_last) + lax.dot_general(v_bf, kt.astype(BF16), TN_DIMS, preferred_element_type=F32)
    return o, st_new


def _hgrn_kernel(*refs, C, n_chunks, n_heads, n_valid, has_s0):
    if has_s0:
        q_ref, g_ref, v_ref, gs_ref, gn_ref, s0_ref, o_ref, sf_ref = refs
    else:
        q_ref, g_ref, v_ref, gs_ref, gn_ref, o_ref, sf_ref = refs
    r = lax.broadcasted_iota(jnp.int32, (C, C), 0)
    c = lax.broadcasted_iota(jnp.int32, (C, C), 1)
    tri = jnp.where(c <= r, 1.0, 0.0).astype(BF16)
    lvl = _level_index(C)
    gn = gn_ref[...]
    sts0 = tuple(s0_ref[0, h].T if has_s0 else jnp.zeros((HEAD_DIM, HEAD_DIM), F32) for h in range(n_heads))

    def body(ci, sts):
        rs = pl.ds(pl.multiple_of(ci * C, C), C)
        out = []
        for h in range(n_heads):
            hs = slice(h * HEAD_DIM, (h + 1) * HEAD_DIM)
            o, st = _hgrn_chunk(q_ref[0, rs, hs], g_ref[0, rs, hs], v_ref[0, rs, hs], sts[h], tri, lvl, n_valid)
            ms = jnp.mean(o * o, axis=-1, keepdims=True)
            o_ref[0, rs, hs] = (o * lax.rsqrt(ms + EPS) * gn * gs_ref[0, rs, hs]).astype(o_ref.dtype)
            out.append(st)
        return tuple(out)

    sts = lax.fori_loop(0, n_chunks, body, sts0) if n_chunks > 1 else body(0, sts0)
    for h in range(n_heads):
        sf_ref[0, h] = sts[h].T


def _hgrn(hq, hlf, hv, hgs, g_hn, s0, C, n_heads, n_valid):
    B, L, _ = hq.shape
    hp = HGRN_HEADS // n_heads
    w = n_heads * HEAD_DIM
    seq = pl.BlockSpec((1, L, w), lambda b, h: (b, 0, h))
    st_spec = pl.BlockSpec((1, n_heads, HEAD_DIM, HEAD_DIM), lambda b, h: (b, h, 0, 0))
    in_specs = [seq, seq, seq, seq, pl.BlockSpec((1, HEAD_DIM), lambda b, h: (0, 0))]
    args = [hq, hlf, hv, hgs, g_hn]
    if s0 is not None:
        in_specs.append(st_spec)
        args.append(s0)
    kern = functools.partial(_hgrn_kernel, C=C, n_chunks=L // C, n_heads=n_heads, n_valid=n_valid,
                             has_s0=s0 is not None)
    return pl.pallas_call(
        kern,
        grid=(B, hp),
        in_specs=in_specs,
        out_specs=[seq, st_spec],
        out_shape=[jax.ShapeDtypeStruct((B, L, HGRN_W), BF16),
                   jax.ShapeDtypeStruct((B, HGRN_HEADS, HEAD_DIM, HEAD_DIM), F32)],
        compiler_params=_cparams(("arbitrary", "arbitrary")),
        name="hgrn",
    )(*args)


def _mem_kv_kernel(x_ref, gn_ref, w_ref, gk_ref, mk_ref, mv_ref, xn_s):
    j = pl.program_id(1)

    @pl.when(j == 0)
    def _():
        x = x_ref[...]
        ms = jnp.mean(x * x, axis=-1, keepdims=True)
        xn_s[...] = (x * lax.rsqrt(ms + EPS) * gn_ref[...]).astype(BF16)

    acc = _dot(xn_s[...], w_ref[...])

    @pl.when(j == 0)
    def _():
        _store_heads(mk_ref, _rms_heads(acc, gk_ref[...], MEM_HEADS, 1.0), MEM_HEADS)

    @pl.when(j == 1)
    def _():
        _store_heads(mv_ref, acc, MEM_HEADS)


def _mem_kv(mem, g_norm, w, g_mk, tm):
    T, D = mem.shape
    out = jax.ShapeDtypeStruct((T, MEM_HEADS, HEAD_DIM), F32)
    return pl.pallas_call(
        _mem_kv_kernel,
        grid=(T // tm, 2),
        in_specs=[pl.BlockSpec((tm, D), lambda i, j: (i, 0)), pl.BlockSpec((1, D), lambda i, j: (0, 0)),
                  pl.BlockSpec((D, MEM_W), lambda i, j: (0, j)), pl.BlockSpec((1, HEAD_DIM), lambda i, j: (0, 0))],
        out_specs=[pl.BlockSpec((tm, MEM_HEADS, HEAD_DIM), lambda i, j: (i, 0, 0))] * 2,
        out_shape=[out, out],
        scratch_shapes=[pltpu.VMEM((tm, D), BF16)],
        compiler_params=_cparams(("arbitrary", "arbitrary")),
        name="mem_kv",
    )(mem, g_norm, w, g_mk)


def _mem_attn_kernel(q_ref, k_ref, v_ref, o_ref):
    for h in range(MEM_HEADS):
        hs = slice(h * HEAD_DIM, (h + 1) * HEAD_DIM)
        s = _dot_nt(q_ref[0, :, hs], k_ref[0, :, h, :].astype(BF16))
        p = jnp.exp(s - jnp.max(s, axis=-1, keepdims=True))
        l = jnp.sum(p, axis=-1, keepdims=True)
        o_ref[0, :, hs] = (_dot(p.astype(BF16), v_ref[0, :, h, :].astype(BF16)) / l).astype(o_ref.dtype)


def _mem_attn(mq, mk, mv, tq):
    B, L, _ = mq.shape
    M = mk.shape[1]
    kv = pl.BlockSpec((1, M, MEM_HEADS, HEAD_DIM), lambda b, i: (b, 0, 0, 0))
    qo = pl.BlockSpec((1, tq, MEM_W), lambda b, i: (b, i, 0))
    return pl.pallas_call(
        _mem_attn_kernel,
        grid=(B, L // tq),
        in_specs=[qo, kv, kv],
        out_specs=qo,
        out_shape=jax.ShapeDtypeStruct((B, L, MEM_W), BF16),
        compiler_params=_cparams(("arbitrary", "arbitrary")),
        name="mem_attn",
    )(mq, mk, mv)


DEC_G = 8
DEC_ROWS = 4 * HEAD_PAD


def _suffix_sum_lanes(x):
    lane = lax.broadcasted_iota(jnp.int32, x.shape, 1)
    s = 1
    while s < LANES:
        x = x + jnp.where(lane + s < LANES, pltpu.roll(x, LANES - s, 1), 0.0)
        s *= 2
    return x


def _prefix_sum_lanes(x):
    lane = lax.broadcasted_iota(jnp.int32, x.shape, 1)
    s = 1
    while s < LANES:
        x = x + jnp.where(lane >= s, pltpu.roll(x, s, 1), 0.0)
        s *= 2
    return x


def _fox_dec_kernel(pt_ref, q_ref, kn_ref, vn_ref, lfn_ref, *refs, n_tok):
    G = DEC_G
    k_refs, v_refs, lf_refs = refs[0:G], refs[G:2 * G], refs[2 * G:3 * G]
    o_ref = refs[3 * G]
    qbd, kn_s, vn_s, m_s, l_s, acc_s, car_s = refs[3 * G + 1:]
    st = pl.program_id(1)

    def attend(k_bf, v_bf, bias):
        s = _dot_nt(qbd[...].astype(BF16), k_bf) + bias
        m_new, l_new, acc_new = _softmax_step(s, v_bf, m_s[...], l_s[...], acc_s[...])
        m_s[...] = m_new
        l_s[...] = l_new
        acc_s[...] = acc_new

    @pl.when(st == 0)
    def _():
        qbd[...] = jnp.zeros_like(qbd)
        q = q_ref[0].astype(F32)
        for t in range(n_tok):
            for h in range(FOX_HEADS):
                hs = slice(h * HEAD_DIM, (h + 1) * HEAD_DIM)
                qbd[t * HEAD_PAD + h:t * HEAD_PAD + h + 1, hs] = q[t:t + 1, hs]
        kn_s[...] = jnp.zeros_like(kn_s)
        vn_s[...] = jnp.zeros_like(vn_s)
        kn_s[0:SUBLANES, :] = kn_ref[0].astype(F32)
        vn_s[0:SUBLANES, :] = vn_ref[0].astype(F32)
        m_s[...] = jnp.full_like(m_s, -jnp.inf)
        l_s[...] = jnp.zeros_like(l_s)
        acc_s[...] = jnp.zeros_like(acc_s)
        car_s[...] = jnp.zeros_like(car_s)
        ecum = _prefix_sum_lanes(lfn_ref[0])
        lane = lax.broadcasted_iota(jnp.int32, (HEAD_PAD, LANES), 1)
        bias = jnp.concatenate([jnp.where(lane <= t, -ecum, -jnp.inf) for t in range(n_tok)], axis=0)
        attend(kn_s[...].astype(BF16), vn_s[...].astype(BF16), bias)

    heads = lambda ref: jnp.concatenate([ref[0, h] for h in range(FOX_HEADS)], axis=-1).astype(BF16)
    carry = car_s[...]
    ds = []
    for i in range(G):
        lf = lf_refs[i][0]
        incl = _suffix_sum_lanes(lf)
        ds.append(carry + (incl - lf))
        carry = carry + incl[:, 0:1]
    car_s[...] = carry
    d_all = jnp.concatenate(ds, axis=-1)
    attend(jnp.concatenate([heads(r) for r in k_refs], axis=0),
           jnp.concatenate([heads(r) for r in v_refs], axis=0),
           jnp.concatenate([d_all] * n_tok, axis=0))

    @pl.when(st == pl.num_programs(1) - 1)
    def _():
        res = acc_s[...] / l_s[...]
        o_ref[...] = jnp.zeros_like(o_ref)
        for t in range(n_tok):
            for h in range(FOX_HEADS):
                hs = slice(h * HEAD_DIM, (h + 1) * HEAD_DIM)
                r = t * HEAD_PAD + h
                o_ref[0, t:t + 1, hs] = res[r:r + 1, hs].astype(o_ref.dtype)


def _fox_dec(page_table, fq, fk, fv, lf_new_t, cache_k, cache_v, cache_lf_t, n_tok):
    B, n_pages = page_table.shape
    G = DEC_G
    assert n_pages % G == 0 and n_tok * HEAD_PAD == DEC_ROWS
    n_steps = n_pages // G
    tok = lambda w: pl.BlockSpec((1, SUBLANES, w), lambda b, s, pt: (b, 0, 0))

    def page_spec(shape, i):
        return pl.BlockSpec((1,) + shape,
                            lambda b, s, pt: (pt[b * n_pages + (n_pages - 1 - (s * G + i))],) + (0,) * len(shape))

    kv_page = (FOX_HEADS, LANES, HEAD_DIM)
    in_specs = ([tok(FOX_W), tok(FOX_W), tok(FOX_W), tok(LANES)]
                + [page_spec(kv_page, i) for i in range(G)]
                + [page_spec(kv_page, i) for i in range(G)]
                + [page_spec((HEAD_PAD, LANES), i) for i in range(G)])
    grid_spec = pltpu.PrefetchScalarGridSpec(
        num_scalar_prefetch=1,
        grid=(B, n_steps),
        in_specs=in_specs,
        out_specs=pl.BlockSpec((1, SUBLANES, FOX_W), lambda b, s, pt: (b, 0, 0)),
        scratch_shapes=[pltpu.VMEM((DEC_ROWS, FOX_W), F32),
                        pltpu.VMEM((LANES, FOX_W), F32), pltpu.VMEM((LANES, FOX_W), F32),
                        pltpu.VMEM((DEC_ROWS, 1), F32), pltpu.VMEM((DEC_ROWS, 1), F32),
                        pltpu.VMEM((DEC_ROWS, FOX_W), F32), pltpu.VMEM((HEAD_PAD, 1), F32)],
    )
    return pl.pallas_call(
        functools.partial(_fox_dec_kernel, n_tok=n_tok),
        grid_spec=grid_spec,
        out_shape=jax.ShapeDtypeStruct((B, SUBLANES, FOX_W), BF16),
        compiler_params=_cparams(("arbitrary", "arbitrary")),
        name="fox_dec",
    )(page_table.reshape(-1), fq, fk, fv, lf_new_t, *([cache_k] * G), *([cache_v] * G), *([cache_lf_t] * G))


def _merge_kernel(fo_ref, ho_ref, mo_ref, x_ref, w_ref, gf_ref, wrh_ref, wrl_ref, br_ref,
                  h_ref, xn_ref, eid_ref, gate_ref, cat_s):
    D = x_ref.shape[1]
    cat_s[:, 0:FOX_W] = fo_ref[...]
    cat_s[:, FOX_W:FOX_W + HGRN_W] = ho_ref[...]
    cat_s[:, FOX_W + HGRN_W:] = mo_ref[...]
    ssq = jnp.zeros((x_ref.shape[0], 1), F32)
    for c in range(D // PROJ_CHUNK):
        cs = slice(c * PROJ_CHUNK, (c + 1) * PROJ_CHUNK)
        hc = x_ref[:, cs] + _dot(cat_s[...], w_ref[:, cs])
        h_ref[:, cs] = hc
        ssq = ssq + jnp.sum(hc * hc, axis=-1, keepdims=True)
    xn = h_ref[...] * lax.rsqrt(ssq * (1.0 / D) + EPS) * gf_ref[...]
    xn_ref[...] = xn
    x_hi = xn.astype(BF16)
    x_lo = (xn - x_hi.astype(F32)).astype(BF16)
    logits = (_dot(x_hi, wrh_ref[...]) + (_dot(x_hi, wrl_ref[...]) + _dot(x_lo, wrh_ref[...]))) + br_ref[...]
    lane = lax.broadcasted_iota(jnp.int32, logits.shape, 1)
    big = jnp.int32(LANES)
    ninf = -jnp.inf
    gl = jnp.where(lane < N_GROUPS, logits, ninf)
    gmax = jnp.max(gl, axis=-1, keepdims=True)
    g_sel = jnp.min(jnp.where(gl == gmax, lane, big), axis=-1, keepdims=True)
    g_prob = 1.0 / jnp.sum(jnp.exp(gl - gmax), axis=-1, keepdims=True)
    lo = N_GROUPS + EXPERTS_PER_GROUP * g_sel
    el = jnp.where((lane >= lo) & (lane < lo + EXPERTS_PER_GROUP), logits, ninf)
    v1 = jnp.max(el, axis=-1, keepdims=True)
    i1 = jnp.min(jnp.where(el == v1, lane, big), axis=-1, keepdims=True)
    el2 = jnp.where(lane == i1, ninf, el)
    v2 = jnp.max(el2, axis=-1, keepdims=True)
    i2 = jnp.min(jnp.where(el2 == v2, lane, big), axis=-1, keepdims=True)
    t = jnp.exp(v2 - v1)
    w1 = g_prob / (1.0 + t)
    w2 = g_prob * t / (1.0 + t)
    eid_ref[...] = jnp.where(lane == 0, i1 - N_GROUPS, jnp.where(lane == 1, i2 - N_GROUPS, 0))
    gate_ref[...] = jnp.where(lane == 0, w1, jnp.where(lane == 1, w2, 0.0))


def _merge(fo, ho, mo, x, w_out, g_ffn, wr_hi, wr_lo, b_r, tm):
    T, D = x.shape
    row = lambda w: pl.BlockSpec((tm, w), lambda i: (i, 0))
    full = lambda a: pl.BlockSpec(a.shape, lambda i: (0,) * a.ndim)
    return pl.pallas_call(
        _merge_kernel,
        grid=(T // tm,),
        in_specs=[row(FOX_W), row(HGRN_W), row(MEM_W), row(D), full(w_out), full(g_ffn),
                  full(wr_hi), full(wr_lo), full(b_r)],
        out_specs=[row(D), row(D), row(LANES), row(LANES)],
        out_shape=[jax.ShapeDtypeStruct((T, D), F32), jax.ShapeDtypeStruct((T, D), F32),
                   jax.ShapeDtypeStruct((T, LANES), jnp.int32), jax.ShapeDtypeStruct((T, LANES), F32)],
        scratch_shapes=[pltpu.VMEM((tm, FOX_W + HGRN_W + MEM_W), BF16)],
        compiler_params=_cparams(("arbitrary",)),
        name="merge",
    )(fo, ho, mo, x, w_out, g_ffn, wr_hi, wr_lo, b_r)


def _experts_kernel(be_ref, nb_ref, tok_ref, x_hbm, wg_ref, wu_ref, wd_ref, y_ref, xbuf, wg_s, wu_s, wd_s, sem):
    b = pl.program_id(0)

    @pl.when((b == 0) | (be_ref[b] != be_ref[jnp.maximum(b - 1, 0)]))
    def _():
        wg_s[...] = wg_ref[0].astype(BF16)
        wu_s[...] = wu_ref[0].astype(BF16)
        wd_s[...] = wd_ref[0].astype(BF16)

    n_used = nb_ref[0]
    BM = MOE_BM

    def gather(blk, slot):
        base = blk * BM
        for r in range(BM):
            pltpu.make_async_copy(x_hbm.at[pl.ds(tok_ref[base + r], 1)], xbuf.at[slot, pl.ds(r, 1)],
                                  sem.at[slot]).start()

    def wait(slot):
        pltpu.make_async_copy(x_hbm.at[pl.ds(0, BM)], xbuf.at[slot], sem.at[slot]).wait()

    slot = lax.rem(b, 2)

    @pl.when(b == 0)
    def _():
        gather(0, 0)

    @pl.when(b + 1 < n_used)
    def _():
        gather(b + 1, 1 - slot)

    @pl.when(b < n_used)
    def _():
        wait(slot)
        x = xbuf[slot].astype(BF16)
        hmid = _dot(x, wg_s[...])
        hmid = hmid * _sigmoid(hmid) * _dot(x, wu_s[...])
        y_ref[...] = _dot(hmid.astype(BF16), wd_s[...])

    @pl.when(b >= n_used)
    def _():
        y_ref[...] = jnp.zeros_like(y_ref)


def _experts(block_e, n_used, slot_tok, xn, w_gate, w_up, w_down):
    n_blocks = block_e.shape[0]
    T, D = xn.shape
    FF = w_gate.shape[2]
    BM = MOE_BM
    grid_spec = pltpu.PrefetchScalarGridSpec(
        num_scalar_prefetch=3,
        grid=(n_blocks,),
        in_specs=[pl.BlockSpec(memory_space=pl.ANY),
                  pl.BlockSpec((1, D, FF), lambda b, be, nb, tk: (be[b], 0, 0)),
                  pl.BlockSpec((1, D, FF), lambda b, be, nb, tk: (be[b], 0, 0)),
                  pl.BlockSpec((1, FF, D), lambda b, be, nb, tk: (be[b], 0, 0))],
        out_specs=pl.BlockSpec((BM, D), lambda b, be, nb, tk: (b, 0)),
        scratch_shapes=[pltpu.VMEM((2, BM, D), F32), pltpu.VMEM((D, FF), BF16), pltpu.VMEM((D, FF), BF16),
                        pltpu.VMEM((FF, D), BF16), pltpu.SemaphoreType.DMA((2,))],
    )
    return pl.pallas_call(
        _experts_kernel,
        grid_spec=grid_spec,
        out_shape=jax.ShapeDtypeStruct((n_blocks * BM, D), F32),
        compiler_params=_cparams(("arbitrary",)),
        name="experts",
    )(block_e, n_used, slot_tok, xn, w_gate, w_up, w_down)


COMB_TM = 128


def _combine_kernel(pos_ref, h_ref, gate_ref, y_hbm, o_ref, ybuf, sem):
    i = pl.program_id(0)
    n = pl.num_programs(0)
    TM = COMB_TM

    def gather(blk, slot):
        base = blk * (2 * TM)
        for r in range(2 * TM):
            pltpu.make_async_copy(y_hbm.at[pl.ds(pos_ref[base + r], 1)], ybuf.at[slot, pl.ds(r, 1)],
                                  sem.at[slot]).start()

    def wait(slot):
        pltpu.make_async_copy(y_hbm.at[pl.ds(0, 2 * TM)], ybuf.at[slot], sem.at[slot]).wait()

    slot = lax.rem(i, 2)

    @pl.when(i == 0)
    def _():
        gather(0, 0)

    @pl.when(i + 1 < n)
    def _():
        gather(i + 1, 1 - slot)

    wait(slot)
    g = gate_ref[...]
    o_ref[...] = h_ref[...] + (g[:, 0:1] * ybuf[slot, 0:TM, :] + g[:, 1:2] * ybuf[slot, TM:2 * TM, :])


def _combine(pos, h, gate, y_slots):
    T, D = h.shape
    TM = COMB_TM
    grid_spec = pltpu.PrefetchScalarGridSpec(
        num_scalar_prefetch=1,
        grid=(T // TM,),
        in_specs=[pl.BlockSpec((TM, D), lambda i, p: (i, 0)), pl.BlockSpec((TM, LANES), lambda i, p: (i, 0)),
                  pl.BlockSpec(memory_space=pl.ANY)],
        out_specs=pl.BlockSpec((TM, D), lambda i, p: (i, 0)),
        scratch_shapes=[pltpu.VMEM((2, 2 * TM, D), F32), pltpu.SemaphoreType.DMA((2,))],
    )
    return pl.pallas_call(
        _combine_kernel,
        grid_spec=grid_spec,
        out_shape=jax.ShapeDtypeStruct((T, D), F32),
        compiler_params=_cparams(("arbitrary",)),
        name="combine",
    )(pos, h, gate, y_slots)


def _moe(h, xn, eid, gate, w_gate, w_up, w_down):
    T = h.shape[0]
    A = T * TOP_K
    BM = MOE_BM
    n_blocks = -(-(A + N_EXPERTS * (BM - 1)) // BM)
    e_flat = eid[:, :TOP_K].reshape(A)
    onehot = (e_flat[:, None] == jnp.arange(N_EXPERTS, dtype=jnp.int32)[None, :]).astype(jnp.int32)
    csum = jnp.cumsum(onehot, axis=0)
    counts = csum[-1]
    rank = jnp.take_along_axis(csum, e_flat[:, None], axis=1)[:, 0] - 1
    padded = (counts + BM - 1) // BM * BM
    pad_end = jnp.cumsum(padded)
    pad_start = pad_end - padded
    pos = (pad_start[e_flat] + rank).astype(jnp.int32)
    tok = jnp.arange(A, dtype=jnp.int32) // TOP_K
    filler = jnp.arange(n_blocks * BM, dtype=jnp.int32) % T
    slot_tok = filler.at[pos].set(tok)
    block_first = jnp.arange(n_blocks, dtype=jnp.int32) * BM
    block_e = jnp.minimum(jnp.sum(pad_end[None, :] <= block_first[:, None], axis=1), N_EXPERTS - 1).astype(jnp.int32)
    n_used = (pad_end[-1] // BM).astype(jnp.int32).reshape(1)
    y_slots = _experts(block_e, n_used, slot_tok, xn, w_gate, w_up, w_down)
    pos_tiles = pos.reshape(T // COMB_TM, COMB_TM, TOP_K).transpose(0, 2, 1).reshape(-1)
    return _combine(pos_tiles, h, gate, y_slots)


def _prep_weights(w_in, b_fox_f, w_router_group, b_router_group, w_router_expert, b_router_expert):
    D = w_in.shape[0]
    c = [0]
    for s in (FOX_W, FOX_W, FOX_W, FOX_HEADS, HGRN_W, HGRN_W, HGRN_W, HGRN_W, MEM_W):
        c.append(c[-1] + s)
    w_t = w_in.T
    seg = lambda i: w_t[c[i]:c[i + 1]]
    pad = jnp.zeros((SEG_W - MEM_W - FOX_HEADS, D), w_in.dtype)
    w_seg = jnp.concatenate([seg(0), seg(1), seg(2), seg(4), seg(5), seg(6), seg(7), seg(8), seg(3), pad],
                            axis=0).astype(BF16)
    b_f_pad = jnp.zeros((1, LANES), F32).at[0, :FOX_HEADS].set(b_fox_f)
    n_r = N_GROUPS + N_EXPERTS
    w_r = jnp.zeros((D, LANES), F32).at[:, :N_GROUPS].set(w_router_group).at[:, N_GROUPS:n_r].set(w_router_expert)
    b_r = jnp.zeros((1, LANES), F32).at[0, :N_GROUPS].set(b_router_group).at[0, N_GROUPS:n_r].set(b_router_expert)
    wr_hi = w_r.astype(BF16)
    wr_lo = (w_r - wr_hi.astype(F32)).astype(BF16)
    return w_seg, b_f_pad, wr_hi, wr_lo, b_r


def kernel(x_prompt, x_sample, cache_fox_k, cache_fox_v, cache_fox_logf, cache_mem_k, cache_mem_v, state_hgrn, page_table, mem_prompt, g_attn_norm, w_in, b_fox_f, g_fox_q, g_fox_k, lb_logits, g_hgrn_out, g_mem_norm, w_mem_kv, g_mem_q, g_mem_k, w_out, g_ffn_norm, w_router_group, b_router_group, w_router_expert, b_router_expert, w_gate_e, w_up_e, w_down_e):
    assert w_in.shape[0] == 1, "single-layer step"
    Bp, S, D = x_prompt.shape
    Bd, L, _ = x_sample.shape
    n_pool, page = cache_fox_k.shape[1], cache_fox_k.shape[2]
    assert page == LANES and L <= SUBLANES
    M = mem_prompt.shape[1]
    l = 0
    row = lambda a: a[l].reshape(1, -1)
    head_major = lambda a: jnp.swapaxes(a, -3, -2)

    w_seg, b_f_pad, wr_hi, wr_lo, b_r = _prep_weights(
        w_in[l], b_fox_f[l], w_router_group[l], b_router_group[l], w_router_expert[l], b_router_expert[l])
    w_out_bf = w_out[l].astype(BF16)
    w_mkv_bf = w_mem_kv[l].astype(BF16)
    experts_w = (w_gate_e[l], w_up_e[l], w_down_e[l])
    proj_args = (row(g_attn_norm), w_seg, row(g_fox_q), row(g_fox_k), row(g_mem_q), lb_logits, b_f_pad)

    Tp = Bp * S
    xp = x_prompt.reshape(Tp, D)
    fq, fk_p, fkb, fv_p, fvb, hq, hlf, hv, hgs, mq, flf_p = _in_proj(xp, *proj_args, tm=512, seq=S)
    seq = lambda a: a.reshape(Bp, S, a.shape[-1])
    ck = _fox_cum(seq(flf_p))
    fox_o = _fox_attn(seq(fq), seq(fkb), seq(fvb), ck)
    hg_o, s_p = _hgrn(seq(hq), seq(hlf), seq(hv), seq(hgs), row(g_hgrn_out), None, C=128, n_heads=3, n_valid=128)
    mk, mv = _mem_kv(mem_prompt.reshape(Bp * M, D), row(g_mem_norm), w_mkv_bf, row(g_mem_k), tm=256)
    mem4 = lambda a, b: a.reshape(b, M, MEM_HEADS, HEAD_DIM)
    mem_o = _mem_attn(seq(mq), mem4(mk, Bp), mem4(mv, Bp), tq=512)
    h, xn, eid, gate = _merge(fox_o.reshape(Tp, FOX_W), hg_o.reshape(Tp, HGRN_W), mem_o.reshape(Tp, MEM_W),
                              xp, w_out_bf, row(g_ffn_norm), wr_hi, wr_lo, b_r, tm=256)
    y_p = _moe(h, xn, eid, gate, *experts_w)

    R = SUBLANES
    Ts = Bd * R
    xs = jnp.pad(x_sample, ((0, 0), (0, R - L), (0, 0))).reshape(Ts, D)
    fq, fk_s, fkb, fv_s, fvb, hq, hlf, hv, hgs, mq, flf_s = _in_proj(xs, *proj_args, tm=Ts, seq=R)
    seqs = lambda a: a.reshape(Bd, R, a.shape[-1])
    lf_new_t = jnp.swapaxes(seqs(flf_s)[:, :, :HEAD_PAD], 1, 2)
    lf_new_t = jnp.pad(lf_new_t, ((0, 0), (0, 0), (0, LANES - R)))
    cache_lf_t = jnp.pad(jnp.swapaxes(cache_fox_logf[l].astype(F32), 1, 2), ((0, 0), (0, HEAD_PAD - FOX_HEADS), (0, 0)))
    fox_o = _fox_dec(page_table, seqs(fq), seqs(fkb), seqs(fvb), lf_new_t, head_major(cache_fox_k[l]), head_major(cache_fox_v[l]),
                     cache_lf_t, n_tok=L)
    hg_o, s_s = _hgrn(seqs(hq), seqs(hlf), seqs(hv), seqs(hgs), row(g_hgrn_out), state_hgrn[l],
                      C=R, n_heads=HGRN_HEADS, n_valid=L)
    mem_o = _mem_attn(seqs(mq), cache_mem_k[l], cache_mem_v[l], tq=R)
    h, xn, eid, gate = _merge(fox_o.reshape(Ts, FOX_W), hg_o.reshape(Ts, HGRN_W), mem_o.reshape(Ts, MEM_W),
                              xs, w_out_bf, row(g_ffn_norm), wr_hi, wr_lo, b_r, tm=Ts)
    y_s = _moe(h, xn, eid, gate, *experts_w)

    cut = lambda a: a.reshape((Bd, R) + a.shape[1:])[:, :L]
    return (y_p.reshape(Bp, S, D), cut(y_s),
            head_major(fk_p)[None], head_major(fv_p)[None],
            flf_p[:, :FOX_HEADS].reshape(1, Bp, S, FOX_HEADS), s_p[None],
            mem4(mk, Bp)[None], mem4(mv, Bp)[None],
            head_major(fk_s)[None, :, :L], head_major(fv_s)[None, :, :L], cut(flf_s)[:, :, :FOX_HEADS][None], s_s[None])
```

```python
import functools

import jax
import jax.numpy as jnp
from jax import lax
from jax.experimental import pallas as pl
from jax.experimental.pallas import tpu as pltpu

F32 = jnp.float32
BF16 = jnp.bfloat16

HEAD_DIM = 128
FOX_HEADS = 6
HGRN_HEADS = 6
MEM_HEADS = 4
FOX_W = FOX_HEADS * HEAD_DIM
HGRN_W = HGRN_HEADS * HEAD_DIM
MEM_W = MEM_HEADS * HEAD_DIM
N_GROUPS = 4
EXPERTS_PER_GROUP = 8
N_EXPERTS = N_GROUPS * EXPERTS_PER_GROUP
TOP_K = 2
EPS = 1e-6
ATTN_SCALE = HEAD_DIM ** -0.5
LANES = 128
SUBLANES = 8
SEG_W = FOX_W
N_SEG = 8
PROJ_CHUNK = 256
HEAD_PAD = 8
MOE_BM = 256
VMEM_LIMIT = 52 * 1024 * 1024

NT_DIMS = (((1,), (1,)), ((), ()))
TN_DIMS = (((0,), (0,)), ((), ()))


def _cparams(sem, vmem=VMEM_LIMIT):
    return pltpu.CompilerParams(dimension_semantics=sem, vmem_limit_bytes=vmem)


def _split3(x):
    hi = x.astype(BF16)
    r1 = x - hi.astype(F32)
    mid = r1.astype(BF16)
    lo = (r1 - mid.astype(F32)).astype(BF16)
    return hi, mid, lo


def _dot(a, b):
    return jnp.dot(a, b, preferred_element_type=F32)


def _dot_nt(a, b):
    return lax.dot_general(a, b, NT_DIMS, preferred_element_type=F32)


def _sigmoid(x):
    return 1.0 / (1.0 + jnp.exp(-x))


def _log_sigmoid(x):
    return jnp.minimum(x, 0.0) - jnp.log(1.0 + jnp.exp(-jnp.abs(x)))


def _rms_heads(a, g, n_heads, scale):
    outs = []
    for h in range(n_heads):
        ah = a[:, h * HEAD_DIM:(h + 1) * HEAD_DIM]
        ms = jnp.mean(ah * ah, axis=-1, keepdims=True)
        outs.append(ah * lax.rsqrt(ms + EPS) * (g * scale))
    return jnp.concatenate(outs, axis=-1)


def _store_heads(ref, a, n_heads):
    for h in range(n_heads):
        ref[:, h, :] = a[:, h * HEAD_DIM:(h + 1) * HEAD_DIM]


def _store_head_major(ref, a, h0):
    nb, _, rows, _ = ref.shape
    for h in range(a.shape[1] // HEAD_DIM):
        ref[:, h0 + h] = a[:, h * HEAD_DIM:(h + 1) * HEAD_DIM].reshape(nb, rows, HEAD_DIM)


def _in_proj_kernel(x_ref, gn_ref, wa_ref, wb_ref, wf_ref, gfq_ref, gfk_ref, gmq_ref, lbl_ref, bf_ref,
                    fq_ref, fk_ref, fkb_ref, fv_ref, fvb_ref, hq_ref, hlf_ref, hv_ref, hgs_ref, mq_ref, flf_ref,
                    xn_s):
    j = pl.program_id(1)

    @pl.when(j == 0)
    def _():
        x = x_ref[...]
        ms = jnp.mean(x * x, axis=-1, keepdims=True)
        xn_s[...] = (x * lax.rsqrt(ms + EPS) * gn_ref[...]).astype(BF16)

    def segment(post, w_ref=None, width=SEG_W):
        w_ref = wb_ref if w_ref is None else w_ref
        for c in range(width // PROJ_CHUNK):
            cs = slice(c * PROJ_CHUNK, (c + 1) * PROJ_CHUNK)
            post(c, cs, _dot_nt(xn_s[...], w_ref[cs, :]))

    heads_per_chunk = PROJ_CHUNK // HEAD_DIM

    @pl.when(j == 0)
    def _():
        def post(c, cs, acc):
            fq_ref[:, cs] = _rms_heads(acc, gfq_ref[...], heads_per_chunk, ATTN_SCALE).astype(fq_ref.dtype)
        segment(post, wa_ref)

    @pl.when(j == 1)
    def _():
        def post(c, cs, acc):
            fk = _rms_heads(acc, gfk_ref[...], heads_per_chunk, 1.0)
            _store_head_major(fk_ref, fk, c * heads_per_chunk)
            fkb_ref[:, cs] = fk.astype(BF16)
        segment(post, wa_ref)

    @pl.when(j == 2)
    def _():
        def post(c, cs, acc):
            _store_head_major(fv_ref, acc, c * heads_per_chunk)
            fvb_ref[:, cs] = acc.astype(BF16)
        segment(post, wa_ref)

    @pl.when(j == 3)
    def _():
        def post(c, cs, acc):
            hq_ref[:, cs] = acc * _sigmoid(acc) * ATTN_SCALE
        segment(post)

    @pl.when(j == 4)
    def _():
        l = lbl_ref[...]
        mx = jnp.max(l, axis=0, keepdims=True)
        ex = jnp.exp(l - mx)
        lb = ex[0:1, :] / jnp.sum(ex, axis=0, keepdims=True)

        def post(c, cs, acc):
            hlf_ref[:, cs] = jnp.log(lb[:, cs] + (1.0 - lb[:, cs]) * _sigmoid(acc))
        segment(post)

    @pl.when(j == 5)
    def _():
        def post(c, cs, acc):
            hv_ref[:, cs] = acc
        segment(post)

    @pl.when(j == 6)
    def _():
        def post(c, cs, acc):
            hgs_ref[:, cs] = acc * _sigmoid(acc)
        segment(post)

    @pl.when(j == 7)
    def _():
        def post(c, cs, acc):
            mq_ref[:, cs] = _rms_heads(acc, gmq_ref[...], heads_per_chunk, ATTN_SCALE).astype(mq_ref.dtype)
        segment(post, width=MEM_W)
        flf_ref[...] = _log_sigmoid(_dot_nt(xn_s[...], wf_ref[...]) + bf_ref[...])


def _in_proj(x, g_norm, w_a, w_b, w_f, g_fq, g_fk, g_mq, lb_logits, b_f_pad, tm, seq):
    T, D = x.shape
    n_a = w_a.shape[0] // SEG_W
    full = lambda a: pl.BlockSpec(a.shape, lambda i, j: (0,) * a.ndim)
    nb, rows = max(1, tm // seq), min(tm, seq)
    assert nb * rows == tm and seq % rows == 0
    tiles_per_seq = seq // rows
    kv_cache = jax.ShapeDtypeStruct((T // seq, FOX_HEADS, seq, HEAD_DIM), F32)
    kv_spec = pl.BlockSpec((nb, FOX_HEADS, rows, HEAD_DIM),
                           lambda i, j: (i // tiles_per_seq, 0, i % tiles_per_seq, 0))
    outs = [
        jax.ShapeDtypeStruct((T, FOX_W), BF16),
        kv_cache,
        jax.ShapeDtypeStruct((T, FOX_W), BF16),
        kv_cache,
        jax.ShapeDtypeStruct((T, FOX_W), BF16),
        jax.ShapeDtypeStruct((T, HGRN_W), F32),
        jax.ShapeDtypeStruct((T, HGRN_W), F32),
        jax.ShapeDtypeStruct((T, HGRN_W), F32),
        jax.ShapeDtypeStruct((T, HGRN_W), F32),
        jax.ShapeDtypeStruct((T, MEM_W), BF16),
        jax.ShapeDtypeStruct((T, LANES), F32),
    ]
    return pl.pallas_call(
        _in_proj_kernel,
        grid=(T // tm, N_SEG),
        in_specs=[pl.BlockSpec((tm, D), lambda i, j: (i, 0)), full(g_norm),
                  pl.BlockSpec((SEG_W, D), lambda i, j: (jnp.minimum(j, n_a - 1), 0)),
                  pl.BlockSpec((SEG_W, D), lambda i, j: (jnp.maximum(j - n_a, 0), 0)),
                  full(w_f), full(g_fq), full(g_fk), full(g_mq), full(lb_logits), full(b_f_pad)],
        out_specs=[kv_spec if o is kv_cache else pl.BlockSpec((tm, o.shape[1]), lambda i, j: (i, 0)) for o in outs],
        out_shape=outs,
        scratch_shapes=[pltpu.VMEM((tm, D), BF16)],
        compiler_params=_cparams(("arbitrary", "arbitrary")),
        name="in_proj",
    )(x, g_norm, w_a, w_b, w_f, g_fq, g_fk, g_mq, lb_logits, b_f_pad)


CUM_C = 256


def _fox_cum_kernel(lf_ref, ck_ref):
    S = lf_ref.shape[1]
    r = lax.broadcasted_iota(jnp.int32, (CUM_C, CUM_C), 0)
    c = lax.broadcasted_iota(jnp.int32, (CUM_C, CUM_C), 1)
    tri = jnp.where(c <= r, 1.0, 0.0).astype(BF16)
    carry = jnp.zeros((1, LANES), F32)
    for i in range(S // CUM_C):
        sl = slice(i * CUM_C, (i + 1) * CUM_C)
        hi, mid, lo = _split3(lf_ref[0, sl, :])
        cum = (_dot(tri, hi) + _dot(tri, mid)) + _dot(tri, lo) + carry
        carry = cum[CUM_C - 1:CUM_C, :]
        ck_ref[0, :, sl] = cum.T[0:HEAD_PAD, :]


def _fox_cum(flf):
    B, S, _ = flf.shape
    return pl.pallas_call(
        _fox_cum_kernel,
        grid=(B,),
        in_specs=[pl.BlockSpec((1, S, LANES), lambda b: (b, 0, 0))],
        out_specs=pl.BlockSpec((1, HEAD_PAD, S), lambda b: (b, 0, 0)),
        out_shape=jax.ShapeDtypeStruct((B, HEAD_PAD, S), F32),
        compiler_params=_cparams(("arbitrary",)),
        name="fox_cum",
    )(flf)


FOX_T = 256


def _softmax_step(s, v_bf, m, l, acc):
    m_new = jnp.maximum(m, jnp.max(s, axis=-1, keepdims=True))
    alpha = jnp.exp(m - m_new)
    p = jnp.exp(s - m_new)
    l = alpha * l + jnp.sum(p, axis=-1, keepdims=True)
    acc = alpha * acc + _dot(p.astype(BF16), v_bf)
    return m_new, l, acc


def _fox_attn_kernel(q_ref, k_ref, v_ref, ck_ref, o_ref, m_s, l_s, acc_s):
    qi = pl.program_id(1)
    T = FOX_T
    m_s[...] = jnp.full_like(m_s, -jnp.inf)
    l_s[...] = jnp.zeros_like(l_s)
    acc_s[...] = jnp.zeros_like(acc_s)
    row = lax.broadcasted_iota(jnp.int32, (T, T), 0)
    col = lax.broadcasted_iota(jnp.int32, (T, T), 1)
    causal = col <= row

    def tile(ks, masked):
        for h in range(FOX_HEADS):
            hs = slice(h * HEAD_DIM, (h + 1) * HEAD_DIM)
            s = _dot_nt(q_ref[0, :, hs], k_ref[0, pl.ds(ks, T), hs]) - ck_ref[0, h:h + 1, pl.ds(ks, T)]
            if masked:
                s = jnp.where(causal, s, -jnp.inf)
            m_old = m_s[:, hs]
            m_new = jnp.maximum(m_old, jnp.max(s, axis=-1, keepdims=True))
            alpha = jnp.exp(m_old - m_new)
            p = jnp.exp(s - jnp.concatenate([m_new] * (T // HEAD_DIM), axis=-1))
            m_s[:, hs] = m_new
            l_s[:, hs] = alpha * l_s[:, hs] + jnp.sum(p, axis=-1, keepdims=True)
            acc_s[:, hs] = alpha * acc_s[:, hs] + _dot(p.astype(BF16), v_ref[0, pl.ds(ks, T), hs])

    def body(kt, c):
        tile(pl.multiple_of(kt * T, T), False)
        return c

    lax.fori_loop(0, qi, body, 0)
    tile(pl.multiple_of(qi * T, T), True)
    o_ref[0] = (acc_s[...] / l_s[...]).astype(o_ref.dtype)


def _fox_attn(fq, fk, fv, ck):
    B, S, _ = fq.shape
    T = FOX_T
    return pl.pallas_call(
        _fox_attn_kernel,
        grid=(B, S // T),
        in_specs=[pl.BlockSpec((1, T, FOX_W), lambda b, i: (b, i, 0)),
                  pl.BlockSpec((1, S, FOX_W), lambda b, i: (b, 0, 0)),
                  pl.BlockSpec((1, S, FOX_W), lambda b, i: (b, 0, 0)),
                  pl.BlockSpec((1, HEAD_PAD, S), lambda b, i: (b, 0, 0))],
        out_specs=pl.BlockSpec((1, T, FOX_W), lambda b, i: (b, i, 0)),
        out_shape=jax.ShapeDtypeStruct((B, S, FOX_W), BF16),
        scratch_shapes=[pltpu.VMEM((T, FOX_W), F32)] * 3,
        compiler_params=_cparams(("arbitrary", "arbitrary")),
        name="fox_attn",
    )(fq, fk, fv, ck)


def _level_ref(b, m):
    C = b.shape[0]
    if 2 * m >= SUBLANES:
        b3 = b.reshape(C // (2 * m), 2 * m, HEAD_DIM)
        r = jnp.broadcast_to(b3[:, m - 1:m, :], b3.shape)
        return r.reshape(C, HEAD_DIM)
    b3 = b.reshape(C // SUBLANES, SUBLANES, HEAD_DIM)
    sub = lax.broadcasted_iota(jnp.int32, b3.shape, 1)
    pick = lambda i: jnp.broadcast_to(b3[:, i:i + 1, :], b3.shape)
    if m == 2:
        r = jnp.where(sub < 4, pick(1), pick(5))
    else:
        r = jnp.where(sub < 2, pick(0), jnp.where(sub < 4, pick(2), jnp.where(sub < 6, pick(4), pick(6))))
    return r.reshape(C, HEAD_DIM)


def _level_index(C):
    t = lax.broadcasted_iota(jnp.int32, (C, C), 0)
    s = lax.broadcasted_iota(jnp.int32, (C, C), 1)
    x = jnp.bitwise_xor(t, s)
    lvl = jnp.full((C, C), -1, jnp.int32)
    j, m = 0, 1
    while m < C:
        lvl = jnp.where(x >= m, j, lvl)
        j, m = j + 1, 2 * m
    return jnp.where(t > s, lvl, -1)


def _hgrn_chunk(q, g, v, st, tri, lvl, n_valid):
    C = q.shape[0]
    hi, mid, lo = _split3(g)
    b = (_dot(tri, hi) + _dot(tri, mid)) + _dot(tri, lo)
    k = 1.0 - jnp.exp(g)
    a = jnp.zeros((C, C), F32)
    j, m = 0, 1
    while m < C:
        e = jnp.exp(-jnp.abs(b - _level_ref(b, m)))
        a_l = _dot_nt((q * e).astype(BF16), (k * e).astype(BF16))
        a = jnp.where(lvl == j, a_l, a)
        j, m = j + 1, 2 * m
    v_bf = v.astype(BF16)
    diag = jnp.sum(q * k, axis=-1, keepdims=True)
    o = _dot_nt((q * jnp.exp(b)).astype(BF16), st.astype(BF16)) + _dot(a.astype(BF16), v_bf) + diag * v
    b_last = b[n_valid - 1:n_valid, :]
    kt = k * jnp.exp(jnp.minimum(b_last - b, 0.0))
    if n_valid < C:
        rows = lax.broadcasted_iota(jnp.int32, (C, HEAD_DIM), 0)
        kt = jnp.where(rows < n_valid, kt, 0.0)
    st_new = st * jnp.exp(b_last) + lax.dot_general(v_bf, kt.astype(BF16), TN_DIMS, preferred_element_type=F32)
    return o, st_new


def _hgrn_kernel(*refs, C, n_chunks, n_heads, n_valid, has_s0):
    if has_s0:
        q_ref, g_ref, v_ref, gs_ref, gn_ref, s0_ref, o_ref, sf_ref = refs
    else:
        q_ref, g_ref, v_ref, gs_ref, gn_ref, o_ref, sf_ref = refs
    r = lax.broadcasted_iota(jnp.int32, (C, C), 0)
    c = lax.broadcasted_iota(jnp.int32, (C, C), 1)
    tri = jnp.where(c <= r, 1.0, 0.0).astype(BF16)
    lvl = _level_index(C)
    gn = gn_ref[...]
    sts0 = tuple(s0_ref[0, h].T if has_s0 else jnp.zeros((HEAD_DIM, HEAD_DIM), F32) for h in range(n_heads))

    def body(ci, sts):
        rs = pl.ds(pl.multiple_of(ci * C, C), C)
        out = []
        for h in range(n_heads):
            hs = slice(h * HEAD_DIM, (h + 1) * HEAD_DIM)
            o, st = _hgrn_chunk(q_ref[0, rs, hs], g_ref[0, rs, hs], v_ref[0, rs, hs], sts[h], tri, lvl, n_valid)
            ms = jnp.mean(o * o, axis=-1, keepdims=True)
            o_ref[0, rs, hs] = (o * lax.rsqrt(ms + EPS) * gn * gs_ref[0, rs, hs]).astype(o_ref.dtype)
            out.append(st)
        return tuple(out)

    sts = lax.fori_loop(0, n_chunks, body, sts0) if n_chunks > 1 else body(0, sts0)
    for h in range(n_heads):
        sf_ref[0, h] = sts[h].T


def _hgrn(hq, hlf, hv, hgs, g_hn, s0, C, n_heads, n_valid):
    B, L, _ = hq.shape
    hp = HGRN_HEADS // n_heads
    w = n_heads * HEAD_DIM
    seq = pl.BlockSpec((1, L, w), lambda b, h: (b, 0, h))
    st_spec = pl.BlockSpec((1, n_heads, HEAD_DIM, HEAD_DIM), lambda b, h: (b, h, 0, 0))
    in_specs = [seq, seq, seq, seq, pl.BlockSpec((1, HEAD_DIM), lambda b, h: (0, 0))]
    args = [hq, hlf, hv, hgs, g_hn]
    if s0 is not None:
        in_specs.append(st_spec)
        args.append(s0)
    kern = functools.partial(_hgrn_kernel, C=C, n_chunks=L // C, n_heads=n_heads, n_valid=n_valid,
                             has_s0=s0 is not None)
    return pl.pallas_call(
        kern,
        grid=(B, hp),
        in_specs=in_specs,
        out_specs=[seq, st_spec],
        out_shape=[jax.ShapeDtypeStruct((B, L, HGRN_W), BF16),
                   jax.ShapeDtypeStruct((B, HGRN_HEADS, HEAD_DIM, HEAD_DIM), F32)],
        compiler_params=_cparams(("arbitrary", "arbitrary")),
        name="hgrn",
    )(*args)


def _mem_kv_kernel(x_ref, gn_ref, w_ref, gk_ref, mk_ref, mv_ref, xn_s):
    j = pl.program_id(1)

    @pl.when(j == 0)
    def _():
        x = x_ref[...]
        ms = jnp.mean(x * x, axis=-1, keepdims=True)
        xn_s[...] = (x * lax.rsqrt(ms + EPS) * gn_ref[...]).astype(BF16)

    acc = _dot(xn_s[...], w_ref[...])

    @pl.when(j == 0)
    def _():
        _store_heads(mk_ref, _rms_heads(acc, gk_ref[...], MEM_HEADS, 1.0), MEM_HEADS)

    @pl.when(j == 1)
    def _():
        _store_heads(mv_ref, acc, MEM_HEADS)


def _mem_kv(mem, g_norm, w, g_mk, tm):
    T, D = mem.shape
    out = jax.ShapeDtypeStruct((T, MEM_HEADS, HEAD_DIM), F32)
    return pl.pallas_call(
        _mem_kv_kernel,
        grid=(T // tm, 2),
        in_specs=[pl.BlockSpec((tm, D), lambda i, j: (i, 0)), pl.BlockSpec((1, D), lambda i, j: (0, 0)),
                  pl.BlockSpec((D, MEM_W), lambda i, j: (0, j)), pl.BlockSpec((1, HEAD_DIM), lambda i, j: (0, 0))],
        out_specs=[pl.BlockSpec((tm, MEM_HEADS, HEAD_DIM), lambda i, j: (i, 0, 0))] * 2,
        out_shape=[out, out],
        scratch_shapes=[pltpu.VMEM((tm, D), BF16)],
        compiler_params=_cparams(("arbitrary", "arbitrary")),
        name="mem_kv",
    )(mem, g_norm, w, g_mk)


def _mem_attn_kernel(q_ref, k_ref, v_ref, o_ref):
    for h in range(MEM_HEADS):
        hs = slice(h * HEAD_DIM, (h + 1) * HEAD_DIM)
        s = _dot_nt(q_ref[0, :, hs], k_ref[0, :, h, :].astype(BF16))
        p = jnp.exp(s - jnp.max(s, axis=-1, keepdims=True))
        l = jnp.sum(p, axis=-1, keepdims=True)
        o_ref[0, :, hs] = (_dot(p.astype(BF16), v_ref[0, :, h, :].astype(BF16)) / l).astype(o_ref.dtype)


def _mem_attn(mq, mk, mv, tq):
    B, L, _ = mq.shape
    M = mk.shape[1]
    kv = pl.BlockSpec((1, M, MEM_HEADS, HEAD_DIM), lambda b, i: (b, 0, 0, 0))
    qo = pl.BlockSpec((1, tq, MEM_W), lambda b, i: (b, i, 0))
    return pl.pallas_call(
        _mem_attn_kernel,
        grid=(B, L // tq),
        in_specs=[qo, kv, kv],
        out_specs=qo,
        out_shape=jax.ShapeDtypeStruct((B, L, MEM_W), BF16),
        compiler_params=_cparams(("arbitrary", "arbitrary")),
        name="mem_attn",
    )(mq, mk, mv)


DEC_G = 8
DEC_ROWS = 4 * HEAD_PAD


def _suffix_sum_lanes(x):
    lane = lax.broadcasted_iota(jnp.int32, x.shape, 1)
    s = 1
    while s < LANES:
        x = x + jnp.where(lane + s < LANES, pltpu.roll(x, LANES - s, 1), 0.0)
        s *= 2
    return x


def _prefix_sum_lanes(x):
    lane = lax.broadcasted_iota(jnp.int32, x.shape, 1)
    s = 1
    while s < LANES:
        x = x + jnp.where(lane >= s, pltpu.roll(x, s, 1), 0.0)
        s *= 2
    return x


def _fox_dec_kernel(pt_ref, q_ref, kn_ref, vn_ref, lfn_ref, *refs, n_tok):
    G = DEC_G
    k_refs, v_refs, lf_refs = refs[0:G], refs[G:2 * G], refs[2 * G:3 * G]
    o_ref = refs[3 * G]
    qbd, kn_s, vn_s, m_s, l_s, acc_s, car_s = refs[3 * G + 1:]
    st = pl.program_id(1)

    def attend(k_bf, v_bf, bias):
        s = _dot_nt(qbd[...].astype(BF16), k_bf) + bias
        m_new, l_new, acc_new = _softmax_step(s, v_bf, m_s[...], l_s[...], acc_s[...])
        m_s[...] = m_new
        l_s[...] = l_new
        acc_s[...] = acc_new

    @pl.when(st == 0)
    def _():
        qbd[...] = jnp.zeros_like(qbd)
        q = q_ref[0].astype(F32)
        for t in range(n_tok):
            for h in range(FOX_HEADS):
                hs = slice(h * HEAD_DIM, (h + 1) * HEAD_DIM)
                qbd[t * HEAD_PAD + h:t * HEAD_PAD + h + 1, hs] = q[t:t + 1, hs]
        kn_s[...] = jnp.zeros_like(kn_s)
        vn_s[...] = jnp.zeros_like(vn_s)
        kn_s[0:SUBLANES, :] = kn_ref[0].astype(F32)
        vn_s[0:SUBLANES, :] = vn_ref[0].astype(F32)
        m_s[...] = jnp.full_like(m_s, -jnp.inf)
        l_s[...] = jnp.zeros_like(l_s)
        acc_s[...] = jnp.zeros_like(acc_s)
        car_s[...] = jnp.zeros_like(car_s)
        ecum = _prefix_sum_lanes(lfn_ref[0])
        lane = lax.broadcasted_iota(jnp.int32, (HEAD_PAD, LANES), 1)
        bias = jnp.concatenate([jnp.where(lane <= t, -ecum, -jnp.inf) for t in range(n_tok)], axis=0)
        attend(kn_s[...].astype(BF16), vn_s[...].astype(BF16), bias)

    heads = lambda ref: jnp.concatenate([ref[0, h] for h in range(FOX_HEADS)], axis=-1).astype(BF16)
    carry = car_s[...]
    ds = []
    for i in range(G):
        lf = lf_refs[i][0]
        incl = _suffix_sum_lanes(lf)
        ds.append(carry + (incl - lf))
        carry = carry + incl[:, 0:1]
    car_s[...] = carry
    d_all = jnp.concatenate(ds, axis=-1)
    attend(jnp.concatenate([heads(r) for r in k_refs], axis=0),
           jnp.concatenate([heads(r) for r in v_refs], axis=0),
           jnp.concatenate([d_all] * n_tok, axis=0))

    @pl.when(st == pl.num_programs(1) - 1)
    def _():
        res = acc_s[...] / l_s[...]
        o_ref[...] = jnp.zeros_like(o_ref)
        for t in range(n_tok):
            for h in range(FOX_HEADS):
                hs = slice(h * HEAD_DIM, (h + 1) * HEAD_DIM)
                r = t * HEAD_PAD + h
                o_ref[0, t:t + 1, hs] = res[r:r + 1, hs].astype(o_ref.dtype)


def _fox_dec(page_table, fq, fk, fv, lf_new_t, cache_k, cache_v, cache_lf_t, n_tok):
    B, n_pages = page_table.shape
    G = DEC_G
    assert n_pages % G == 0 and n_tok * HEAD_PAD == DEC_ROWS
    n_steps = n_pages // G
    tok = lambda w: pl.BlockSpec((1, SUBLANES, w), lambda b, s, pt: (b, 0, 0))

    def page_spec(shape, i):
        return pl.BlockSpec((1,) + shape,
                            lambda b, s, pt: (pt[b * n_pages + (n_pages - 1 - (s * G + i))],) + (0,) * len(shape))

    kv_page = (FOX_HEADS, LANES, HEAD_DIM)
    in_specs = ([tok(FOX_W), tok(FOX_W), tok(FOX_W), tok(LANES)]
                + [page_spec(kv_page, i) for i in range(G)]
                + [page_spec(kv_page, i) for i in range(G)]
                + [page_spec((HEAD_PAD, LANES), i) for i in range(G)])
    grid_spec = pltpu.PrefetchScalarGridSpec(
        num_scalar_prefetch=1,
        grid=(B, n_steps),
        in_specs=in_specs,
        out_specs=pl.BlockSpec((1, SUBLANES, FOX_W), lambda b, s, pt: (b, 0, 0)),
        scratch_shapes=[pltpu.VMEM((DEC_ROWS, FOX_W), F32),
                        pltpu.VMEM((LANES, FOX_W), F32), pltpu.VMEM((LANES, FOX_W), F32),
                        pltpu.VMEM((DEC_ROWS, 1), F32), pltpu.VMEM((DEC_ROWS, 1), F32),
                        pltpu.VMEM((DEC_ROWS, FOX_W), F32), pltpu.VMEM((HEAD_PAD, 1), F32)],
    )
    return pl.pallas_call(
        functools.partial(_fox_dec_kernel, n_tok=n_tok),
        grid_spec=grid_spec,
        out_shape=jax.ShapeDtypeStruct((B, SUBLANES, FOX_W), BF16),
        compiler_params=_cparams(("arbitrary", "arbitrary")),
        name="fox_dec",
    )(page_table.reshape(-1), fq, fk, fv, lf_new_t, *([cache_k] * G), *([cache_v] * G), *([cache_lf_t] * G))


def _merge_kernel(fo_p, ho_p, mo_p, x_p, fo_s, ho_s, mo_s, x_s, w_ref, gf_ref, wrh_ref, wrl_ref, br_ref,
                  h_ref, xn_ref, eid_ref, gate_ref, cat_s, x_sc, *, n_p):
    i = pl.program_id(0)
    D = x_sc.shape[1]

    def stage(fo_ref, ho_ref, mo_ref, x_ref):
        cat_s[:, 0:FOX_W] = fo_ref[...]
        cat_s[:, FOX_W:FOX_W + HGRN_W] = ho_ref[...]
        cat_s[:, FOX_W + HGRN_W:] = mo_ref[...]
        x_sc[...] = x_ref[...]

    @pl.when(i < n_p)
    def _():
        stage(fo_p, ho_p, mo_p, x_p)

    @pl.when(i >= n_p)
    def _():
        stage(fo_s, ho_s, mo_s, x_s)

    ssq = jnp.zeros((x_sc.shape[0], 1), F32)
    for c in range(D // PROJ_CHUNK):
        cs = slice(c * PROJ_CHUNK, (c + 1) * PROJ_CHUNK)
        hc = x_sc[:, cs] + _dot(cat_s[...], w_ref[:, cs])
        h_ref[:, cs] = hc
        ssq = ssq + jnp.sum(hc * hc, axis=-1, keepdims=True)
    xn = h_ref[...] * lax.rsqrt(ssq * (1.0 / D) + EPS) * gf_ref[...]
    xn_ref[...] = xn
    x_hi = xn.astype(BF16)
    x_lo = (xn - x_hi.astype(F32)).astype(BF16)
    logits = (_dot(x_hi, wrh_ref[...]) + (_dot(x_hi, wrl_ref[...]) + _dot(x_lo, wrh_ref[...]))) + br_ref[...]
    lane = lax.broadcasted_iota(jnp.int32, logits.shape, 1)
    big = jnp.int32(LANES)
    ninf = -jnp.inf
    gl = jnp.where(lane < N_GROUPS, logits, ninf)
    gmax = jnp.max(gl, axis=-1, keepdims=True)
    g_sel = jnp.min(jnp.where(gl == gmax, lane, big), axis=-1, keepdims=True)
    g_prob = 1.0 / jnp.sum(jnp.exp(gl - gmax), axis=-1, keepdims=True)
    lo = N_GROUPS + EXPERTS_PER_GROUP * g_sel
    el = jnp.where((lane >= lo) & (lane < lo + EXPERTS_PER_GROUP), logits, ninf)
    v1 = jnp.max(el, axis=-1, keepdims=True)
    i1 = jnp.min(jnp.where(el == v1, lane, big), axis=-1, keepdims=True)
    el2 = jnp.where(lane == i1, ninf, el)
    v2 = jnp.max(el2, axis=-1, keepdims=True)
    i2 = jnp.min(jnp.where(el2 == v2, lane, big), axis=-1, keepdims=True)
    t = jnp.exp(v2 - v1)
    w1 = g_prob / (1.0 + t)
    w2 = g_prob * t / (1.0 + t)
    eid_ref[...] = jnp.where(lane == 0, i1 - N_GROUPS, jnp.where(lane == 1, i2 - N_GROUPS, 0))
    gate_ref[...] = jnp.where(lane == 0, w1, jnp.where(lane == 1, w2, 0.0))


def _merge(prompt, sample, w_out, g_ffn, wr_hi, wr_lo, b_r, tm):
    Tp, D = prompt[3].shape
    Ts = sample[3].shape[0]
    n_p, n_s = Tp // tm, Ts // tm
    T = Tp + Ts
    p_row = lambda w: pl.BlockSpec((tm, w), lambda i: (jnp.minimum(i, n_p - 1), 0))
    s_row = lambda w: pl.BlockSpec((tm, w), lambda i: (jnp.maximum(i - n_p, 0), 0))
    row = lambda w: pl.BlockSpec((tm, w), lambda i: (i, 0))
    full = lambda a: pl.BlockSpec(a.shape, lambda i: (0,) * a.ndim)
    widths = (FOX_W, HGRN_W, MEM_W, D)
    return pl.pallas_call(
        functools.partial(_merge_kernel, n_p=n_p),
        grid=(n_p + n_s,),
        in_specs=[p_row(w) for w in widths] + [s_row(w) for w in widths]
                 + [full(w_out), full(g_ffn), full(wr_hi), full(wr_lo), full(b_r)],
        out_specs=[row(D), row(D), row(LANES), row(LANES)],
        out_shape=[jax.ShapeDtypeStruct((T, D), F32), jax.ShapeDtypeStruct((T, D), F32),
                   jax.ShapeDtypeStruct((T, LANES), jnp.int32), jax.ShapeDtypeStruct((T, LANES), F32)],
        scratch_shapes=[pltpu.VMEM((tm, FOX_W + HGRN_W + MEM_W), BF16), pltpu.VMEM((tm, D), F32)],
        compiler_params=_cparams(("arbitrary",)),
        name="merge",
    )(*prompt, *sample, w_out, g_ffn, wr_hi, wr_lo, b_r)


def _experts_kernel(be_ref, nb_ref, tok_ref, x_hbm, wg_ref, wu_ref, wd_ref, y_ref, xbuf, wg_s, wu_s, wd_s, sem):
    b = pl.program_id(0)

    @pl.when((b == 0) | (be_ref[b] != be_ref[jnp.maximum(b - 1, 0)]))
    def _():
        wg_s[...] = wg_ref[0].astype(BF16)
        wu_s[...] = wu_ref[0].astype(BF16)
        wd_s[...] = wd_ref[0].astype(BF16)

    n_used = nb_ref[0]
    BM = MOE_BM

    def gather(blk, slot):
        base = blk * BM
        for r in range(BM):
            pltpu.make_async_copy(x_hbm.at[pl.ds(tok_ref[base + r], 1)], xbuf.at[slot, pl.ds(r, 1)],
                                  sem.at[slot]).start()

    def wait(slot):
        pltpu.make_async_copy(x_hbm.at[pl.ds(0, BM)], xbuf.at[slot], sem.at[slot]).wait()

    slot = lax.rem(b, 2)

    @pl.when(b == 0)
    def _():
        gather(0, 0)

    @pl.when(b + 1 < n_used)
    def _():
        gather(b + 1, 1 - slot)

    @pl.when(b < n_used)
    def _():
        wait(slot)
        x = xbuf[slot].astype(BF16)
        hmid = _dot(x, wg_s[...])
        hmid = hmid * _sigmoid(hmid) * _dot(x, wu_s[...])
        y_ref[...] = _dot(hmid.astype(BF16), wd_s[...])

    @pl.when(b >= n_used)
    def _():
        y_ref[...] = jnp.zeros_like(y_ref)


def _experts(block_e, n_used, slot_tok, xn, w_gate, w_up, w_down):
    n_blocks = block_e.shape[0]
    T, D = xn.shape
    FF = w_gate.shape[2]
    BM = MOE_BM
    grid_spec = pltpu.PrefetchScalarGridSpec(
        num_scalar_prefetch=3,
        grid=(n_blocks,),
        in_specs=[pl.BlockSpec(memory_space=pl.ANY),
                  pl.BlockSpec((1, D, FF), lambda b, be, nb, tk: (be[b], 0, 0)),
                  pl.BlockSpec((1, D, FF), lambda b, be, nb, tk: (be[b], 0, 0)),
                  pl.BlockSpec((1, FF, D), lambda b, be, nb, tk: (be[b], 0, 0))],
        out_specs=pl.BlockSpec((BM, D), lambda b, be, nb, tk: (b, 0)),
        scratch_shapes=[pltpu.VMEM((2, BM, D), F32), pltpu.VMEM((D, FF), BF16), pltpu.VMEM((D, FF), BF16),
                        pltpu.VMEM((FF, D), BF16), pltpu.SemaphoreType.DMA((2,))],
    )
    return pl.pallas_call(
        _experts_kernel,
        grid_spec=grid_spec,
        out_shape=jax.ShapeDtypeStruct((n_blocks * BM, D), F32),
        compiler_params=_cparams(("arbitrary",)),
        name="experts",
    )(block_e, n_used, slot_tok, xn, w_gate, w_up, w_down)


COMB_TM = 128


def _combine_kernel(pos_ref, h_ref, gate_ref, y_hbm, op_ref, os_ref, ybuf, sem, *, n_p):
    i = pl.program_id(0)
    n = pl.num_programs(0)
    TM = COMB_TM

    def gather(blk, slot):
        base = blk * (2 * TM)
        for r in range(2 * TM):
            pltpu.make_async_copy(y_hbm.at[pl.ds(pos_ref[base + r], 1)], ybuf.at[slot, pl.ds(r, 1)],
                                  sem.at[slot]).start()

    def wait(slot):
        pltpu.make_async_copy(y_hbm.at[pl.ds(0, 2 * TM)], ybuf.at[slot], sem.at[slot]).wait()

    slot = lax.rem(i, 2)

    @pl.when(i == 0)
    def _():
        gather(0, 0)

    @pl.when(i + 1 < n)
    def _():
        gather(i + 1, 1 - slot)

    wait(slot)
    g = gate_ref[...]
    res = h_ref[...] + (g[:, 0:1] * ybuf[slot, 0:TM, :] + g[:, 1:2] * ybuf[slot, TM:2 * TM, :])

    @pl.when(i < n_p)
    def _():
        op_ref[...] = res

    @pl.when(i >= n_p)
    def _():
        os_ref[...] = res


def _combine(pos, h, gate, y_slots, t_prompt):
    T, D = h.shape
    TM = COMB_TM
    n_p = t_prompt // TM
    grid_spec = pltpu.PrefetchScalarGridSpec(
        num_scalar_prefetch=1,
        grid=(T // TM,),
        in_specs=[pl.BlockSpec((TM, D), lambda i, p: (i, 0)), pl.BlockSpec((TM, LANES), lambda i, p: (i, 0)),
                  pl.BlockSpec(memory_space=pl.ANY)],
        out_specs=[pl.BlockSpec((TM, D), lambda i, p: (jnp.minimum(i, n_p - 1), 0)),
                   pl.BlockSpec((TM, D), lambda i, p: (jnp.maximum(i - n_p, 0), 0))],
        scratch_shapes=[pltpu.VMEM((2, 2 * TM, D), F32), pltpu.SemaphoreType.DMA((2,))],
    )
    return pl.pallas_call(
        functools.partial(_combine_kernel, n_p=n_p),
        grid_spec=grid_spec,
        out_shape=[jax.ShapeDtypeStruct((t_prompt, D), F32), jax.ShapeDtypeStruct((T - t_prompt, D), F32)],
        compiler_params=_cparams(("arbitrary",)),
        name="combine",
    )(pos, h, gate, y_slots)


def _moe(h, xn, eid, gate, w_gate, w_up, w_down, t_prompt):
    T = h.shape[0]
    A = T * TOP_K
    BM = MOE_BM
    n_blocks = -(-(A + N_EXPERTS * (BM - 1)) // BM)
    e_flat = eid[:, :TOP_K].reshape(A)
    onehot = (e_flat[:, None] == jnp.arange(N_EXPERTS, dtype=jnp.int32)[None, :]).astype(jnp.int32)
    csum = jnp.cumsum(onehot, axis=0)
    counts = csum[-1]
    rank = jnp.take_along_axis(csum, e_flat[:, None], axis=1)[:, 0] - 1
    padded = (counts + BM - 1) // BM * BM
    pad_end = jnp.cumsum(padded)
    pad_start = pad_end - padded
    pos = (pad_start[e_flat] + rank).astype(jnp.int32)
    tok = jnp.arange(A, dtype=jnp.int32) // TOP_K
    filler = jnp.arange(n_blocks * BM, dtype=jnp.int32) % T
    slot_tok = filler.at[pos].set(tok)
    block_first = jnp.arange(n_blocks, dtype=jnp.int32) * BM
    block_e = jnp.minimum(jnp.sum(pad_end[None, :] <= block_first[:, None], axis=1), N_EXPERTS - 1).astype(jnp.int32)
    n_used = (pad_end[-1] // BM).astype(jnp.int32).reshape(1)
    y_slots = _experts(block_e, n_used, slot_tok, xn, w_gate, w_up, w_down)
    pos_tiles = pos.reshape(T // COMB_TM, COMB_TM, TOP_K).transpose(0, 2, 1).reshape(-1)
    return _combine(pos_tiles, h, gate, y_slots, t_prompt)


def _prep_weights(w_in, b_fox_f, w_router_group, b_router_group, w_router_expert, b_router_expert):
    D = w_in.shape[0]
    c = [0]
    for s in (FOX_W, FOX_W, FOX_W, FOX_HEADS, HGRN_W, HGRN_W, HGRN_W, HGRN_W, MEM_W):
        c.append(c[-1] + s)
    w_t = w_in.T
    w_a = w_t[c[0]:c[3]].astype(BF16)
    w_b = jnp.pad(w_t[c[4]:c[9]], ((0, SEG_W - MEM_W), (0, 0))).astype(BF16)
    w_f = jnp.pad(w_t[c[3]:c[4]], ((0, LANES - FOX_HEADS), (0, 0))).astype(BF16)
    b_f_pad = jnp.zeros((1, LANES), F32).at[0, :FOX_HEADS].set(b_fox_f)
    n_r = N_GROUPS + N_EXPERTS
    w_r = jnp.zeros((D, LANES), F32).at[:, :N_GROUPS].set(w_router_group).at[:, N_GROUPS:n_r].set(w_router_expert)
    b_r = jnp.zeros((1, LANES), F32).at[0, :N_GROUPS].set(b_router_group).at[0, N_GROUPS:n_r].set(b_router_expert)
    wr_hi = w_r.astype(BF16)
    wr_lo = (w_r - wr_hi.astype(F32)).astype(BF16)
    return (w_a, w_b, w_f), b_f_pad, wr_hi, wr_lo, b_r


def kernel(x_prompt, x_sample, cache_fox_k, cache_fox_v, cache_fox_logf, cache_mem_k, cache_mem_v, state_hgrn, page_table, mem_prompt, g_attn_norm, w_in, b_fox_f, g_fox_q, g_fox_k, lb_logits, g_hgrn_out, g_mem_norm, w_mem_kv, g_mem_q, g_mem_k, w_out, g_ffn_norm, w_router_group, b_router_group, w_router_expert, b_router_expert, w_gate_e, w_up_e, w_down_e):
    assert w_in.shape[0] == 1, "single-layer step"
    Bp, S, D = x_prompt.shape
    Bd, L, _ = x_sample.shape
    n_pool, page = cache_fox_k.shape[1], cache_fox_k.shape[2]
    assert page == LANES and L <= SUBLANES
    M = mem_prompt.shape[1]
    l = 0
    row = lambda a: a[l].reshape(1, -1)
    head_major = lambda a: jnp.swapaxes(a, -3, -2)

    w_seg, b_f_pad, wr_hi, wr_lo, b_r = _prep_weights(
        w_in[l], b_fox_f[l], w_router_group[l], b_router_group[l], w_router_expert[l], b_router_expert[l])
    w_out_bf = w_out[l].astype(BF16)
    w_mkv_bf = w_mem_kv[l].astype(BF16)
    experts_w = (w_gate_e[l], w_up_e[l], w_down_e[l])
    proj_args = (row(g_attn_norm), *w_seg, row(g_fox_q), row(g_fox_k), row(g_mem_q), lb_logits, b_f_pad)

    Tp = Bp * S
    xp = x_prompt.reshape(Tp, D)
    fq, fk_p, fkb, fv_p, fvb, hq, hlf, hv, hgs, mq, flf_p = _in_proj(xp, *proj_args, tm=512, seq=S)
    seq = lambda a: a.reshape(Bp, S, a.shape[-1])
    ck = _fox_cum(seq(flf_p))
    fox_o = _fox_attn(seq(fq), seq(fkb), seq(fvb), ck)
    hg_o, s_p = _hgrn(seq(hq), seq(hlf), seq(hv), seq(hgs), row(g_hgrn_out), None, C=128, n_heads=3, n_valid=128)
    mk, mv = _mem_kv(mem_prompt.reshape(Bp * M, D), row(g_mem_norm), w_mkv_bf, row(g_mem_k), tm=256)
    mem4 = lambda a, b: a.reshape(b, M, MEM_HEADS, HEAD_DIM)
    mem_o = _mem_attn(seq(mq), mem4(mk, Bp), mem4(mv, Bp), tq=512)
    prompt = (fox_o.reshape(Tp, FOX_W), hg_o.reshape(Tp, HGRN_W), mem_o.reshape(Tp, MEM_W), xp)

    R = SUBLANES
    Ts = Bd * R
    xs = jnp.pad(x_sample, ((0, 0), (0, R - L), (0, 0))).reshape(Ts, D)
    fq, fk_s, fkb, fv_s, fvb, hq, hlf, hv, hgs, mq, flf_s = _in_proj(xs, *proj_args, tm=Ts, seq=R)
    seqs = lambda a: a.reshape(Bd, R, a.shape[-1])
    lf_new_t = jnp.swapaxes(seqs(flf_s)[:, :, :HEAD_PAD], 1, 2)
    lf_new_t = jnp.pad(lf_new_t, ((0, 0), (0, 0), (0, LANES - R)))
    cache_lf_t = jnp.pad(jnp.swapaxes(cache_fox_logf[l].astype(F32), 1, 2), ((0, 0), (0, HEAD_PAD - FOX_HEADS), (0, 0)))
    fox_o = _fox_dec(page_table, seqs(fq), seqs(fkb), seqs(fvb), lf_new_t, head_major(cache_fox_k[l]), head_major(cache_fox_v[l]),
                     cache_lf_t, n_tok=L)
    hg_o, s_s = _hgrn(seqs(hq), seqs(hlf), seqs(hv), seqs(hgs), row(g_hgrn_out), state_hgrn[l],
                      C=R, n_heads=HGRN_HEADS, n_valid=L)
    mem_o = _mem_attn(seqs(mq), cache_mem_k[l], cache_mem_v[l], tq=R)
    sample = (fox_o.reshape(Ts, FOX_W), hg_o.reshape(Ts, HGRN_W), mem_o.reshape(Ts, MEM_W), xs)

    h, xn, eid, gate = _merge(prompt, sample, w_out_bf, row(g_ffn_norm), wr_hi, wr_lo, b_r, tm=Ts)
    y_p, y_s = _moe(h, xn, eid, gate, *experts_w, t_prompt=Tp)

    cut = lambda a: a.reshape((Bd, R) + a.shape[1:])[:, :L]
    return (y_p.reshape(Bp, S, D), cut(y_s),
            head_major(fk_p)[None], head_major(fv_p)[None],
            flf_p[:, :FOX_HEADS].reshape(1, Bp, S, FOX_HEADS), s_p[None],
            mem4(mk, Bp)[None], mem4(mv, Bp)[None],
            head_major(fk_s)[None, :, :L], head_major(fv_s)[None, :, :L], cut(flf_s)[:, :, :FOX_HEADS][None], s_s[None])
```

```python
import functools

import jax
import jax.numpy as jnp
from jax import lax
from jax.experimental import pallas as pl
from jax.experimental.pallas import tpu as pltpu

F32 = jnp.float32
BF16 = jnp.bfloat16

HEAD_DIM = 128
FOX_HEADS = 6
HGRN_HEADS = 6
MEM_HEADS = 4
FOX_W = FOX_HEADS * HEAD_DIM
HGRN_W = HGRN_HEADS * HEAD_DIM
MEM_W = MEM_HEADS * HEAD_DIM
N_GROUPS = 4
EXPERTS_PER_GROUP = 8
N_EXPERTS = N_GROUPS * EXPERTS_PER_GROUP
TOP_K = 2
EPS = 1e-6
ATTN_SCALE = HEAD_DIM ** -0.5
LANES = 128
SUBLANES = 8
SEG_W = FOX_W
PROJ_CHUNK = 256
HEAD_PAD = 8
MOE_BM = 256
VMEM_LIMIT = 52 * 1024 * 1024

NT_DIMS = (((1,), (1,)), ((), ()))
TN_DIMS = (((0,), (0,)), ((), ()))


def _cparams(sem, vmem=VMEM_LIMIT):
    return pltpu.CompilerParams(dimension_semantics=sem, vmem_limit_bytes=vmem)


def _split3(x):
    hi = x.astype(BF16)
    r1 = x - hi.astype(F32)
    mid = r1.astype(BF16)
    lo = (r1 - mid.astype(F32)).astype(BF16)
    return hi, mid, lo


def _dot(a, b):
    return jnp.dot(a, b, preferred_element_type=F32)


def _dot_nt(a, b):
    return lax.dot_general(a, b, NT_DIMS, preferred_element_type=F32)


def _sigmoid(x):
    return 1.0 / (1.0 + jnp.exp(-x))


def _log_sigmoid(x):
    return jnp.minimum(x, 0.0) - jnp.log(1.0 + jnp.exp(-jnp.abs(x)))


def _rms_heads(a, g, n_heads, scale):
    outs = []
    for h in range(n_heads):
        ah = a[:, h * HEAD_DIM:(h + 1) * HEAD_DIM]
        ms = jnp.mean(ah * ah, axis=-1, keepdims=True)
        outs.append(ah * lax.rsqrt(ms + EPS) * (g * scale))
    return jnp.concatenate(outs, axis=-1)


def _store_heads(ref, a, n_heads):
    for h in range(n_heads):
        ref[:, h, :] = a[:, h * HEAD_DIM:(h + 1) * HEAD_DIM]


def _store_head_major(ref, a, h0):
    nb, _, rows, _ = ref.shape
    for h in range(a.shape[1] // HEAD_DIM):
        ref[:, h0 + h] = a[:, h * HEAD_DIM:(h + 1) * HEAD_DIM].reshape(nb, rows, HEAD_DIM)


def _static_when(cond):
    def deco(f):
        if cond:
            f()
        return f
    return deco


def _in_proj_kernel(*refs, names, segs):
    r = dict(zip(names, refs))
    xn_s = r["xn_s"]
    x = r["x"][...]
    ms = jnp.mean(x * x, axis=-1, keepdims=True)
    xn_s[...] = (x * lax.rsqrt(ms + EPS) * r["g_norm"][...]).astype(BF16)

    def segment(post, w_ref, row0, width=SEG_W):
        for c in range(width // PROJ_CHUNK):
            cs = slice(c * PROJ_CHUNK, (c + 1) * PROJ_CHUNK)
            post(c, cs, _dot_nt(xn_s[...], w_ref[row0 + c * PROJ_CHUNK:row0 + (c + 1) * PROJ_CHUNK, :]))

    heads_per_chunk = PROJ_CHUNK // HEAD_DIM

    @_static_when("fq" in segs)
    def _():
        def post(c, cs, acc):
            r["fq"][:, cs] = _rms_heads(acc, r["g_fq"][...], heads_per_chunk, ATTN_SCALE).astype(BF16)
        segment(post, r["w_a"], 0)

    @_static_when("fk" in segs)
    def _():
        def post(c, cs, acc):
            fk = _rms_heads(acc, r["g_fk"][...], heads_per_chunk, 1.0)
            _store_head_major(r["fk"], fk, c * heads_per_chunk)
            r["fkb"][:, cs] = fk.astype(BF16)
        segment(post, r["w_a"], SEG_W)

    @_static_when("fv" in segs)
    def _():
        def post(c, cs, acc):
            _store_head_major(r["fv"], acc, c * heads_per_chunk)
            r["fvb"][:, cs] = acc.astype(BF16)
        segment(post, r["w_a"], 2 * SEG_W)

    @_static_when("mq" in segs)
    def _():
        def post(c, cs, acc):
            r["mq"][:, cs] = _rms_heads(acc, r["g_mq"][...], heads_per_chunk, ATTN_SCALE).astype(BF16)
        segment(post, r["w_q"], 0, MEM_W)

    @_static_when("flf" in segs)
    def _():
        r["flf"][...] = _log_sigmoid(_dot_nt(xn_s[...], r["w_f"][...]) + r["b_f"][...])

    @_static_when("hq" in segs)
    def _():
        def post(c, cs, acc):
            r["hq"][:, cs] = acc * _sigmoid(acc) * ATTN_SCALE
        segment(post, r["w_b"], 0)

    @_static_when("hlf" in segs)
    def _():
        l = r["lb_logits"][...]
        mx = jnp.max(l, axis=0, keepdims=True)
        ex = jnp.exp(l - mx)
        lb = ex[0:1, :] / jnp.sum(ex, axis=0, keepdims=True)

        def post(c, cs, acc):
            r["hlf"][:, cs] = jnp.log(lb[:, cs] + (1.0 - lb[:, cs]) * _sigmoid(acc))
        segment(post, r["w_b"], SEG_W)

    @_static_when("hv" in segs)
    def _():
        def post(c, cs, acc):
            r["hv"][:, cs] = acc
        segment(post, r["w_b"], 2 * SEG_W)

    @_static_when("hgs" in segs)
    def _():
        def post(c, cs, acc):
            r["hgs"][:, cs] = acc * _sigmoid(acc)
        segment(post, r["w_b"], 3 * SEG_W)


def _in_proj_call(x, ins, outs, tm, seq, name):
    T, D = x.shape
    nb, rows = max(1, tm // seq), min(tm, seq)
    assert nb * rows == tm and seq % rows == 0
    tiles_per_seq = seq // rows
    resident = lambda a: pl.BlockSpec(a.shape, lambda i: (0,) * a.ndim, pipeline_mode=pl.Buffered(1))

    def out_spec(o):
        if len(o.shape) == 4:
            return pl.BlockSpec((nb, FOX_HEADS, rows, HEAD_DIM), lambda i: (i // tiles_per_seq, 0, i % tiles_per_seq, 0))
        return pl.BlockSpec((tm, o.shape[1]), lambda i: (i, 0))

    names = ("x",) + tuple(ins) + tuple(outs) + ("xn_s",)
    res = pl.pallas_call(
        functools.partial(_in_proj_kernel, names=names, segs=tuple(outs)),
        grid=(T // tm,),
        in_specs=[pl.BlockSpec((tm, D), lambda i: (i, 0))] + [resident(a) for a in ins.values()],
        out_specs=[out_spec(o) for o in outs.values()],
        out_shape=list(outs.values()),
        scratch_shapes=[pltpu.VMEM((tm, D), BF16)],
        compiler_params=_cparams(("arbitrary",)),
        name=name,
    )(x, *ins.values())
    return dict(zip(outs, res))


def _in_proj(x, g_norm, w_a, w_q, w_f, w_b, g_fq, g_fk, g_mq, lb_logits, b_f_pad, tm, seq):
    T = x.shape[0]
    kv_cache = jax.ShapeDtypeStruct((T // seq, FOX_HEADS, seq, HEAD_DIM), F32)
    rows = lambda w, dt: jax.ShapeDtypeStruct((T, w), dt)
    fox = _in_proj_call(
        x, dict(g_norm=g_norm, w_a=w_a, w_q=w_q, w_f=w_f, g_fq=g_fq, g_fk=g_fk, g_mq=g_mq, b_f=b_f_pad),
        dict(fq=rows(FOX_W, BF16),
             fk=kv_cache,
             fkb=rows(FOX_W, BF16),
             fv=kv_cache, fvb=rows(FOX_W, BF16),
             mq=rows(MEM_W, BF16),
             flf=rows(LANES, F32)),
        tm, seq, "in_proj_fox")
    hg = _in_proj_call(
        x, dict(g_norm=g_norm, w_b=w_b, lb_logits=lb_logits),
        dict(hq=rows(HGRN_W, F32),
             hlf=rows(HGRN_W, F32),
             hv=rows(HGRN_W, F32),
             hgs=rows(HGRN_W, F32)),
        tm, seq, "in_proj_hgrn")
    return (fox["fq"], fox["fk"], fox["fkb"], fox["fv"], fox["fvb"], hg["hq"], hg["hlf"], hg["hv"], hg["hgs"],
            fox["mq"], fox["flf"])


CUM_C = 256


def _fox_cum_kernel(lf_ref, ck_ref):
    S = lf_ref.shape[1]
    r = lax.broadcasted_iota(jnp.int32, (CUM_C, CUM_C), 0)
    c = lax.broadcasted_iota(jnp.int32, (CUM_C, CUM_C), 1)
    tri = jnp.where(c <= r, 1.0, 0.0).astype(BF16)
    carry = jnp.zeros((1, LANES), F32)
    for i in range(S // CUM_C):
        sl = slice(i * CUM_C, (i + 1) * CUM_C)
        hi, mid, lo = _split3(lf_ref[0, sl, :])
        cum = (_dot(tri, hi) + _dot(tri, mid)) + _dot(tri, lo) + carry
        carry = cum[CUM_C - 1:CUM_C, :]
        ck_ref[0, :, sl] = cum.T[0:HEAD_PAD, :]


def _fox_cum(flf):
    B, S, _ = flf.shape
    return pl.pallas_call(
        _fox_cum_kernel,
        grid=(B,),
        in_specs=[pl.BlockSpec((1, S, LANES), lambda b: (b, 0, 0))],
        out_specs=pl.BlockSpec((1, HEAD_PAD, S), lambda b: (b, 0, 0)),
        out_shape=jax.ShapeDtypeStruct((B, HEAD_PAD, S), F32),
        compiler_params=_cparams(("arbitrary",)),
        name="fox_cum",
    )(flf)


FOX_T = 256


def _softmax_step(s, v_bf, m, l, acc):
    m_new = jnp.maximum(m, jnp.max(s, axis=-1, keepdims=True))
    alpha = jnp.exp(m - m_new)
    p = jnp.exp(s - m_new)
    l = alpha * l + jnp.sum(p, axis=-1, keepdims=True)
    acc = alpha * acc + _dot(p.astype(BF16), v_bf)
    return m_new, l, acc


def _fox_attn_kernel(q_ref, k_ref, v_ref, ck_ref, o_ref, m_s, l_s, acc_s):
    qi = pl.program_id(1)
    T = FOX_T
    m_s[...] = jnp.full_like(m_s, -jnp.inf)
    l_s[...] = jnp.zeros_like(l_s)
    acc_s[...] = jnp.zeros_like(acc_s)
    row = lax.broadcasted_iota(jnp.int32, (T, T), 0)
    col = lax.broadcasted_iota(jnp.int32, (T, T), 1)
    causal = col <= row

    def tile(ks, masked):
        for h in range(FOX_HEADS):
            hs = slice(h * HEAD_DIM, (h + 1) * HEAD_DIM)
            s = _dot_nt(q_ref[0, :, hs], k_ref[0, pl.ds(ks, T), hs]) - ck_ref[0, h:h + 1, pl.ds(ks, T)]
            if masked:
                s = jnp.where(causal, s, -jnp.inf)
            m_old = m_s[:, hs]
            m_new = jnp.maximum(m_old, jnp.max(s, axis=-1, keepdims=True))
            alpha = jnp.exp(m_old - m_new)
            p = jnp.exp(s - jnp.concatenate([m_new] * (T // HEAD_DIM), axis=-1))
            m_s[:, hs] = m_new
            l_s[:, hs] = alpha * l_s[:, hs] + jnp.sum(p, axis=-1, keepdims=True)
            acc_s[:, hs] = alpha * acc_s[:, hs] + _dot(p.astype(BF16), v_ref[0, pl.ds(ks, T), hs])

    def body(kt, c):
        tile(pl.multiple_of(kt * T, T), False)
        return c

    lax.fori_loop(0, qi, body, 0)
    tile(pl.multiple_of(qi * T, T), True)
    o_ref[0] = (acc_s[...] / l_s[...]).astype(o_ref.dtype)


def _fox_attn(fq, fk, fv, ck):
    B, S, _ = fq.shape
    T = FOX_T
    return pl.pallas_call(
        _fox_attn_kernel,
        grid=(B, S // T),
        in_specs=[pl.BlockSpec((1, T, FOX_W), lambda b, i: (b, i, 0)),
                  pl.BlockSpec((1, S, FOX_W), lambda b, i: (b, 0, 0)),
                  pl.BlockSpec((1, S, FOX_W), lambda b, i: (b, 0, 0)),
                  pl.BlockSpec((1, HEAD_PAD, S), lambda b, i: (b, 0, 0))],
        out_specs=pl.BlockSpec((1, T, FOX_W), lambda b, i: (b, i, 0)),
        out_shape=jax.ShapeDtypeStruct((B, S, FOX_W), BF16),
        scratch_shapes=[pltpu.VMEM((T, FOX_W), F32)] * 3,
        compiler_params=_cparams(("arbitrary", "arbitrary")),
        name="fox_attn",
    )(fq, fk, fv, ck)


def _level_ref(b, m):
    C = b.shape[0]
    if 2 * m >= SUBLANES:
        b3 = b.reshape(C // (2 * m), 2 * m, HEAD_DIM)
        r = jnp.broadcast_to(b3[:, m - 1:m, :], b3.shape)
        return r.reshape(C, HEAD_DIM)
    b3 = b.reshape(C // SUBLANES, SUBLANES, HEAD_DIM)
    sub = lax.broadcasted_iota(jnp.int32, b3.shape, 1)
    pick = lambda i: jnp.broadcast_to(b3[:, i:i + 1, :], b3.shape)
    if m == 2:
        r = jnp.where(sub < 4, pick(1), pick(5))
    else:
        r = jnp.where(sub < 2, pick(0), jnp.where(sub < 4, pick(2), jnp.where(sub < 6, pick(4), pick(6))))
    return r.reshape(C, HEAD_DIM)


def _level_index(C):
    t = lax.broadcasted_iota(jnp.int32, (C, C), 0)
    s = lax.broadcasted_iota(jnp.int32, (C, C), 1)
    x = jnp.bitwise_xor(t, s)
    lvl = jnp.full((C, C), -1, jnp.int32)
    j, m = 0, 1
    while m < C:
        lvl = jnp.where(x >= m, j, lvl)
        j, m = j + 1, 2 * m
    return jnp.where(t > s, lvl, -1)


def _hgrn_chunk(q, g, v, st, tri, lvl, n_valid):
    C = q.shape[0]
    hi, mid, lo = _split3(g)
    b = (_dot(tri, hi) + _dot(tri, mid)) + _dot(tri, lo)
    k = 1.0 - jnp.exp(g)
    a = jnp.zeros((C, C), F32)
    j, m = 0, 1
    while m < C:
        e = jnp.exp(-jnp.abs(b - _level_ref(b, m)))
        a_l = _dot_nt((q * e).astype(BF16), (k * e).astype(BF16))
        a = jnp.where(lvl == j, a_l, a)
        j, m = j + 1, 2 * m
    v_bf = v.astype(BF16)
    diag = jnp.sum(q * k, axis=-1, keepdims=True)
    o = _dot_nt((q * jnp.exp(b)).astype(BF16), st.astype(BF16)) + _dot(a.astype(BF16), v_bf) + diag * v
    b_last = b[n_valid - 1:n_valid, :]
    kt = k * jnp.exp(jnp.minimum(b_last - b, 0.0))
    if n_valid < C:
        rows = lax.broadcasted_iota(jnp.int32, (C, HEAD_DIM), 0)
        kt = jnp.where(rows < n_valid, kt, 0.0)
    st_new = st * jnp.exp(b_last) + lax.dot_general(v_bf, kt.astype(BF16), TN_DIMS, preferred_element_type=F32)
    return o, st_new


def _hgrn_kernel(*refs, C, n_chunks, n_heads, n_valid, has_s0):
    if has_s0:
        q_ref, g_ref, v_ref, gs_ref, gn_ref, s0_ref, o_ref, sf_ref = refs
    else:
        q_ref, g_ref, v_ref, gs_ref, gn_ref, o_ref, sf_ref = refs
    r = lax.broadcasted_iota(jnp.int32, (C, C), 0)
    c = lax.broadcasted_iota(jnp.int32, (C, C), 1)
    tri = jnp.where(c <= r, 1.0, 0.0).astype(BF16)
    lvl = _level_index(C)
    gn = gn_ref[...]
    sts0 = tuple(s0_ref[0, h].T if has_s0 else jnp.zeros((HEAD_DIM, HEAD_DIM), F32) for h in range(n_heads))

    def body(ci, sts):
        rs = pl.ds(pl.multiple_of(ci * C, C), C)
        out = []
        for h in range(n_heads):
            hs = slice(h * HEAD_DIM, (h + 1) * HEAD_DIM)
            o, st = _hgrn_chunk(q_ref[0, rs, hs], g_ref[0, rs, hs], v_ref[0, rs, hs], sts[h], tri, lvl, n_valid)
            ms = jnp.mean(o * o, axis=-1, keepdims=True)
            o_ref[0, rs, hs] = (o * lax.rsqrt(ms + EPS) * gn * gs_ref[0, rs, hs]).astype(o_ref.dtype)
            out.append(st)
        return tuple(out)

    sts = lax.fori_loop(0, n_chunks, body, sts0) if n_chunks > 1 else body(0, sts0)
    for h in range(n_heads):
        sf_ref[0, h] = sts[h].T


def _hgrn(hq, hlf, hv, hgs, g_hn, s0, C, n_heads, n_valid):
    B, L, _ = hq.shape
    hp = HGRN_HEADS // n_heads
    w = n_heads * HEAD_DIM
    seq = pl.BlockSpec((1, L, w), lambda b, h: (b, 0, h))
    st_spec = pl.BlockSpec((1, n_heads, HEAD_DIM, HEAD_DIM), lambda b, h: (b, h, 0, 0))
    in_specs = [seq, seq, seq, seq, pl.BlockSpec((1, HEAD_DIM), lambda b, h: (0, 0))]
    args = [hq, hlf, hv, hgs, g_hn]
    if s0 is not None:
        in_specs.append(st_spec)
        args.append(s0)
    kern = functools.partial(_hgrn_kernel, C=C, n_chunks=L // C, n_heads=n_heads, n_valid=n_valid,
                             has_s0=s0 is not None)
    return pl.pallas_call(
        kern,
        grid=(B, hp),
        in_specs=in_specs,
        out_specs=[seq, st_spec],
        out_shape=[jax.ShapeDtypeStruct((B, L, HGRN_W), BF16),
                   jax.ShapeDtypeStruct((B, HGRN_HEADS, HEAD_DIM, HEAD_DIM), F32)],
        compiler_params=_cparams(("arbitrary", "arbitrary")),
        name="hgrn",
    )(*args)


def _mem_kv_kernel(x_ref, gn_ref, w_ref, gk_ref, mk_ref, mv_ref, xn_s):
    j = pl.program_id(1)

    @pl.when(j == 0)
    def _():
        x = x_ref[...]
        ms = jnp.mean(x * x, axis=-1, keepdims=True)
        xn_s[...] = (x * lax.rsqrt(ms + EPS) * gn_ref[...]).astype(BF16)

    acc = _dot(xn_s[...], w_ref[...])

    @pl.when(j == 0)
    def _():
        _store_heads(mk_ref, _rms_heads(acc, gk_ref[...], MEM_HEADS, 1.0), MEM_HEADS)

    @pl.when(j == 1)
    def _():
        _store_heads(mv_ref, acc, MEM_HEADS)


def _mem_kv(mem, g_norm, w, g_mk, tm):
    T, D = mem.shape
    out = jax.ShapeDtypeStruct((T, MEM_HEADS, HEAD_DIM), F32)
    return pl.pallas_call(
        _mem_kv_kernel,
        grid=(T // tm, 2),
        in_specs=[pl.BlockSpec((tm, D), lambda i, j: (i, 0)), pl.BlockSpec((1, D), lambda i, j: (0, 0)),
                  pl.BlockSpec((D, MEM_W), lambda i, j: (0, j)), pl.BlockSpec((1, HEAD_DIM), lambda i, j: (0, 0))],
        out_specs=[pl.BlockSpec((tm, MEM_HEADS, HEAD_DIM), lambda i, j: (i, 0, 0))] * 2,
        out_shape=[out, out],
        scratch_shapes=[pltpu.VMEM((tm, D), BF16)],
        compiler_params=_cparams(("arbitrary", "arbitrary")),
        name="mem_kv",
    )(mem, g_norm, w, g_mk)


def _mem_attn_kernel(q_ref, k_ref, v_ref, o_ref):
    for h in range(MEM_HEADS):
        hs = slice(h * HEAD_DIM, (h + 1) * HEAD_DIM)
        s = _dot_nt(q_ref[0, :, hs], k_ref[0, :, h, :].astype(BF16))
        p = jnp.exp(s - jnp.max(s, axis=-1, keepdims=True))
        l = jnp.sum(p, axis=-1, keepdims=True)
        o_ref[0, :, hs] = (_dot(p.astype(BF16), v_ref[0, :, h, :].astype(BF16)) / l).astype(o_ref.dtype)


def _mem_attn(mq, mk, mv, tq):
    B, L, _ = mq.shape
    M = mk.shape[1]
    kv = pl.BlockSpec((1, M, MEM_HEADS, HEAD_DIM), lambda b, i: (b, 0, 0, 0))
    qo = pl.BlockSpec((1, tq, MEM_W), lambda b, i: (b, i, 0))
    return pl.pallas_call(
        _mem_attn_kernel,
        grid=(B, L // tq),
        in_specs=[qo, kv, kv],
        out_specs=qo,
        out_shape=jax.ShapeDtypeStruct((B, L, MEM_W), BF16),
        compiler_params=_cparams(("arbitrary", "arbitrary")),
        name="mem_attn",
    )(mq, mk, mv)


DEC_G = 8
DEC_ROWS = 4 * HEAD_PAD


def _suffix_sum_lanes(x):
    lane = lax.broadcasted_iota(jnp.int32, x.shape, 1)
    s = 1
    while s < LANES:
        x = x + jnp.where(lane + s < LANES, pltpu.roll(x, LANES - s, 1), 0.0)
        s *= 2
    return x


def _prefix_sum_lanes(x):
    lane = lax.broadcasted_iota(jnp.int32, x.shape, 1)
    s = 1
    while s < LANES:
        x = x + jnp.where(lane >= s, pltpu.roll(x, s, 1), 0.0)
        s *= 2
    return x


def _fox_dec_kernel(pt_ref, q_ref, kn_ref, vn_ref, lfn_ref, *refs, n_tok):
    G = DEC_G
    k_refs, v_refs, lf_refs = refs[0:G], refs[G:2 * G], refs[2 * G:3 * G]
    o_ref = refs[3 * G]
    qbd, kn_s, vn_s, m_s, l_s, acc_s, car_s = refs[3 * G + 1:]
    st = pl.program_id(1)

    def attend(k_bf, v_bf, bias):
        s = _dot_nt(qbd[...].astype(BF16), k_bf) + bias
        m_new, l_new, acc_new = _softmax_step(s, v_bf, m_s[...], l_s[...], acc_s[...])
        m_s[...] = m_new
        l_s[...] = l_new
        acc_s[...] = acc_new

    @pl.when(st == 0)
    def _():
        qbd[...] = jnp.zeros_like(qbd)
        q = q_ref[0].astype(F32)
        for t in range(n_tok):
            for h in range(FOX_HEADS):
                hs = slice(h * HEAD_DIM, (h + 1) * HEAD_DIM)
                qbd[t * HEAD_PAD + h:t * HEAD_PAD + h + 1, hs] = q[t:t + 1, hs]
        kn_s[...] = jnp.zeros_like(kn_s)
        vn_s[...] = jnp.zeros_like(vn_s)
        kn_s[0:SUBLANES, :] = kn_ref[0].astype(F32)
        vn_s[0:SUBLANES, :] = vn_ref[0].astype(F32)
        m_s[...] = jnp.full_like(m_s, -jnp.inf)
        l_s[...] = jnp.zeros_like(l_s)
        acc_s[...] = jnp.zeros_like(acc_s)
        car_s[...] = jnp.zeros_like(car_s)
        ecum = _prefix_sum_lanes(lfn_ref[0])
        lane = lax.broadcasted_iota(jnp.int32, (HEAD_PAD, LANES), 1)
        bias = jnp.concatenate([jnp.where(lane <= t, -ecum, -jnp.inf) for t in range(n_tok)], axis=0)
        attend(kn_s[...].astype(BF16), vn_s[...].astype(BF16), bias)

    heads = lambda ref: jnp.concatenate([ref[0, h] for h in range(FOX_HEADS)], axis=-1).astype(BF16)
    carry = car_s[...]
    ds = []
    for i in range(G):
        lf = lf_refs[i][0]
        incl = _suffix_sum_lanes(lf)
        ds.append(carry + (incl - lf))
        carry = carry + incl[:, 0:1]
    car_s[...] = carry
    d_all = jnp.concatenate(ds, axis=-1)
    attend(jnp.concatenate([heads(r) for r in k_refs], axis=0),
           jnp.concatenate([heads(r) for r in v_refs], axis=0),
           jnp.concatenate([d_all] * n_tok, axis=0))

    @pl.when(st == pl.num_programs(1) - 1)
    def _():
        res = acc_s[...] / l_s[...]
        o_ref[...] = jnp.zeros_like(o_ref)
        for t in range(n_tok):
            for h in range(FOX_HEADS):
                hs = slice(h * HEAD_DIM, (h + 1) * HEAD_DIM)
                r = t * HEAD_PAD + h
                o_ref[0, t:t + 1, hs] = res[r:r + 1, hs].astype(o_ref.dtype)


def _fox_dec(page_table, fq, fk, fv, lf_new_t, cache_k, cache_v, cache_lf_t, n_tok):
    B, n_pages = page_table.shape
    G = DEC_G
    assert n_pages % G == 0 and n_tok * HEAD_PAD == DEC_ROWS
    n_steps = n_pages // G
    tok = lambda w: pl.BlockSpec((1, SUBLANES, w), lambda b, s, pt: (b, 0, 0))

    def page_spec(shape, i):
        return pl.BlockSpec((1,) + shape,
                            lambda b, s, pt: (pt[b * n_pages + (n_pages - 1 - (s * G + i))],) + (0,) * len(shape))

    kv_page = (FOX_HEADS, LANES, HEAD_DIM)
    in_specs = ([tok(FOX_W), tok(FOX_W), tok(FOX_W), tok(LANES)]
                + [page_spec(kv_page, i) for i in range(G)]
                + [page_spec(kv_page, i) for i in range(G)]
                + [page_spec((HEAD_PAD, LANES), i) for i in range(G)])
    grid_spec = pltpu.PrefetchScalarGridSpec(
        num_scalar_prefetch=1,
        grid=(B, n_steps),
        in_specs=in_specs,
        out_specs=pl.BlockSpec((1, SUBLANES, FOX_W), lambda b, s, pt: (b, 0, 0)),
        scratch_shapes=[pltpu.VMEM((DEC_ROWS, FOX_W), F32),
                        pltpu.VMEM((LANES, FOX_W), F32), pltpu.VMEM((LANES, FOX_W), F32),
                        pltpu.VMEM((DEC_ROWS, 1), F32), pltpu.VMEM((DEC_ROWS, 1), F32),
                        pltpu.VMEM((DEC_ROWS, FOX_W), F32), pltpu.VMEM((HEAD_PAD, 1), F32)],
    )
    return pl.pallas_call(
        functools.partial(_fox_dec_kernel, n_tok=n_tok),
        grid_spec=grid_spec,
        out_shape=jax.ShapeDtypeStruct((B, SUBLANES, FOX_W), BF16),
        compiler_params=_cparams(("arbitrary", "arbitrary")),
        name="fox_dec",
    )(page_table.reshape(-1), fq, fk, fv, lf_new_t, *([cache_k] * G), *([cache_v] * G), *([cache_lf_t] * G))


def _merge_kernel(fo_p, ho_p, mo_p, x_p, fo_s, ho_s, mo_s, x_s, w_ref, gf_ref, wrh_ref, wrl_ref, br_ref,
                  h_ref, xn_ref, eid_ref, gate_ref, cat_s, x_sc, *, n_p):
    i = pl.program_id(0)
    D = x_sc.shape[1]

    def stage(fo_ref, ho_ref, mo_ref, x_ref):
        cat_s[:, 0:FOX_W] = fo_ref[...]
        cat_s[:, FOX_W:FOX_W + HGRN_W] = ho_ref[...]
        cat_s[:, FOX_W + HGRN_W:] = mo_ref[...]
        x_sc[...] = x_ref[...]

    @pl.when(i < n_p)
    def _():
        stage(fo_p, ho_p, mo_p, x_p)

    @pl.when(i >= n_p)
    def _():
        stage(fo_s, ho_s, mo_s, x_s)

    ssq = jnp.zeros((x_sc.shape[0], 1), F32)
    for c in range(D // PROJ_CHUNK):
        cs = slice(c * PROJ_CHUNK, (c + 1) * PROJ_CHUNK)
        hc = x_sc[:, cs] + _dot(cat_s[...], w_ref[:, cs])
        h_ref[:, cs] = hc
        ssq = ssq + jnp.sum(hc * hc, axis=-1, keepdims=True)
    xn = h_ref[...] * lax.rsqrt(ssq * (1.0 / D) + EPS) * gf_ref[...]
    xn_ref[...] = xn
    x_hi = xn.astype(BF16)
    x_lo = (xn - x_hi.astype(F32)).astype(BF16)
    logits = (_dot(x_hi, wrh_ref[...]) + (_dot(x_hi, wrl_ref[...]) + _dot(x_lo, wrh_ref[...]))) + br_ref[...]
    lane = lax.broadcasted_iota(jnp.int32, logits.shape, 1)
    big = jnp.int32(LANES)
    ninf = -jnp.inf
    gl = jnp.where(lane < N_GROUPS, logits, ninf)
    gmax = jnp.max(gl, axis=-1, keepdims=True)
    g_sel = jnp.min(jnp.where(gl == gmax, lane, big), axis=-1, keepdims=True)
    g_prob = 1.0 / jnp.sum(jnp.exp(gl - gmax), axis=-1, keepdims=True)
    lo = N_GROUPS + EXPERTS_PER_GROUP * g_sel
    el = jnp.where((lane >= lo) & (lane < lo + EXPERTS_PER_GROUP), logits, ninf)
    v1 = jnp.max(el, axis=-1, keepdims=True)
    i1 = jnp.min(jnp.where(el == v1, lane, big), axis=-1, keepdims=True)
    el2 = jnp.where(lane == i1, ninf, el)
    v2 = jnp.max(el2, axis=-1, keepdims=True)
    i2 = jnp.min(jnp.where(el2 == v2, lane, big), axis=-1, keepdims=True)
    t = jnp.exp(v2 - v1)
    w1 = g_prob / (1.0 + t)
    w2 = g_prob * t / (1.0 + t)
    eid_ref[...] = jnp.where(lane == 0, i1 - N_GROUPS, jnp.where(lane == 1, i2 - N_GROUPS, 0))
    gate_ref[...] = jnp.where(lane == 0, w1, jnp.where(lane == 1, w2, 0.0))


def _merge(prompt, sample, w_out, g_ffn, wr_hi, wr_lo, b_r, tm):
    Tp, D = prompt[3].shape
    Ts = sample[3].shape[0]
    n_p, n_s = Tp // tm, Ts // tm
    T = Tp + Ts
    p_row = lambda w: pl.BlockSpec((tm, w), lambda i: (jnp.minimum(i, n_p - 1), 0))
    s_row = lambda w: pl.BlockSpec((tm, w), lambda i: (jnp.maximum(i - n_p, 0), 0))
    row = lambda w: pl.BlockSpec((tm, w), lambda i: (i, 0))
    full = lambda a: pl.BlockSpec(a.shape, lambda i: (0,) * a.ndim)
    widths = (FOX_W, HGRN_W, MEM_W, D)
    return pl.pallas_call(
        functools.partial(_merge_kernel, n_p=n_p),
        grid=(n_p + n_s,),
        in_specs=[p_row(w) for w in widths] + [s_row(w) for w in widths]
                 + [full(w_out), full(g_ffn), full(wr_hi), full(wr_lo), full(b_r)],
        out_specs=[row(D), row(D), row(LANES), row(LANES)],
        out_shape=[jax.ShapeDtypeStruct((T, D), F32), jax.ShapeDtypeStruct((T, D), F32),
                   jax.ShapeDtypeStruct((T, LANES), jnp.int32), jax.ShapeDtypeStruct((T, LANES), F32)],
        scratch_shapes=[pltpu.VMEM((tm, FOX_W + HGRN_W + MEM_W), BF16), pltpu.VMEM((tm, D), F32)],
        compiler_params=_cparams(("arbitrary",)),
        name="merge",
    )(*prompt, *sample, w_out, g_ffn, wr_hi, wr_lo, b_r)


def _experts_kernel(be_ref, nb_ref, tok_ref, x_hbm, wg_ref, wu_ref, wd_ref, y_ref, xbuf, wg_s, wu_s, wd_s, sem):
    b = pl.program_id(0)

    @pl.when((b == 0) | (be_ref[b] != be_ref[jnp.maximum(b - 1, 0)]))
    def _():
        wg_s[...] = wg_ref[0].astype(BF16)
        wu_s[...] = wu_ref[0].astype(BF16)
        wd_s[...] = wd_ref[0].astype(BF16)

    n_used = nb_ref[0]
    BM = MOE_BM

    def gather(blk, slot):
        base = blk * BM
        for r in range(BM):
            pltpu.make_async_copy(x_hbm.at[pl.ds(tok_ref[base + r], 1)], xbuf.at[slot, pl.ds(r, 1)],
                                  sem.at[slot]).start()

    def wait(slot):
        pltpu.make_async_copy(x_hbm.at[pl.ds(0, BM)], xbuf.at[slot], sem.at[slot]).wait()

    slot = lax.rem(b, 2)

    @pl.when(b == 0)
    def _():
        gather(0, 0)

    @pl.when(b + 1 < n_used)
    def _():
        gather(b + 1, 1 - slot)

    @pl.when(b < n_used)
    def _():
        wait(slot)
        x = xbuf[slot].astype(BF16)
        hmid = _dot(x, wg_s[...])
        hmid = hmid * _sigmoid(hmid) * _dot(x, wu_s[...])
        y_ref[...] = _dot(hmid.astype(BF16), wd_s[...])

    @pl.when(b >= n_used)
    def _():
        y_ref[...] = jnp.zeros_like(y_ref)


def _experts(block_e, n_used, slot_tok, xn, w_gate, w_up, w_down):
    n_blocks = block_e.shape[0]
    T, D = xn.shape
    FF = w_gate.shape[2]
    BM = MOE_BM
    grid_spec = pltpu.PrefetchScalarGridSpec(
        num_scalar_prefetch=3,
        grid=(n_blocks,),
        in_specs=[pl.BlockSpec(memory_space=pl.ANY),
                  pl.BlockSpec((1, D, FF), lambda b, be, nb, tk: (be[b], 0, 0)),
                  pl.BlockSpec((1, D, FF), lambda b, be, nb, tk: (be[b], 0, 0)),
                  pl.BlockSpec((1, FF, D), lambda b, be, nb, tk: (be[b], 0, 0))],
        out_specs=pl.BlockSpec((BM, D), lambda b, be, nb, tk: (b, 0)),
        scratch_shapes=[pltpu.VMEM((2, BM, D), F32), pltpu.VMEM((D, FF), BF16), pltpu.VMEM((D, FF), BF16),
                        pltpu.VMEM((FF, D), BF16), pltpu.SemaphoreType.DMA((2,))],
    )
    return pl.pallas_call(
        _experts_kernel,
        grid_spec=grid_spec,
        out_shape=jax.ShapeDtypeStruct((n_blocks * BM, D), F32),
        compiler_params=_cparams(("arbitrary",)),
        name="experts",
    )(block_e, n_used, slot_tok, xn, w_gate, w_up, w_down)


COMB_TM = 128


def _combine_kernel(pos_ref, h_ref, gate_ref, y_hbm, op_ref, os_ref, ybuf, sem, *, n_p):
    i = pl.program_id(0)
    n = pl.num_programs(0)
    TM = COMB_TM

    def gather(blk, slot):
        base = blk * (2 * TM)
        for r in range(2 * TM):
            pltpu.make_async_copy(y_hbm.at[pl.ds(pos_ref[base + r], 1)], ybuf.at[slot, pl.ds(r, 1)],
                                  sem.at[slot]).start()

    def wait(slot):
        pltpu.make_async_copy(y_hbm.at[pl.ds(0, 2 * TM)], ybuf.at[slot], sem.at[slot]).wait()

    slot = lax.rem(i, 2)

    @pl.when(i == 0)
    def _():
        gather(0, 0)

    @pl.when(i + 1 < n)
    def _():
        gather(i + 1, 1 - slot)

    wait(slot)
    g = gate_ref[...]
    res = h_ref[...] + (g[:, 0:1] * ybuf[slot, 0:TM, :] + g[:, 1:2] * ybuf[slot, TM:2 * TM, :])

    @pl.when(i < n_p)
    def _():
        op_ref[...] = res

    @pl.when(i >= n_p)
    def _():
        os_ref[...] = res


def _combine(pos, h, gate, y_slots, t_prompt):
    T, D = h.shape
    TM = COMB_TM
    n_p = t_prompt // TM
    grid_spec = pltpu.PrefetchScalarGridSpec(
        num_scalar_prefetch=1,
        grid=(T // TM,),
        in_specs=[pl.BlockSpec((TM, D), lambda i, p: (i, 0)), pl.BlockSpec((TM, LANES), lambda i, p: (i, 0)),
                  pl.BlockSpec(memory_space=pl.ANY)],
        out_specs=[pl.BlockSpec((TM, D), lambda i, p: (jnp.minimum(i, n_p - 1), 0)),
                   pl.BlockSpec((TM, D), lambda i, p: (jnp.maximum(i - n_p, 0), 0))],
        scratch_shapes=[pltpu.VMEM((2, 2 * TM, D), F32), pltpu.SemaphoreType.DMA((2,))],
    )
    return pl.pallas_call(
        functools.partial(_combine_kernel, n_p=n_p),
        grid_spec=grid_spec,
        out_shape=[jax.ShapeDtypeStruct((t_prompt, D), F32), jax.ShapeDtypeStruct((T - t_prompt, D), F32)],
        compiler_params=_cparams(("arbitrary",)),
        name="combine",
    )(pos, h, gate, y_slots)


def _moe(h, xn, eid, gate, w_gate, w_up, w_down, t_prompt):
    T = h.shape[0]
    A = T * TOP_K
    BM = MOE_BM
    n_blocks = -(-(A + N_EXPERTS * (BM - 1)) // BM)
    e_flat = eid[:, :TOP_K].reshape(A)
    onehot = (e_flat[:, None] == jnp.arange(N_EXPERTS, dtype=jnp.int32)[None, :]).astype(jnp.int32)
    csum = jnp.cumsum(onehot, axis=0)
    counts = csum[-1]
    rank = jnp.take_along_axis(csum, e_flat[:, None], axis=1)[:, 0] - 1
    padded = (counts + BM - 1) // BM * BM
    pad_end = jnp.cumsum(padded)
    pad_start = pad_end - padded
    pos = (pad_start[e_flat] + rank).astype(jnp.int32)
    tok = jnp.arange(A, dtype=jnp.int32) // TOP_K
    filler = jnp.arange(n_blocks * BM, dtype=jnp.int32) % T
    slot_tok = filler.at[pos].set(tok)
    block_first = jnp.arange(n_blocks, dtype=jnp.int32) * BM
    block_e = jnp.minimum(jnp.sum(pad_end[None, :] <= block_first[:, None], axis=1), N_EXPERTS - 1).astype(jnp.int32)
    n_used = (pad_end[-1] // BM).astype(jnp.int32).reshape(1)
    y_slots = _experts(block_e, n_used, slot_tok, xn, w_gate, w_up, w_down)
    pos_tiles = pos.reshape(T // COMB_TM, COMB_TM, TOP_K).transpose(0, 2, 1).reshape(-1)
    return _combine(pos_tiles, h, gate, y_slots, t_prompt)


def _prep_weights(w_in, b_fox_f, w_router_group, b_router_group, w_router_expert, b_router_expert):
    D = w_in.shape[0]
    c = [0]
    for s in (FOX_W, FOX_W, FOX_W, FOX_HEADS, HGRN_W, HGRN_W, HGRN_W, HGRN_W, MEM_W):
        c.append(c[-1] + s)
    w_t = w_in.T
    w_a = w_t[c[0]:c[3]].astype(BF16)
    w_b = w_t[c[4]:c[8]].astype(BF16)
    w_q = w_t[c[8]:c[9]].astype(BF16)
    w_f = jnp.pad(w_t[c[3]:c[4]], ((0, LANES - FOX_HEADS), (0, 0))).astype(BF16)
    b_f_pad = jnp.zeros((1, LANES), F32).at[0, :FOX_HEADS].set(b_fox_f)
    n_r = N_GROUPS + N_EXPERTS
    w_r = jnp.zeros((D, LANES), F32).at[:, :N_GROUPS].set(w_router_group).at[:, N_GROUPS:n_r].set(w_router_expert)
    b_r = jnp.zeros((1, LANES), F32).at[0, :N_GROUPS].set(b_router_group).at[0, N_GROUPS:n_r].set(b_router_expert)
    wr_hi = w_r.astype(BF16)
    wr_lo = (w_r - wr_hi.astype(F32)).astype(BF16)
    return (w_a, w_q, w_f, w_b), b_f_pad, wr_hi, wr_lo, b_r


def kernel(x_prompt, x_sample, cache_fox_k, cache_fox_v, cache_fox_logf, cache_mem_k, cache_mem_v, state_hgrn, page_table, mem_prompt, g_attn_norm, w_in, b_fox_f, g_fox_q, g_fox_k, lb_logits, g_hgrn_out, g_mem_norm, w_mem_kv, g_mem_q, g_mem_k, w_out, g_ffn_norm, w_router_group, b_router_group, w_router_expert, b_router_expert, w_gate_e, w_up_e, w_down_e):
    assert w_in.shape[0] == 1, "single-layer step"
    Bp, S, D = x_prompt.shape
    Bd, L, _ = x_sample.shape
    n_pool, page = cache_fox_k.shape[1], cache_fox_k.shape[2]
    assert page == LANES and L <= SUBLANES
    M = mem_prompt.shape[1]
    l = 0
    row = lambda a: a[l].reshape(1, -1)
    head_major = lambda a: jnp.swapaxes(a, -3, -2)

    w_seg, b_f_pad, wr_hi, wr_lo, b_r = _prep_weights(
        w_in[l], b_fox_f[l], w_router_group[l], b_router_group[l], w_router_expert[l], b_router_expert[l])
    w_out_bf = w_out[l].astype(BF16)
    w_mkv_bf = w_mem_kv[l].astype(BF16)
    experts_w = (w_gate_e[l], w_up_e[l], w_down_e[l])
    proj_args = (row(g_attn_norm), *w_seg, row(g_fox_q), row(g_fox_k), row(g_mem_q), lb_logits, b_f_pad)

    Tp = Bp * S
    xp = x_prompt.reshape(Tp, D)
    fq, fk_p, fkb, fv_p, fvb, hq, hlf, hv, hgs, mq, flf_p = _in_proj(xp, *proj_args, tm=512, seq=S)
    seq = lambda a: a.reshape(Bp, S, a.shape[-1])
    ck = _fox_cum(seq(flf_p))
    fox_o = _fox_attn(seq(fq), seq(fkb), seq(fvb), ck)
    hg_o, s_p = _hgrn(seq(hq), seq(hlf), seq(hv), seq(hgs), row(g_hgrn_out), None, C=128, n_heads=3, n_valid=128)
    mk, mv = _mem_kv(mem_prompt.reshape(Bp * M, D), row(g_mem_norm), w_mkv_bf, row(g_mem_k), tm=256)
    mem4 = lambda a, b: a.reshape(b, M, MEM_HEADS, HEAD_DIM)
    mem_o = _mem_attn(seq(mq), mem4(mk, Bp), mem4(mv, Bp), tq=512)
    prompt = (fox_o.reshape(Tp, FOX_W), hg_o.reshape(Tp, HGRN_W), mem_o.reshape(Tp, MEM_W), xp)

    R = SUBLANES
    Ts = Bd * R
    xs = jnp.pad(x_sample, ((0, 0), (0, R - L), (0, 0))).reshape(Ts, D)
    fq, fk_s, fkb, fv_s, fvb, hq, hlf, hv, hgs, mq, flf_s = _in_proj(xs, *proj_args, tm=Ts, seq=R)
    seqs = lambda a: a.reshape(Bd, R, a.shape[-1])
    lf_new_t = jnp.swapaxes(seqs(flf_s)[:, :, :HEAD_PAD], 1, 2)
    lf_new_t = jnp.pad(lf_new_t, ((0, 0), (0, 0), (0, LANES - R)))
    cache_lf_t = jnp.pad(jnp.swapaxes(cache_fox_logf[l].astype(F32), 1, 2), ((0, 0), (0, HEAD_PAD - FOX_HEADS), (0, 0)))
    fox_o = _fox_dec(page_table, seqs(fq), seqs(fkb), seqs(fvb), lf_new_t, head_major(cache_fox_k[l]), head_major(cache_fox_v[l]),
                     cache_lf_t, n_tok=L)
    hg_o, s_s = _hgrn(seqs(hq), seqs(hlf), seqs(hv), seqs(hgs), row(g_hgrn_out), state_hgrn[l],
                      C=R, n_heads=HGRN_HEADS, n_valid=L)
    mem_o = _mem_attn(seqs(mq), cache_mem_k[l], cache_mem_v[l], tq=R)
    sample = (fox_o.reshape(Ts, FOX_W), hg_o.reshape(Ts, HGRN_W), mem_o.reshape(Ts, MEM_W), xs)

    h, xn, eid, gate = _merge(prompt, sample, w_out_bf, row(g_ffn_norm), wr_hi, wr_lo, b_r, tm=Ts)
    y_p, y_s = _moe(h, xn, eid, gate, *experts_w, t_prompt=Tp)

    cut = lambda a: a.reshape((Bd, R) + a.shape[1:])[:, :L]
    return (y_p.reshape(Bp, S, D), cut(y_s),
            head_major(fk_p)[None], head_major(fv_p)[None],
            flf_p[:, :FOX_HEADS].reshape(1, Bp, S, FOX_HEADS), s_p[None],
            mem4(mk, Bp)[None], mem4(mv, Bp)[None],
            head_major(fk_s)[None, :, :L], head_major(fv_s)[None, :, :L], cut(flf_s)[:, :, :FOX_HEADS][None], s_s[None])
```

```python
import functools

import jax
import jax.numpy as jnp
from jax import lax
from jax.experimental import pallas as pl
from jax.experimental.pallas import tpu as pltpu

F32 = jnp.float32
BF16 = jnp.bfloat16

HEAD_DIM = 128
FOX_HEADS = 6
HGRN_HEADS = 6
MEM_HEADS = 4
FOX_W = FOX_HEADS * HEAD_DIM
HGRN_W = HGRN_HEADS * HEAD_DIM
MEM_W = MEM_HEADS * HEAD_DIM
N_GROUPS = 4
EXPERTS_PER_GROUP = 8
N_EXPERTS = N_GROUPS * EXPERTS_PER_GROUP
TOP_K = 2
EPS = 1e-6
ATTN_SCALE = HEAD_DIM ** -0.5
LANES = 128
SUBLANES = 8
SEG_W = FOX_W
PROJ_CHUNK = 256
HEAD_PAD = 8
MOE_BM = 256
VMEM_LIMIT = 52 * 1024 * 1024

NT_DIMS = (((1,), (1,)), ((), ()))
TN_DIMS = (((0,), (0,)), ((), ()))


def _cparams(sem, vmem=VMEM_LIMIT):
    return pltpu.CompilerParams(dimension_semantics=sem, vmem_limit_bytes=vmem)


def _split3(x):
    hi = x.astype(BF16)
    r1 = x - hi.astype(F32)
    mid = r1.astype(BF16)
    lo = (r1 - mid.astype(F32)).astype(BF16)
    return hi, mid, lo


def _dot(a, b):
    return jnp.dot(a, b, preferred_element_type=F32)


def _dot_nt(a, b):
    return lax.dot_general(a, b, NT_DIMS, preferred_element_type=F32)


def _sigmoid(x):
    return 1.0 / (1.0 + jnp.exp(-x))


def _log_sigmoid(x):
    return jnp.minimum(x, 0.0) - jnp.log(1.0 + jnp.exp(-jnp.abs(x)))


def _rms_heads(a, g, n_heads, scale):
    outs = []
    for h in range(n_heads):
        ah = a[:, h * HEAD_DIM:(h + 1) * HEAD_DIM]
        ms = jnp.mean(ah * ah, axis=-1, keepdims=True)
        outs.append(ah * lax.rsqrt(ms + EPS) * (g * scale))
    return jnp.concatenate(outs, axis=-1)


def _store_heads(ref, a, n_heads):
    for h in range(n_heads):
        ref[:, h, :] = a[:, h * HEAD_DIM:(h + 1) * HEAD_DIM]


def _store_head_major(ref, a, h0):
    nb, _, rows, _ = ref.shape
    for h in range(a.shape[1] // HEAD_DIM):
        ref[:, h0 + h] = a[:, h * HEAD_DIM:(h + 1) * HEAD_DIM].reshape(nb, rows, HEAD_DIM)


def _static_when(cond):
    def deco(f):
        if cond:
            f()
        return f
    return deco


def _in_proj_kernel(*refs, names, segs):
    r = dict(zip(names, refs))
    xn_s = r["xn_s"]
    x = r["x"][...]
    ms = jnp.mean(x * x, axis=-1, keepdims=True)
    xn_s[...] = (x * lax.rsqrt(ms + EPS) * r["g_norm"][...]).astype(BF16)

    def segment(post, w_ref, row0, width=SEG_W):
        for c in range(width // PROJ_CHUNK):
            cs = slice(c * PROJ_CHUNK, (c + 1) * PROJ_CHUNK)
            post(c, cs, _dot_nt(xn_s[...], w_ref[row0 + c * PROJ_CHUNK:row0 + (c + 1) * PROJ_CHUNK, :]))

    heads_per_chunk = PROJ_CHUNK // HEAD_DIM

    @_static_when("fq" in segs)
    def _():
        def post(c, cs, acc):
            r["fq"][:, cs] = _rms_heads(acc, r["g_fq"][...], heads_per_chunk, ATTN_SCALE).astype(BF16)
        segment(post, r["w_a"], 0)

    @_static_when("fk" in segs)
    def _():
        def post(c, cs, acc):
            fk = _rms_heads(acc, r["g_fk"][...], heads_per_chunk, 1.0)
            _store_head_major(r["fk"], fk, c * heads_per_chunk)
            r["fkb"][:, cs] = fk.astype(BF16)
        segment(post, r["w_a"], SEG_W)

    @_static_when("fv" in segs)
    def _():
        def post(c, cs, acc):
            _store_head_major(r["fv"], acc, c * heads_per_chunk)
            r["fvb"][:, cs] = acc.astype(BF16)
        segment(post, r["w_a"], 2 * SEG_W)

    @_static_when("mq" in segs)
    def _():
        def post(c, cs, acc):
            r["mq"][:, cs] = _rms_heads(acc, r["g_mq"][...], heads_per_chunk, ATTN_SCALE).astype(BF16)
        segment(post, r["w_q"], 0, MEM_W)

    @_static_when("flf" in segs)
    def _():
        r["flf"][...] = _log_sigmoid(_dot_nt(xn_s[...], r["w_f"][...]) + r["b_f"][...])

    @_static_when("hq" in segs)
    def _():
        def post(c, cs, acc):
            r["hq"][:, cs] = acc * _sigmoid(acc) * ATTN_SCALE
        segment(post, r["w_b"], 0)

    @_static_when("hlf" in segs)
    def _():
        l = r["lb_logits"][...]
        mx = jnp.max(l, axis=0, keepdims=True)
        ex = jnp.exp(l - mx)
        lb = ex[0:1, :] / jnp.sum(ex, axis=0, keepdims=True)

        def post(c, cs, acc):
            r["hlf"][:, cs] = jnp.log(lb[:, cs] + (1.0 - lb[:, cs]) * _sigmoid(acc))
        segment(post, r["w_b"], SEG_W)

    @_static_when("hv" in segs)
    def _():
        def post(c, cs, acc):
            r["hv"][:, cs] = acc
        segment(post, r["w_b"], 2 * SEG_W)

    @_static_when("hgs" in segs)
    def _():
        def post(c, cs, acc):
            r["hgs"][:, cs] = acc * _sigmoid(acc)
        segment(post, r["w_b"], 3 * SEG_W)


def _in_proj_call(x, ins, outs, tm, seq, name):
    T, D = x.shape
    nb, rows = max(1, tm // seq), min(tm, seq)
    assert nb * rows == tm and seq % rows == 0
    tiles_per_seq = seq // rows
    resident = lambda a: pl.BlockSpec(a.shape, lambda i: (0,) * a.ndim, pipeline_mode=pl.Buffered(1))

    def out_spec(o):
        if len(o.shape) == 4:
            return pl.BlockSpec((nb, FOX_HEADS, rows, HEAD_DIM), lambda i: (i // tiles_per_seq, 0, i % tiles_per_seq, 0))
        return pl.BlockSpec((tm, o.shape[1]), lambda i: (i, 0))

    names = ("x",) + tuple(ins) + tuple(outs) + ("xn_s",)
    res = pl.pallas_call(
        functools.partial(_in_proj_kernel, names=names, segs=tuple(outs)),
        grid=(T // tm,),
        in_specs=[pl.BlockSpec((tm, D), lambda i: (i, 0))] + [resident(a) for a in ins.values()],
        out_specs=[out_spec(o) for o in outs.values()],
        out_shape=list(outs.values()),
        scratch_shapes=[pltpu.VMEM((tm, D), BF16)],
        compiler_params=_cparams(("arbitrary",)),
        name=name,
    )(x, *ins.values())
    return dict(zip(outs, res))


def _in_proj(x, g_norm, w_a, w_q, w_f, w_b, g_fq, g_fk, g_mq, lb_logits, b_f_pad, tm, seq):
    T = x.shape[0]
    kv_cache = jax.ShapeDtypeStruct((T // seq, FOX_HEADS, seq, HEAD_DIM), F32)
    rows = lambda w, dt: jax.ShapeDtypeStruct((T, w), dt)
    fox = _in_proj_call(
        x, dict(g_norm=g_norm, w_a=w_a, w_q=w_q, w_f=w_f, g_fq=g_fq, g_fk=g_fk, g_mq=g_mq, b_f=b_f_pad),
        dict(fq=rows(FOX_W, BF16),
             fk=kv_cache,
             fkb=rows(FOX_W, BF16),
             fv=kv_cache, fvb=rows(FOX_W, BF16),
             mq=rows(MEM_W, BF16),
             flf=rows(LANES, F32)),
        tm, seq, "in_proj_fox")
    hg = _in_proj_call(
        x, dict(g_norm=g_norm, w_b=w_b, lb_logits=lb_logits),
        dict(hq=rows(HGRN_W, F32),
             hlf=rows(HGRN_W, F32),
             hv=rows(HGRN_W, F32),
             hgs=rows(HGRN_W, F32)),
        tm, seq, "in_proj_hgrn")
    return (fox["fq"], fox["fk"], fox["fkb"], fox["fv"], fox["fvb"], hg["hq"], hg["hlf"], hg["hv"], hg["hgs"],
            fox["mq"], fox["flf"])


CUM_C = 256


def _fox_cum_kernel(lf_ref, ck_ref):
    S = lf_ref.shape[1]
    r = lax.broadcasted_iota(jnp.int32, (CUM_C, CUM_C), 0)
    c = lax.broadcasted_iota(jnp.int32, (CUM_C, CUM_C), 1)
    tri = jnp.where(c <= r, 1.0, 0.0).astype(BF16)
    carry = jnp.zeros((1, LANES), F32)
    for i in range(S // CUM_C):
        sl = slice(i * CUM_C, (i + 1) * CUM_C)
        hi, mid, lo = _split3(lf_ref[0, sl, :])
        cum = (_dot(tri, hi) + _dot(tri, mid)) + _dot(tri, lo) + carry
        carry = cum[CUM_C - 1:CUM_C, :]
        ck_ref[0, :, sl] = cum.T[0:HEAD_PAD, :]


def _fox_cum(flf):
    B, S, _ = flf.shape
    return pl.pallas_call(
        _fox_cum_kernel,
        grid=(B,),
        in_specs=[pl.BlockSpec((1, S, LANES), lambda b: (b, 0, 0))],
        out_specs=pl.BlockSpec((1, HEAD_PAD, S), lambda b: (b, 0, 0)),
        out_shape=jax.ShapeDtypeStruct((B, HEAD_PAD, S), F32),
        compiler_params=_cparams(("arbitrary",)),
        name="fox_cum",
    )(flf)


FOX_T = 256


def _softmax_step(s, v_bf, m, l, acc):
    m_new = jnp.maximum(m, jnp.max(s, axis=-1, keepdims=True))
    alpha = jnp.exp(m - m_new)
    p = jnp.exp(s - m_new)
    l = alpha * l + jnp.sum(p, axis=-1, keepdims=True)
    acc = alpha * acc + _dot(p.astype(BF16), v_bf)
    return m_new, l, acc


def _fox_attn_kernel(q_ref, k_ref, v_ref, ck_ref, o_ref, m_s, l_s, acc_s):
    qi = pl.program_id(1)
    T = FOX_T
    m_s[...] = jnp.full_like(m_s, -jnp.inf)
    l_s[...] = jnp.zeros_like(l_s)
    acc_s[...] = jnp.zeros_like(acc_s)
    row = lax.broadcasted_iota(jnp.int32, (T, T), 0)
    col = lax.broadcasted_iota(jnp.int32, (T, T), 1)
    causal = col <= row

    def tile(ks, masked):
        for h in range(FOX_HEADS):
            hs = slice(h * HEAD_DIM, (h + 1) * HEAD_DIM)
            s = _dot_nt(q_ref[0, :, hs], k_ref[0, pl.ds(ks, T), hs]) - ck_ref[0, h:h + 1, pl.ds(ks, T)]
            if masked:
                s = jnp.where(causal, s, -jnp.inf)
            m_old = m_s[:, hs]
            m_new = jnp.maximum(m_old, jnp.max(s, axis=-1, keepdims=True))
            alpha = jnp.exp(m_old - m_new)
            p = jnp.exp(s - jnp.concatenate([m_new] * (T // HEAD_DIM), axis=-1))
            m_s[:, hs] = m_new
            l_s[:, hs] = alpha * l_s[:, hs] + jnp.sum(p, axis=-1, keepdims=True)
            acc_s[:, hs] = alpha * acc_s[:, hs] + _dot(p.astype(BF16), v_ref[0, pl.ds(ks, T), hs])

    def body(kt, c):
        tile(pl.multiple_of(kt * T, T), False)
        return c

    lax.fori_loop(0, qi, body, 0)
    tile(pl.multiple_of(qi * T, T), True)
    o_ref[0] = (acc_s[...] / l_s[...]).astype(o_ref.dtype)


def _fox_attn(fq, fk, fv, ck):
    B, S, _ = fq.shape
    T = FOX_T
    return pl.pallas_call(
        _fox_attn_kernel,
        grid=(B, S // T),
        in_specs=[pl.BlockSpec((1, T, FOX_W), lambda b, i: (b, i, 0)),
                  pl.BlockSpec((1, S, FOX_W), lambda b, i: (b, 0, 0)),
                  pl.BlockSpec((1, S, FOX_W), lambda b, i: (b, 0, 0)),
                  pl.BlockSpec((1, HEAD_PAD, S), lambda b, i: (b, 0, 0))],
        out_specs=pl.BlockSpec((1, T, FOX_W), lambda b, i: (b, i, 0)),
        out_shape=jax.ShapeDtypeStruct((B, S, FOX_W), BF16),
        scratch_shapes=[pltpu.VMEM((T, FOX_W), F32)] * 3,
        compiler_params=_cparams(("arbitrary", "arbitrary")),
        name="fox_attn",
    )(fq, fk, fv, ck)


def _level_ref(b, m):
    C = b.shape[0]
    if 2 * m >= SUBLANES:
        b3 = b.reshape(C // (2 * m), 2 * m, HEAD_DIM)
        r = jnp.broadcast_to(b3[:, m - 1:m, :], b3.shape)
        return r.reshape(C, HEAD_DIM)
    b3 = b.reshape(C // SUBLANES, SUBLANES, HEAD_DIM)
    sub = lax.broadcasted_iota(jnp.int32, b3.shape, 1)
    pick = lambda i: jnp.broadcast_to(b3[:, i:i + 1, :], b3.shape)
    if m == 2:
        r = jnp.where(sub < 4, pick(1), pick(5))
    else:
        r = jnp.where(sub < 2, pick(0), jnp.where(sub < 4, pick(2), jnp.where(sub < 6, pick(4), pick(6))))
    return r.reshape(C, HEAD_DIM)


def _level_index(C):
    t = lax.broadcasted_iota(jnp.int32, (C, C), 0)
    s = lax.broadcasted_iota(jnp.int32, (C, C), 1)
    x = jnp.bitwise_xor(t, s)
    lvl = jnp.full((C, C), -1, jnp.int32)
    j, m = 0, 1
    while m < C:
        lvl = jnp.where(x >= m, j, lvl)
        j, m = j + 1, 2 * m
    return jnp.where(t > s, lvl, -1)


def _hgrn_chunk(q, g, v, st, tri, lvl, n_valid):
    C = q.shape[0]
    hi, mid, lo = _split3(g)
    b = (_dot(tri, hi) + _dot(tri, mid)) + _dot(tri, lo)
    k = 1.0 - jnp.exp(g)
    a = jnp.zeros((C, C), F32)
    j, m = 0, 1
    while m < C:
        e = jnp.exp(-jnp.abs(b - _level_ref(b, m)))
        a_l = _dot_nt((q * e).astype(BF16), (k * e).astype(BF16))
        a = jnp.where(lvl == j, a_l, a)
        j, m = j + 1, 2 * m
    v_bf = v.astype(BF16)
    diag = jnp.sum(q * k, axis=-1, keepdims=True)
    o = _dot_nt((q * jnp.exp(b)).astype(BF16), st.astype(BF16)) + _dot(a.astype(BF16), v_bf) + diag * v
    b_last = b[n_valid - 1:n_valid, :]
    kt = k * jnp.exp(jnp.minimum(b_last - b, 0.0))
    if n_valid < C:
        rows = lax.broadcasted_iota(jnp.int32, (C, HEAD_DIM), 0)
        kt = jnp.where(rows < n_valid, kt, 0.0)
    st_new = st * jnp.exp(b_last) + lax.dot_general(v_bf, kt.astype(BF16), TN_DIMS, preferred_element_type=F32)
    return o, st_new


def _hgrn_kernel(*refs, C, n_chunks, n_heads, n_valid, has_s0):
    if has_s0:
        q_ref, g_ref, v_ref, gs_ref, gn_ref, s0_ref, o_ref, sf_ref = refs
    else:
        q_ref, g_ref, v_ref, gs_ref, gn_ref, o_ref, sf_ref = refs
    r = lax.broadcasted_iota(jnp.int32, (C, C), 0)
    c = lax.broadcasted_iota(jnp.int32, (C, C), 1)
    tri = jnp.where(c <= r, 1.0, 0.0).astype(BF16)
    lvl = _level_index(C)
    gn = gn_ref[...]
    sts0 = tuple(s0_ref[0, h].T if has_s0 else jnp.zeros((HEAD_DIM, HEAD_DIM), F32) for h in range(n_heads))

    def body(ci, sts):
        rs = pl.ds(pl.multiple_of(ci * C, C), C)
        out = []
        for h in range(n_heads):
            hs = slice(h * HEAD_DIM, (h + 1) * HEAD_DIM)
            o, st = _hgrn_chunk(q_ref[0, rs, hs], g_ref[0, rs, hs], v_ref[0, rs, hs], sts[h], tri, lvl, n_valid)
            ms = jnp.mean(o * o, axis=-1, keepdims=True)
            o_ref[0, rs, hs] = (o * lax.rsqrt(ms + EPS) * gn * gs_ref[0, rs, hs]).astype(o_ref.dtype)
            out.append(st)
        return tuple(out)

    sts = lax.fori_loop(0, n_chunks, body, sts0) if n_chunks > 1 else body(0, sts0)
    for h in range(n_heads):
        sf_ref[0, h] = sts[h].T


def _hgrn(hq, hlf, hv, hgs, g_hn, s0, C, n_heads, n_valid):
    B, L, _ = hq.shape
    hp = HGRN_HEADS // n_heads
    w = n_heads * HEAD_DIM
    seq = pl.BlockSpec((1, L, w), lambda b, h: (b, 0, h))
    st_spec = pl.BlockSpec((1, n_heads, HEAD_DIM, HEAD_DIM), lambda b, h: (b, h, 0, 0))
    in_specs = [seq, seq, seq, seq, pl.BlockSpec((1, HEAD_DIM), lambda b, h: (0, 0))]
    args = [hq, hlf, hv, hgs, g_hn]
    if s0 is not None:
        in_specs.append(st_spec)
        args.append(s0)
    kern = functools.partial(_hgrn_kernel, C=C, n_chunks=L // C, n_heads=n_heads, n_valid=n_valid,
                             has_s0=s0 is not None)
    return pl.pallas_call(
        kern,
        grid=(B, hp),
        in_specs=in_specs,
        out_specs=[seq, st_spec],
        out_shape=[jax.ShapeDtypeStruct((B, L, HGRN_W), BF16),
                   jax.ShapeDtypeStruct((B, HGRN_HEADS, HEAD_DIM, HEAD_DIM), F32)],
        compiler_params=_cparams(("arbitrary", "arbitrary")),
        name="hgrn",
    )(*args)


def _mem_kv_kernel(x_ref, gn_ref, w_ref, gk_ref, mk_ref, mv_ref, xn_s):
    j = pl.program_id(1)

    @pl.when(j == 0)
    def _():
        x = x_ref[...]
        ms = jnp.mean(x * x, axis=-1, keepdims=True)
        xn_s[...] = (x * lax.rsqrt(ms + EPS) * gn_ref[...]).astype(BF16)

    acc = _dot(xn_s[...], w_ref[...])

    @pl.when(j == 0)
    def _():
        _store_heads(mk_ref, _rms_heads(acc, gk_ref[...], MEM_HEADS, 1.0), MEM_HEADS)

    @pl.when(j == 1)
    def _():
        _store_heads(mv_ref, acc, MEM_HEADS)


def _mem_kv(mem, g_norm, w, g_mk, tm):
    T, D = mem.shape
    out = jax.ShapeDtypeStruct((T, MEM_HEADS, HEAD_DIM), F32)
    return pl.pallas_call(
        _mem_kv_kernel,
        grid=(T // tm, 2),
        in_specs=[pl.BlockSpec((tm, D), lambda i, j: (i, 0)), pl.BlockSpec((1, D), lambda i, j: (0, 0)),
                  pl.BlockSpec((D, MEM_W), lambda i, j: (0, j)), pl.BlockSpec((1, HEAD_DIM), lambda i, j: (0, 0))],
        out_specs=[pl.BlockSpec((tm, MEM_HEADS, HEAD_DIM), lambda i, j: (i, 0, 0))] * 2,
        out_shape=[out, out],
        scratch_shapes=[pltpu.VMEM((tm, D), BF16)],
        compiler_params=_cparams(("arbitrary", "arbitrary")),
        name="mem_kv",
    )(mem, g_norm, w, g_mk)


def _mem_attn_kernel(q_ref, k_ref, v_ref, o_ref):
    for h in range(MEM_HEADS):
        hs = slice(h * HEAD_DIM, (h + 1) * HEAD_DIM)
        s = _dot_nt(q_ref[0, :, hs], k_ref[0, :, h, :].astype(BF16))
        p = jnp.exp(s - jnp.max(s, axis=-1, keepdims=True))
        l = jnp.sum(p, axis=-1, keepdims=True)
        o_ref[0, :, hs] = (_dot(p.astype(BF16), v_ref[0, :, h, :].astype(BF16)) / l).astype(o_ref.dtype)


def _mem_attn(mq, mk, mv, tq):
    B, L, _ = mq.shape
    M = mk.shape[1]
    kv = pl.BlockSpec((1, M, MEM_HEADS, HEAD_DIM), lambda b, i: (b, 0, 0, 0))
    qo = pl.BlockSpec((1, tq, MEM_W), lambda b, i: (b, i, 0))
    return pl.pallas_call(
        _mem_attn_kernel,
        grid=(B, L // tq),
        in_specs=[qo, kv, kv],
        out_specs=qo,
        out_shape=jax.ShapeDtypeStruct((B, L, MEM_W), BF16),
        compiler_params=_cparams(("arbitrary", "arbitrary")),
        name="mem_attn",
    )(mq, mk, mv)


DEC_G = 8
DEC_ROWS = 4 * HEAD_PAD


def _suffix_sum_lanes(x):
    lane = lax.broadcasted_iota(jnp.int32, x.shape, 1)
    s = 1
    while s < LANES:
        x = x + jnp.where(lane + s < LANES, pltpu.roll(x, LANES - s, 1), 0.0)
        s *= 2
    return x


def _prefix_sum_lanes(x):
    lane = lax.broadcasted_iota(jnp.int32, x.shape, 1)
    s = 1
    while s < LANES:
        x = x + jnp.where(lane >= s, pltpu.roll(x, s, 1), 0.0)
        s *= 2
    return x


def _fox_dec_kernel(pt_ref, q_ref, kn_ref, vn_ref, lfn_ref, *refs, n_tok):
    G = DEC_G
    k_refs, v_refs, lf_refs = refs[0:G], refs[G:2 * G], refs[2 * G:3 * G]
    o_ref = refs[3 * G]
    qbd, kn_s, vn_s, m_s, l_s, acc_s, car_s = refs[3 * G + 1:]
    st = pl.program_id(1)

    def attend(k_bf, v_bf, bias):
        s = _dot_nt(qbd[...].astype(BF16), k_bf) + bias
        m_new, l_new, acc_new = _softmax_step(s, v_bf, m_s[...], l_s[...], acc_s[...])
        m_s[...] = m_new
        l_s[...] = l_new
        acc_s[...] = acc_new

    @pl.when(st == 0)
    def _():
        qbd[...] = jnp.zeros_like(qbd)
        q = q_ref[0].astype(F32)
        for t in range(n_tok):
            for h in range(FOX_HEADS):
                hs = slice(h * HEAD_DIM, (h + 1) * HEAD_DIM)
                qbd[t * HEAD_PAD + h:t * HEAD_PAD + h + 1, hs] = q[t:t + 1, hs]
        kn_s[...] = jnp.zeros_like(kn_s)
        vn_s[...] = jnp.zeros_like(vn_s)
        kn_s[0:SUBLANES, :] = kn_ref[0].astype(F32)
        vn_s[0:SUBLANES, :] = vn_ref[0].astype(F32)
        m_s[...] = jnp.full_like(m_s, -jnp.inf)
        l_s[...] = jnp.zeros_like(l_s)
        acc_s[...] = jnp.zeros_like(acc_s)
        car_s[...] = jnp.zeros_like(car_s)
        ecum = _prefix_sum_lanes(lfn_ref[0])
        lane = lax.broadcasted_iota(jnp.int32, (HEAD_PAD, LANES), 1)
        bias = jnp.concatenate([jnp.where(lane <= t, -ecum, -jnp.inf) for t in range(n_tok)], axis=0)
        attend(kn_s[...].astype(BF16), vn_s[...].astype(BF16), bias)

    heads = lambda ref: jnp.concatenate([ref[0, h] for h in range(FOX_HEADS)], axis=-1).astype(BF16)
    carry = car_s[...]
    ds = []
    for i in range(G):
        lf = lf_refs[i][0]
        incl = _suffix_sum_lanes(lf)
        ds.append(carry + (incl - lf))
        carry = carry + incl[:, 0:1]
    car_s[...] = carry
    d_all = jnp.concatenate(ds, axis=-1)
    attend(jnp.concatenate([heads(r) for r in k_refs], axis=0),
           jnp.concatenate([heads(r) for r in v_refs], axis=0),
           jnp.concatenate([d_all] * n_tok, axis=0))

    @pl.when(st == pl.num_programs(1) - 1)
    def _():
        res = acc_s[...] / l_s[...]
        o_ref[...] = jnp.zeros_like(o_ref)
        for t in range(n_tok):
            for h in range(FOX_HEADS):
                hs = slice(h * HEAD_DIM, (h + 1) * HEAD_DIM)
                r = t * HEAD_PAD + h
                o_ref[0, t:t + 1, hs] = res[r:r + 1, hs].astype(o_ref.dtype)


def _fox_dec(page_table, fq, fk, fv, lf_new_t, cache_k, cache_v, cache_lf_t, n_tok):
    B, n_pages = page_table.shape
    G = DEC_G
    assert n_pages % G == 0 and n_tok * HEAD_PAD == DEC_ROWS
    n_steps = n_pages // G
    tok = lambda w: pl.BlockSpec((1, SUBLANES, w), lambda b, s, pt: (b, 0, 0))

    def page_spec(shape, i):
        return pl.BlockSpec((1,) + shape,
                            lambda b, s, pt: (pt[b * n_pages + (n_pages - 1 - (s * G + i))],) + (0,) * len(shape))

    kv_page = (FOX_HEADS, LANES, HEAD_DIM)
    in_specs = ([tok(FOX_W), tok(FOX_W), tok(FOX_W), tok(LANES)]
                + [page_spec(kv_page, i) for i in range(G)]
                + [page_spec(kv_page, i) for i in range(G)]
                + [page_spec((HEAD_PAD, LANES), i) for i in range(G)])
    grid_spec = pltpu.PrefetchScalarGridSpec(
        num_scalar_prefetch=1,
        grid=(B, n_steps),
        in_specs=in_specs,
        out_specs=pl.BlockSpec((1, SUBLANES, FOX_W), lambda b, s, pt: (b, 0, 0)),
        scratch_shapes=[pltpu.VMEM((DEC_ROWS, FOX_W), F32),
                        pltpu.VMEM((LANES, FOX_W), F32), pltpu.VMEM((LANES, FOX_W), F32),
                        pltpu.VMEM((DEC_ROWS, 1), F32), pltpu.VMEM((DEC_ROWS, 1), F32),
                        pltpu.VMEM((DEC_ROWS, FOX_W), F32), pltpu.VMEM((HEAD_PAD, 1), F32)],
    )
    return pl.pallas_call(
        functools.partial(_fox_dec_kernel, n_tok=n_tok),
        grid_spec=grid_spec,
        out_shape=jax.ShapeDtypeStruct((B, SUBLANES, FOX_W), BF16),
        compiler_params=_cparams(("arbitrary", "arbitrary")),
        name="fox_dec",
    )(page_table.reshape(-1), fq, fk, fv, lf_new_t, *([cache_k] * G), *([cache_v] * G), *([cache_lf_t] * G))


def _merge_kernel(fo_p, ho_p, mo_p, x_p, fo_s, ho_s, mo_s, x_s, w_ref, gf_ref, wr2_ref, br_ref,
                  h_ref, xn_ref, eid_ref, gate_ref, cat_s, x_sc, *, n_p):
    i = pl.program_id(0)
    D = x_sc.shape[1]

    def stage(fo_ref, ho_ref, mo_ref, x_ref):
        cat_s[:, 0:FOX_W] = fo_ref[...]
        cat_s[:, FOX_W:FOX_W + HGRN_W] = ho_ref[...]
        cat_s[:, FOX_W + HGRN_W:] = mo_ref[...]
        x_sc[...] = x_ref[...]

    @pl.when(i < n_p)
    def _():
        stage(fo_p, ho_p, mo_p, x_p)

    @pl.when(i >= n_p)
    def _():
        stage(fo_s, ho_s, mo_s, x_s)

    ssq = jnp.zeros((x_sc.shape[0], 1), F32)
    for c in range(D // PROJ_CHUNK):
        cs = slice(c * PROJ_CHUNK, (c + 1) * PROJ_CHUNK)
        hc = x_sc[:, cs] + _dot(cat_s[...], w_ref[:, cs])
        h_ref[:, cs] = hc
        ssq = ssq + jnp.sum(hc * hc, axis=-1, keepdims=True)
    xn = h_ref[...] * lax.rsqrt(ssq * (1.0 / D) + EPS) * gf_ref[...]
    xn_ref[...] = xn
    x_hi = xn.astype(BF16)
    x_lo = (xn - x_hi.astype(F32)).astype(BF16)
    hi2 = _dot(x_hi, wr2_ref[...])
    logits = (hi2[:, :LANES] + (hi2[:, LANES:] + _dot(x_lo, wr2_ref[:, :LANES]))) + br_ref[...]
    lane = lax.broadcasted_iota(jnp.int32, logits.shape, 1)
    big = jnp.int32(LANES)
    ninf = -jnp.inf
    gl = jnp.where(lane < N_GROUPS, logits, ninf)
    gmax = jnp.max(gl, axis=-1, keepdims=True)
    g_sel = jnp.min(jnp.where(gl == gmax, lane, big), axis=-1, keepdims=True)
    g_prob = 1.0 / jnp.sum(jnp.exp(gl - gmax), axis=-1, keepdims=True)
    lo = N_GROUPS + EXPERTS_PER_GROUP * g_sel
    el = jnp.where((lane >= lo) & (lane < lo + EXPERTS_PER_GROUP), logits, ninf)
    v1 = jnp.max(el, axis=-1, keepdims=True)
    i1 = jnp.min(jnp.where(el == v1, lane, big), axis=-1, keepdims=True)
    el2 = jnp.where(lane == i1, ninf, el)
    v2 = jnp.max(el2, axis=-1, keepdims=True)
    i2 = jnp.min(jnp.where(el2 == v2, lane, big), axis=-1, keepdims=True)
    t = jnp.exp(v2 - v1)
    w1 = g_prob / (1.0 + t)
    w2 = g_prob * t / (1.0 + t)
    eid_ref[...] = jnp.where(lane == 0, i1 - N_GROUPS, jnp.where(lane == 1, i2 - N_GROUPS, 0))
    gate_ref[...] = jnp.where(lane == 0, w1, jnp.where(lane == 1, w2, 0.0))


def _merge(prompt, sample, w_out, g_ffn, wr2, b_r, tm):
    Tp, D = prompt[3].shape
    Ts = sample[3].shape[0]
    n_p, n_s = Tp // tm, Ts // tm
    T = Tp + Ts
    p_row = lambda w: pl.BlockSpec((tm, w), lambda i: (jnp.minimum(i, n_p - 1), 0))
    s_row = lambda w: pl.BlockSpec((tm, w), lambda i: (jnp.maximum(i - n_p, 0), 0))
    row = lambda w: pl.BlockSpec((tm, w), lambda i: (i, 0))
    full = lambda a: pl.BlockSpec(a.shape, lambda i: (0,) * a.ndim)
    widths = (FOX_W, HGRN_W, MEM_W, D)
    return pl.pallas_call(
        functools.partial(_merge_kernel, n_p=n_p),
        grid=(n_p + n_s,),
        in_specs=[p_row(w) for w in widths] + [s_row(w) for w in widths]
                 + [full(w_out), full(g_ffn), full(wr2), full(b_r)],
        out_specs=[row(D), row(D), row(LANES), row(LANES)],
        out_shape=[jax.ShapeDtypeStruct((T, D), F32), jax.ShapeDtypeStruct((T, D), F32),
                   jax.ShapeDtypeStruct((T, LANES), jnp.int32), jax.ShapeDtypeStruct((T, LANES), F32)],
        scratch_shapes=[pltpu.VMEM((tm, FOX_W + HGRN_W + MEM_W), BF16), pltpu.VMEM((tm, D), F32)],
        compiler_params=_cparams(("arbitrary",)),
        name="merge",
    )(*prompt, *sample, w_out, g_ffn, wr2, b_r)


def _experts_kernel(be_ref, nb_ref, src_ref, tok_ref, x_hbm, wg_ref, wu_ref, wd_ref, y_ref,
                    xbuf, wg_s, wu_s, wd_s, sem):
    b = pl.program_id(0)

    @pl.when((b == 0) | (be_ref[b] != be_ref[jnp.maximum(b - 1, 0)]))
    def _():
        wg_s[...] = wg_ref[0].astype(BF16)
        wu_s[...] = wu_ref[0].astype(BF16)
        wd_s[...] = wd_ref[0].astype(BF16)

    n_used = nb_ref[0]
    BM = MOE_BM

    def gather(blk, slot):
        base = src_ref[blk]
        for r in range(BM):
            pltpu.make_async_copy(x_hbm.at[pl.ds(tok_ref[base + r], 1)], xbuf.at[slot, pl.ds(r, 1)],
                                  sem.at[slot]).start()

    def wait(slot):
        pltpu.make_async_copy(x_hbm.at[pl.ds(0, BM)], xbuf.at[slot], sem.at[slot]).wait()

    slot = lax.rem(b, 2)

    @pl.when(b == 0)
    def _():
        gather(0, 0)

    for s in range(2):
        @pl.when((b + 1 < n_used) & (slot == s))
        def _():
            gather(b + 1, 1 - s)

    @pl.when(b < n_used)
    def _():
        wait(slot)
        x = xbuf[slot].astype(BF16)
        hmid = _dot(x, wg_s[...])
        hmid = hmid * _sigmoid(hmid) * _dot(x, wu_s[...])
        y_ref[...] = _dot(hmid.astype(BF16), wd_s[...])

    @pl.when(b >= n_used)
    def _():
        y_ref[...] = jnp.zeros_like(y_ref)


def _experts(block_e, n_used, block_src, sorted_tok, xn, w_gate, w_up, w_down):
    n_blocks = block_e.shape[0]
    D, FF = w_gate.shape[1:]
    BM = MOE_BM
    grid_spec = pltpu.PrefetchScalarGridSpec(
        num_scalar_prefetch=4,
        grid=(n_blocks,),
        in_specs=[pl.BlockSpec(memory_space=pl.ANY),
                  pl.BlockSpec((1, D, FF), lambda b, be, *_: (be[b], 0, 0)),
                  pl.BlockSpec((1, D, FF), lambda b, be, *_: (be[b], 0, 0)),
                  pl.BlockSpec((1, FF, D), lambda b, be, *_: (be[b], 0, 0))],
        out_specs=pl.BlockSpec((BM, D), lambda b, *_: (b, 0)),
        scratch_shapes=[pltpu.VMEM((2, BM, D), F32), pltpu.VMEM((D, FF), BF16), pltpu.VMEM((D, FF), BF16),
                        pltpu.VMEM((FF, D), BF16), pltpu.SemaphoreType.DMA((2,))],
    )
    return pl.pallas_call(
        _experts_kernel,
        grid_spec=grid_spec,
        out_shape=jax.ShapeDtypeStruct((n_blocks * BM, D), F32),
        compiler_params=_cparams(("arbitrary",)),
        name="experts",
    )(block_e, n_used, block_src, sorted_tok, xn, w_gate, w_up, w_down)


COMB_TM = 128


def _combine_kernel(pos_ref, h_ref, gate_ref, y_hbm, op_ref, os_ref, ybuf, sem, *, n_p):
    i = pl.program_id(0)
    n = pl.num_programs(0)
    TM = COMB_TM

    def gather(blk, slot):
        base = blk * (2 * TM)
        for r in range(2 * TM):
            pltpu.make_async_copy(y_hbm.at[pl.ds(pos_ref[base + r], 1)], ybuf.at[slot, pl.ds(r, 1)],
                                  sem.at[slot]).start()

    def wait(slot):
        pltpu.make_async_copy(y_hbm.at[pl.ds(0, 2 * TM)], ybuf.at[slot], sem.at[slot]).wait()

    slot = lax.rem(i, 2)

    @pl.when(i == 0)
    def _():
        gather(0, 0)

    for s in range(2):
        @pl.when((i + 1 < n) & (slot == s))
        def _():
            gather(i + 1, 1 - s)

    wait(slot)
    g = gate_ref[...]
    res = h_ref[...] + (g[:, 0:1] * ybuf[slot, 0:TM, :] + g[:, 1:2] * ybuf[slot, TM:2 * TM, :])

    @pl.when(i < n_p)
    def _():
        op_ref[...] = res

    @pl.when(i >= n_p)
    def _():
        os_ref[...] = res


def _combine(pos, h, gate, y_slots, t_prompt):
    T, D = h.shape
    TM = COMB_TM
    n_p = t_prompt // TM
    grid_spec = pltpu.PrefetchScalarGridSpec(
        num_scalar_prefetch=1,
        grid=(T // TM,),
        in_specs=[pl.BlockSpec((TM, D), lambda i, p: (i, 0)), pl.BlockSpec((TM, LANES), lambda i, p: (i, 0)),
                  pl.BlockSpec(memory_space=pl.ANY)],
        out_specs=[pl.BlockSpec((TM, D), lambda i, p: (jnp.minimum(i, n_p - 1), 0)),
                   pl.BlockSpec((TM, D), lambda i, p: (jnp.maximum(i - n_p, 0), 0))],
        scratch_shapes=[pltpu.VMEM((2, 2 * TM, D), F32), pltpu.SemaphoreType.DMA((2,))],
    )
    return pl.pallas_call(
        functools.partial(_combine_kernel, n_p=n_p),
        grid_spec=grid_spec,
        out_shape=[jax.ShapeDtypeStruct((t_prompt, D), F32), jax.ShapeDtypeStruct((T - t_prompt, D), F32)],
        compiler_params=_cparams(("arbitrary",)),
        name="combine",
    )(pos, h, gate, y_slots)


def _moe(h, xn, eid, gate, w_gate, w_up, w_down, t_prompt):
    T = h.shape[0]
    A = T * TOP_K
    BM = MOE_BM
    n_blocks = -(-(A + N_EXPERTS * (BM - 1)) // BM)
    e_flat = eid[:, :TOP_K].reshape(A)
    onehot = (e_flat[:, None] == jnp.arange(N_EXPERTS, dtype=jnp.int32)[None, :]).astype(jnp.int32)
    csum = jnp.cumsum(onehot, axis=0)
    counts = csum[-1]
    rank = jnp.take_along_axis(csum, e_flat[:, None], axis=1)[:, 0] - 1
    padded = (counts + BM - 1) // BM * BM
    pad_end = jnp.cumsum(padded)
    pad_start = pad_end - padded
    pos = (pad_start[e_flat] + rank).astype(jnp.int32)
    order = jnp.argsort(e_flat, stable=True)
    sorted_tok = jnp.pad((order // TOP_K).astype(jnp.int32), (0, BM))
    start = jnp.cumsum(counts) - counts
    block_first = jnp.arange(n_blocks, dtype=jnp.int32) * BM
    block_e = jnp.minimum(jnp.sum(pad_end[None, :] <= block_first[:, None], axis=1), N_EXPERTS - 1).astype(jnp.int32)
    block_src = jnp.minimum(start[block_e] + block_first - pad_start[block_e], A).astype(jnp.int32)
    n_used = (pad_end[-1] // BM).astype(jnp.int32).reshape(1)
    y_slots = _experts(block_e, n_used, block_src, sorted_tok, xn, w_gate, w_up, w_down)
    pos_tiles = pos.reshape(T // COMB_TM, COMB_TM, TOP_K).transpose(0, 2, 1).reshape(-1)
    return _combine(pos_tiles, h, gate, y_slots, t_prompt)


def _prep_weights(w_in, b_fox_f, w_router_group, b_router_group, w_router_expert, b_router_expert):
    D = w_in.shape[0]
    c = [0]
    for s in (FOX_W, FOX_W, FOX_W, FOX_HEADS, HGRN_W, HGRN_W, HGRN_W, HGRN_W, MEM_W):
        c.append(c[-1] + s)
    w_t = w_in.T
    w_a = w_t[c[0]:c[3]].astype(BF16)
    w_b = w_t[c[4]:c[8]].astype(BF16)
    w_q = w_t[c[8]:c[9]].astype(BF16)
    w_f = jnp.pad(w_t[c[3]:c[4]], ((0, LANES - FOX_HEADS), (0, 0))).astype(BF16)
    b_f_pad = jnp.zeros((1, LANES), F32).at[0, :FOX_HEADS].set(b_fox_f)
    n_r = N_GROUPS + N_EXPERTS
    w_r = jnp.zeros((D, LANES), F32).at[:, :N_GROUPS].set(w_router_group).at[:, N_GROUPS:n_r].set(w_router_expert)
    b_r = jnp.zeros((1, LANES), F32).at[0, :N_GROUPS].set(b_router_group).at[0, N_GROUPS:n_r].set(b_router_expert)
    wr_hi = w_r.astype(BF16)
    wr_lo = (w_r - wr_hi.astype(F32)).astype(BF16)
    wr2 = jnp.concatenate([wr_hi, wr_lo], axis=1)
    return (w_a, w_q, w_f, w_b), b_f_pad, wr2, b_r


def kernel(x_prompt, x_sample, cache_fox_k, cache_fox_v, cache_fox_logf, cache_mem_k, cache_mem_v, state_hgrn, page_table, mem_prompt, g_attn_norm, w_in, b_fox_f, g_fox_q, g_fox_k, lb_logits, g_hgrn_out, g_mem_norm, w_mem_kv, g_mem_q, g_mem_k, w_out, g_ffn_norm, w_router_group, b_router_group, w_router_expert, b_router_expert, w_gate_e, w_up_e, w_down_e):
    assert w_in.shape[0] == 1, "single-layer step"
    Bp, S, D = x_prompt.shape
    Bd, L, _ = x_sample.shape
    n_pool, page = cache_fox_k.shape[1], cache_fox_k.shape[2]
    assert page == LANES and L <= SUBLANES
    M = mem_prompt.shape[1]
    l = 0
    row = lambda a: a[l].reshape(1, -1)
    head_major = lambda a: jnp.swapaxes(a, -3, -2)

    w_seg, b_f_pad, wr2, b_r = _prep_weights(
        w_in[l], b_fox_f[l], w_router_group[l], b_router_group[l], w_router_expert[l], b_router_expert[l])
    w_out_bf = w_out[l].astype(BF16)
    w_mkv_bf = w_mem_kv[l].astype(BF16)
    experts_w = (w_gate_e[l], w_up_e[l], w_down_e[l])
    proj_args = (row(g_attn_norm), *w_seg, row(g_fox_q), row(g_fox_k), row(g_mem_q), lb_logits, b_f_pad)

    Tp = Bp * S
    xp = x_prompt.reshape(Tp, D)
    fq, fk_p, fkb, fv_p, fvb, hq, hlf, hv, hgs, mq, flf_p = _in_proj(xp, *proj_args, tm=512, seq=S)
    seq = lambda a: a.reshape(Bp, S, a.shape[-1])
    ck = _fox_cum(seq(flf_p))
    fox_o = _fox_attn(seq(fq), seq(fkb), seq(fvb), ck)
    hg_o, s_p = _hgrn(seq(hq), seq(hlf), seq(hv), seq(hgs), row(g_hgrn_out), None, C=128, n_heads=3, n_valid=128)
    mk, mv = _mem_kv(mem_prompt.reshape(Bp * M, D), row(g_mem_norm), w_mkv_bf, row(g_mem_k), tm=256)
    mem4 = lambda a, b: a.reshape(b, M, MEM_HEADS, HEAD_DIM)
    mem_o = _mem_attn(seq(mq), mem4(mk, Bp), mem4(mv, Bp), tq=512)
    prompt = (fox_o.reshape(Tp, FOX_W), hg_o.reshape(Tp, HGRN_W), mem_o.reshape(Tp, MEM_W), xp)

    R = SUBLANES
    Ts = Bd * R
    xs = jnp.pad(x_sample, ((0, 0), (0, R - L), (0, 0))).reshape(Ts, D)
    fq, fk_s, fkb, fv_s, fvb, hq, hlf, hv, hgs, mq, flf_s = _in_proj(xs, *proj_args, tm=Ts, seq=R)
    seqs = lambda a: a.reshape(Bd, R, a.shape[-1])
    lf_new_t = jnp.swapaxes(seqs(flf_s)[:, :, :HEAD_PAD], 1, 2)
    lf_new_t = jnp.pad(lf_new_t, ((0, 0), (0, 0), (0, LANES - R)))
    cache_lf_t = jnp.pad(jnp.swapaxes(cache_fox_logf[l].astype(F32), 1, 2), ((0, 0), (0, HEAD_PAD - FOX_HEADS), (0, 0)))
    fox_o = _fox_dec(page_table, seqs(fq), seqs(fkb), seqs(fvb), lf_new_t, head_major(cache_fox_k[l]), head_major(cache_fox_v[l]),
                     cache_lf_t, n_tok=L)
    hg_o, s_s = _hgrn(seqs(hq), seqs(hlf), seqs(hv), seqs(hgs), row(g_hgrn_out), state_hgrn[l],
                      C=R, n_heads=HGRN_HEADS, n_valid=L)
    mem_o = _mem_attn(seqs(mq), cache_mem_k[l], cache_mem_v[l], tq=R)
    sample = (fox_o.reshape(Ts, FOX_W), hg_o.reshape(Ts, HGRN_W), mem_o.reshape(Ts, MEM_W), xs)

    h, xn, eid, gate = _merge(prompt, sample, w_out_bf, row(g_ffn_norm), wr2, b_r, tm=Ts)
    y_p, y_s = _moe(h, xn, eid, gate, *experts_w, t_prompt=Tp)

    cut = lambda a: a.reshape((Bd, R) + a.shape[1:])[:, :L]
    return (y_p.reshape(Bp, S, D), cut(y_s),
            head_major(fk_p)[None], head_major(fv_p)[None],
            flf_p[:, :FOX_HEADS].reshape(1, Bp, S, FOX_HEADS), s_p[None],
            mem4(mk, Bp)[None], mem4(mv, Bp)[None],
            head_major(fk_s)[None, :, :L], head_major(fv_s)[None, :, :L], cut(flf_s)[:, :, :FOX_HEADS][None], s_s[None])
```

```python
import functools

import jax
import jax.numpy as jnp
from jax import lax
from jax.experimental import pallas as pl
from jax.experimental.pallas import tpu as pltpu

F32 = jnp.float32
BF16 = jnp.bfloat16

HEAD_DIM = 128
FOX_HEADS = 6
HGRN_HEADS = 6
MEM_HEADS = 4
FOX_W = FOX_HEADS * HEAD_DIM
HGRN_W = HGRN_HEADS * HEAD_DIM
MEM_W = MEM_HEADS * HEAD_DIM
N_GROUPS = 4
EXPERTS_PER_GROUP = 8
N_EXPERTS = N_GROUPS * EXPERTS_PER_GROUP
TOP_K = 2
EPS = 1e-6
ATTN_SCALE = HEAD_DIM ** -0.5
LANES = 128
SUBLANES = 8
SEG_W = FOX_W
PROJ_CHUNK = 256
HEAD_PAD = 8
MOE_BM = 256
MERGE_TM = 512
MERGE_SUB = 256
VMEM_LIMIT = 52 * 1024 * 1024

NT_DIMS = (((1,), (1,)), ((), ()))
TN_DIMS = (((0,), (0,)), ((), ()))


def _cparams(sem, vmem=VMEM_LIMIT):
    return pltpu.CompilerParams(dimension_semantics=sem, vmem_limit_bytes=vmem)


def _split3(x):
    hi = x.astype(BF16)
    r1 = x - hi.astype(F32)
    mid = r1.astype(BF16)
    lo = (r1 - mid.astype(F32)).astype(BF16)
    return hi, mid, lo


def _dot(a, b):
    return jnp.dot(a, b, preferred_element_type=F32)


def _dot_nt(a, b):
    return lax.dot_general(a, b, NT_DIMS, preferred_element_type=F32)


def _sigmoid(x):
    return 1.0 / (1.0 + jnp.exp(-x))


def _log_sigmoid(x):
    return jnp.minimum(x, 0.0) - jnp.log(1.0 + jnp.exp(-jnp.abs(x)))


def _rms_heads(a, g, n_heads, scale):
    outs = []
    for h in range(n_heads):
        ah = a[:, h * HEAD_DIM:(h + 1) * HEAD_DIM]
        ms = jnp.mean(ah * ah, axis=-1, keepdims=True)
        outs.append(ah * lax.rsqrt(ms + EPS) * (g * scale))
    return jnp.concatenate(outs, axis=-1)


def _store_heads(ref, a, n_heads):
    for h in range(n_heads):
        ref[:, h, :] = a[:, h * HEAD_DIM:(h + 1) * HEAD_DIM]


def _store_head_major(ref, a, h0):
    nb, _, rows, _ = ref.shape
    for h in range(a.shape[1] // HEAD_DIM):
        ref[:, h0 + h] = a[:, h * HEAD_DIM:(h + 1) * HEAD_DIM].reshape(nb, rows, HEAD_DIM)


def _static_when(cond):
    def deco(f):
        if cond:
            f()
        return f
    return deco


def _in_proj_kernel(*refs, names, segs):
    r = dict(zip(names, refs))
    xn_s = r["xn_s"]
    x = r["x"][...]
    ms = jnp.mean(x * x, axis=-1, keepdims=True)
    xn_s[...] = (x * lax.rsqrt(ms + EPS) * r["g_norm"][...]).astype(BF16)

    def segment(post, w_ref, row0, width=SEG_W):
        for c in range(width // PROJ_CHUNK):
            cs = slice(c * PROJ_CHUNK, (c + 1) * PROJ_CHUNK)
            post(c, cs, _dot_nt(xn_s[...], w_ref[row0 + c * PROJ_CHUNK:row0 + (c + 1) * PROJ_CHUNK, :]))

    heads_per_chunk = PROJ_CHUNK // HEAD_DIM

    @_static_when("fq" in segs)
    def _():
        def post(c, cs, acc):
            r["fq"][:, cs] = _rms_heads(acc, r["g_fq"][...], heads_per_chunk, ATTN_SCALE).astype(BF16)
        segment(post, r["w_a"], 0)

    @_static_when("fk" in segs)
    def _():
        def post(c, cs, acc):
            fk = _rms_heads(acc, r["g_fk"][...], heads_per_chunk, 1.0)
            _store_head_major(r["fk"], fk, c * heads_per_chunk)
            r["fkb"][:, cs] = fk.astype(BF16)
        segment(post, r["w_a"], SEG_W)

    @_static_when("fv" in segs)
    def _():
        def post(c, cs, acc):
            _store_head_major(r["fv"], acc, c * heads_per_chunk)
            r["fvb"][:, cs] = acc.astype(BF16)
        segment(post, r["w_a"], 2 * SEG_W)

    @_static_when("mq" in segs)
    def _():
        def post(c, cs, acc):
            r["mq"][:, cs] = _rms_heads(acc, r["g_mq"][...], heads_per_chunk, ATTN_SCALE).astype(BF16)
        segment(post, r["w_q"], 0, MEM_W)

    @_static_when("flf" in segs)
    def _():
        r["flf"][...] = _log_sigmoid(_dot_nt(xn_s[...], r["w_f"][...]) + r["b_f"][...])

    @_static_when("hq" in segs)
    def _():
        def post(c, cs, acc):
            r["hq"][:, cs] = acc * _sigmoid(acc) * ATTN_SCALE
        segment(post, r["w_b"], 0)

    @_static_when("hlf" in segs)
    def _():
        l = r["lb_logits"][...]
        mx = jnp.max(l, axis=0, keepdims=True)
        ex = jnp.exp(l - mx)
        lb = ex[0:1, :] / jnp.sum(ex, axis=0, keepdims=True)

        def post(c, cs, acc):
            r["hlf"][:, cs] = jnp.log(lb[:, cs] + (1.0 - lb[:, cs]) * _sigmoid(acc))
        segment(post, r["w_b"], SEG_W)

    @_static_when("hv" in segs)
    def _():
        def post(c, cs, acc):
            r["hv"][:, cs] = acc
        segment(post, r["w_b"], 2 * SEG_W)

    @_static_when("hgs" in segs)
    def _():
        def post(c, cs, acc):
            r["hgs"][:, cs] = acc * _sigmoid(acc)
        segment(post, r["w_b"], 3 * SEG_W)


def _in_proj_call(x, ins, outs, tm, seq, name):
    T, D = x.shape
    nb, rows = max(1, tm // seq), min(tm, seq)
    assert nb * rows == tm and seq % rows == 0
    tiles_per_seq = seq // rows
    resident = lambda a: pl.BlockSpec(a.shape, lambda i: (0,) * a.ndim, pipeline_mode=pl.Buffered(1))

    def out_spec(o):
        if len(o.shape) == 4:
            return pl.BlockSpec((nb, FOX_HEADS, rows, HEAD_DIM), lambda i: (i // tiles_per_seq, 0, i % tiles_per_seq, 0))
        return pl.BlockSpec((tm, o.shape[1]), lambda i: (i, 0))

    names = ("x",) + tuple(ins) + tuple(outs) + ("xn_s",)
    res = pl.pallas_call(
        functools.partial(_in_proj_kernel, names=names, segs=tuple(outs)),
        grid=(T // tm,),
        in_specs=[pl.BlockSpec((tm, D), lambda i: (i, 0))] + [resident(a) for a in ins.values()],
        out_specs=[out_spec(o) for o in outs.values()],
        out_shape=list(outs.values()),
        scratch_shapes=[pltpu.VMEM((tm, D), BF16)],
        compiler_params=_cparams(("arbitrary",)),
        name=name,
    )(x, *ins.values())
    return dict(zip(outs, res))


def _in_proj(x, g_norm, w_a, w_q, w_f, w_b, g_fq, g_fk, g_mq, lb_logits, b_f_pad, tm, seq):
    T = x.shape[0]
    kv_cache = jax.ShapeDtypeStruct((T // seq, FOX_HEADS, seq, HEAD_DIM), F32)
    rows = lambda w, dt: jax.ShapeDtypeStruct((T, w), dt)
    fox = _in_proj_call(
        x, dict(g_norm=g_norm, w_a=w_a, w_q=w_q, w_f=w_f, g_fq=g_fq, g_fk=g_fk, g_mq=g_mq, b_f=b_f_pad),
        dict(fq=rows(FOX_W, BF16),
             fk=kv_cache,
             fkb=rows(FOX_W, BF16),
             fv=kv_cache, fvb=rows(FOX_W, BF16),
             mq=rows(MEM_W, BF16),
             flf=rows(LANES, F32)),
        tm, seq, "in_proj_fox")
    hg = _in_proj_call(
        x, dict(g_norm=g_norm, w_b=w_b, lb_logits=lb_logits),
        dict(hq=rows(HGRN_W, F32),
             hlf=rows(HGRN_W, F32),
             hv=rows(HGRN_W, F32),
             hgs=rows(HGRN_W, F32)),
        tm, seq, "in_proj_hgrn")
    return (fox["fq"], fox["fk"], fox["fkb"], fox["fv"], fox["fvb"], hg["hq"], hg["hlf"], hg["hv"], hg["hgs"],
            fox["mq"], fox["flf"])


CUM_C = 256


def _fox_cum_kernel(lf_ref, ck_ref):
    S = lf_ref.shape[1]
    r = lax.broadcasted_iota(jnp.int32, (CUM_C, CUM_C), 0)
    c = lax.broadcasted_iota(jnp.int32, (CUM_C, CUM_C), 1)
    tri = jnp.where(c <= r, 1.0, 0.0).astype(BF16)
    carry = jnp.zeros((1, LANES), F32)
    for i in range(S // CUM_C):
        sl = slice(i * CUM_C, (i + 1) * CUM_C)
        hi, mid, lo = _split3(lf_ref[0, sl, :])
        cum = (_dot(tri, hi) + _dot(tri, mid)) + _dot(tri, lo) + carry
        carry = cum[CUM_C - 1:CUM_C, :]
        ck_ref[0, :, sl] = cum.T[0:HEAD_PAD, :]


def _fox_cum(flf):
    B, S, _ = flf.shape
    return pl.pallas_call(
        _fox_cum_kernel,
        grid=(B,),
        in_specs=[pl.BlockSpec((1, S, LANES), lambda b: (b, 0, 0))],
        out_specs=pl.BlockSpec((1, HEAD_PAD, S), lambda b: (b, 0, 0)),
        out_shape=jax.ShapeDtypeStruct((B, HEAD_PAD, S), F32),
        compiler_params=_cparams(("arbitrary",)),
        name="fox_cum",
    )(flf)


FOX_T = 256


def _softmax_step(s, v_bf, m, l, acc):
    m_new = jnp.maximum(m, jnp.max(s, axis=-1, keepdims=True))
    alpha = jnp.exp(m - m_new)
    p = jnp.exp(s - m_new)
    l = alpha * l + jnp.sum(p, axis=-1, keepdims=True)
    acc = alpha * acc + _dot(p.astype(BF16), v_bf)
    return m_new, l, acc


def _fox_attn_kernel(q_ref, k_ref, v_ref, ck_ref, o_ref, m_s, l_s, acc_s):
    qi = pl.program_id(1)
    T = FOX_T
    m_s[...] = jnp.full_like(m_s, -jnp.inf)
    l_s[...] = jnp.zeros_like(l_s)
    acc_s[...] = jnp.zeros_like(acc_s)
    row = lax.broadcasted_iota(jnp.int32, (T, T), 0)
    col = lax.broadcasted_iota(jnp.int32, (T, T), 1)
    causal = col <= row

    def tile(ks, masked):
        for h in range(FOX_HEADS):
            hs = slice(h * HEAD_DIM, (h + 1) * HEAD_DIM)
            s = _dot_nt(q_ref[0, :, hs], k_ref[0, pl.ds(ks, T), hs]) - ck_ref[0, h:h + 1, pl.ds(ks, T)]
            if masked:
                s = jnp.where(causal, s, -jnp.inf)
            m_old = m_s[:, hs]
            m_new = jnp.maximum(m_old, jnp.max(s, axis=-1, keepdims=True))
            alpha = jnp.exp(m_old - m_new)
            p = jnp.exp(s - jnp.concatenate([m_new] * (T // HEAD_DIM), axis=-1))
            m_s[:, hs] = m_new
            l_s[:, hs] = alpha * l_s[:, hs] + jnp.sum(p, axis=-1, keepdims=True)
            acc_s[:, hs] = alpha * acc_s[:, hs] + _dot(p.astype(BF16), v_ref[0, pl.ds(ks, T), hs])

    def body(kt, c):
        tile(pl.multiple_of(kt * T, T), False)
        return c

    lax.fori_loop(0, qi, body, 0)
    tile(pl.multiple_of(qi * T, T), True)
    o_ref[0] = (acc_s[...] / l_s[...]).astype(o_ref.dtype)


def _fox_attn(fq, fk, fv, ck):
    B, S, _ = fq.shape
    T = FOX_T
    return pl.pallas_call(
        _fox_attn_kernel,
        grid=(B, S // T),
        in_specs=[pl.BlockSpec((1, T, FOX_W), lambda b, i: (b, i, 0)),
                  pl.BlockSpec((1, S, FOX_W), lambda b, i: (b, 0, 0)),
                  pl.BlockSpec((1, S, FOX_W), lambda b, i: (b, 0, 0)),
                  pl.BlockSpec((1, HEAD_PAD, S), lambda b, i: (b, 0, 0))],
        out_specs=pl.BlockSpec((1, T, FOX_W), lambda b, i: (b, i, 0)),
        out_shape=jax.ShapeDtypeStruct((B, S, FOX_W), BF16),
        scratch_shapes=[pltpu.VMEM((T, FOX_W), F32)] * 3,
        compiler_params=_cparams(("arbitrary", "arbitrary")),
        name="fox_attn",
    )(fq, fk, fv, ck)


def _level_ref(b, m):
    C = b.shape[0]
    if 2 * m >= SUBLANES:
        b3 = b.reshape(C // (2 * m), 2 * m, HEAD_DIM)
        r = jnp.broadcast_to(b3[:, m - 1:m, :], b3.shape)
        return r.reshape(C, HEAD_DIM)
    b3 = b.reshape(C // SUBLANES, SUBLANES, HEAD_DIM)
    sub = lax.broadcasted_iota(jnp.int32, b3.shape, 1)
    pick = lambda i: jnp.broadcast_to(b3[:, i:i + 1, :], b3.shape)
    if m == 2:
        r = jnp.where(sub < 4, pick(1), pick(5))
    else:
        r = jnp.where(sub < 2, pick(0), jnp.where(sub < 4, pick(2), jnp.where(sub < 6, pick(4), pick(6))))
    return r.reshape(C, HEAD_DIM)


def _level_index(C):
    t = lax.broadcasted_iota(jnp.int32, (C, C), 0)
    s = lax.broadcasted_iota(jnp.int32, (C, C), 1)
    x = jnp.bitwise_xor(t, s)
    lvl = jnp.full((C, C), -1, jnp.int32)
    j, m = 0, 1
    while m < C:
        lvl = jnp.where(x >= m, j, lvl)
        j, m = j + 1, 2 * m
    return jnp.where(t > s, lvl, -1)


def _neg_abs(x):
    return pltpu.bitcast(jnp.bitwise_or(pltpu.bitcast(x, jnp.uint32), jnp.uint32(0x80000000)), F32)


def _hgrn_chunk(q, g, v, st, tri, lvl, n_valid):
    C = q.shape[0]
    hi, mid, lo = _split3(g)
    b = (_dot(tri, hi) + _dot(tri, mid)) + _dot(tri, lo)
    k = 1.0 - jnp.exp(g)
    a = jnp.zeros((C, C), F32)
    j, m = 0, 1
    while m < C:
        e = jnp.exp(_neg_abs(b - _level_ref(b, m)))
        a_l = _dot_nt((q * e).astype(BF16), (k * e).astype(BF16))
        a = jnp.where(lvl == j, a_l, a)
        j, m = j + 1, 2 * m
    v_bf = v.astype(BF16)
    diag = jnp.sum(q * k, axis=-1, keepdims=True)
    o = _dot_nt((q * jnp.exp(b)).astype(BF16), st.astype(BF16)) + _dot(a.astype(BF16), v_bf) + diag * v
    b_last = b[n_valid - 1:n_valid, :]
    kt = k * jnp.exp(jnp.minimum(b_last - b, 0.0))
    if n_valid < C:
        rows = lax.broadcasted_iota(jnp.int32, (C, HEAD_DIM), 0)
        kt = jnp.where(rows < n_valid, kt, 0.0)
    st_new = st * jnp.exp(b_last) + lax.dot_general(v_bf, kt.astype(BF16), TN_DIMS, preferred_element_type=F32)
    return o, st_new


def _hgrn_kernel(*refs, C, n_chunks, n_heads, n_valid, has_s0):
    if has_s0:
        q_ref, g_ref, v_ref, gs_ref, gn_ref, s0_ref, o_ref, sf_ref = refs
    else:
        q_ref, g_ref, v_ref, gs_ref, gn_ref, o_ref, sf_ref = refs
    r = lax.broadcasted_iota(jnp.int32, (C, C), 0)
    c = lax.broadcasted_iota(jnp.int32, (C, C), 1)
    tri = jnp.where(c <= r, 1.0, 0.0).astype(BF16)
    lvl = _level_index(C)
    gn = gn_ref[...]
    sts0 = tuple(s0_ref[0, h].T if has_s0 else jnp.zeros((HEAD_DIM, HEAD_DIM), F32) for h in range(n_heads))

    def body(ci, sts):
        rs = pl.ds(pl.multiple_of(ci * C, C), C)
        out = []
        for h in range(n_heads):
            hs = slice(h * HEAD_DIM, (h + 1) * HEAD_DIM)
            o, st = _hgrn_chunk(q_ref[0, rs, hs], g_ref[0, rs, hs], v_ref[0, rs, hs], sts[h], tri, lvl, n_valid)
            ms = jnp.mean(o * o, axis=-1, keepdims=True)
            o_ref[0, rs, hs] = (o * lax.rsqrt(ms + EPS) * gn * gs_ref[0, rs, hs]).astype(o_ref.dtype)
            out.append(st)
        return tuple(out)

    sts = lax.fori_loop(0, n_chunks, body, sts0) if n_chunks > 1 else body(0, sts0)
    for h in range(n_heads):
        sf_ref[0, h] = sts[h].T


def _hgrn(hq, hlf, hv, hgs, g_hn, s0, C, n_heads, n_valid):
    B, L, _ = hq.shape
    hp = HGRN_HEADS // n_heads
    w = n_heads * HEAD_DIM
    seq = pl.BlockSpec((1, L, w), lambda b, h: (b, 0, h))
    st_spec = pl.BlockSpec((1, n_heads, HEAD_DIM, HEAD_DIM), lambda b, h: (b, h, 0, 0))
    in_specs = [seq, seq, seq, seq, pl.BlockSpec((1, HEAD_DIM), lambda b, h: (0, 0))]
    args = [hq, hlf, hv, hgs, g_hn]
    if s0 is not None:
        in_specs.append(st_spec)
        args.append(s0)
    kern = functools.partial(_hgrn_kernel, C=C, n_chunks=L // C, n_heads=n_heads, n_valid=n_valid,
                             has_s0=s0 is not None)
    return pl.pallas_call(
        kern,
        grid=(B, hp),
        in_specs=in_specs,
        out_specs=[seq, st_spec],
        out_shape=[jax.ShapeDtypeStruct((B, L, HGRN_W), BF16),
                   jax.ShapeDtypeStruct((B, HGRN_HEADS, HEAD_DIM, HEAD_DIM), F32)],
        compiler_params=_cparams(("arbitrary", "arbitrary")),
        name="hgrn",
    )(*args)


def _mem_kv_kernel(x_ref, gn_ref, w_ref, gk_ref, mk_ref, mv_ref, xn_s):
    j = pl.program_id(1)

    @pl.when(j == 0)
    def _():
        x = x_ref[...]
        ms = jnp.mean(x * x, axis=-1, keepdims=True)
        xn_s[...] = (x * lax.rsqrt(ms + EPS) * gn_ref[...]).astype(BF16)

    acc = _dot(xn_s[...], w_ref[...])

    @pl.when(j == 0)
    def _():
        _store_heads(mk_ref, _rms_heads(acc, gk_ref[...], MEM_HEADS, 1.0), MEM_HEADS)

    @pl.when(j == 1)
    def _():
        _store_heads(mv_ref, acc, MEM_HEADS)


def _mem_kv(mem, g_norm, w, g_mk, tm):
    T, D = mem.shape
    out = jax.ShapeDtypeStruct((T, MEM_HEADS, HEAD_DIM), F32)
    return pl.pallas_call(
        _mem_kv_kernel,
        grid=(T // tm, 2),
        in_specs=[pl.BlockSpec((tm, D), lambda i, j: (i, 0)), pl.BlockSpec((1, D), lambda i, j: (0, 0)),
                  pl.BlockSpec((D, MEM_W), lambda i, j: (0, j)), pl.BlockSpec((1, HEAD_DIM), lambda i, j: (0, 0))],
        out_specs=[pl.BlockSpec((tm, MEM_HEADS, HEAD_DIM), lambda i, j: (i, 0, 0))] * 2,
        out_shape=[out, out],
        scratch_shapes=[pltpu.VMEM((tm, D), BF16)],
        compiler_params=_cparams(("arbitrary", "arbitrary")),
        name="mem_kv",
    )(mem, g_norm, w, g_mk)


def _mem_attn_kernel(q_ref, k_ref, v_ref, o_ref):
    for b in range(q_ref.shape[0]):
        for h in range(MEM_HEADS):
            hs = slice(h * HEAD_DIM, (h + 1) * HEAD_DIM)
            s = _dot_nt(q_ref[b, :, hs], k_ref[b, :, h, :].astype(BF16))
            p = jnp.exp(s - jnp.max(s, axis=-1, keepdims=True))
            l = jnp.sum(p, axis=-1, keepdims=True)
            o_ref[b, :, hs] = (_dot(p.astype(BF16), v_ref[b, :, h, :].astype(BF16)) / l).astype(o_ref.dtype)


def _mem_attn(mq, mk, mv, tq, nb=1):
    B, L, _ = mq.shape
    M = mk.shape[1]
    kv = pl.BlockSpec((nb, M, MEM_HEADS, HEAD_DIM), lambda b, i: (b, 0, 0, 0))
    qo = pl.BlockSpec((nb, tq, MEM_W), lambda b, i: (b, i, 0))
    return pl.pallas_call(
        _mem_attn_kernel,
        grid=(B // nb, L // tq),
        in_specs=[qo, kv, kv],
        out_specs=qo,
        out_shape=jax.ShapeDtypeStruct((B, L, MEM_W), BF16),
        compiler_params=_cparams(("arbitrary", "arbitrary")),
        name="mem_attn",
    )(mq, mk, mv)


DEC_G = 8
DEC_ROWS = 4 * HEAD_PAD


def _suffix_sum_lanes(x):
    lane = lax.broadcasted_iota(jnp.int32, x.shape, 1)
    s = 1
    while s < LANES:
        x = x + jnp.where(lane + s < LANES, pltpu.roll(x, LANES - s, 1), 0.0)
        s *= 2
    return x


def _prefix_sum_lanes(x):
    lane = lax.broadcasted_iota(jnp.int32, x.shape, 1)
    s = 1
    while s < LANES:
        x = x + jnp.where(lane >= s, pltpu.roll(x, s, 1), 0.0)
        s *= 2
    return x


def _fox_dec_kernel(pt_ref, q_ref, kn_ref, vn_ref, lfn_ref, *refs, n_tok):
    G = DEC_G
    k_refs, v_refs, lf_refs = refs[0:G], refs[G:2 * G], refs[2 * G:3 * G]
    o_ref = refs[3 * G]
    qbd, kn_s, vn_s, m_s, l_s, acc_s, car_s = refs[3 * G + 1:]
    st = pl.program_id(1)

    def attend(k_bf, v_bf, bias):
        s = _dot_nt(qbd[...].astype(BF16), k_bf) + bias
        m_new, l_new, acc_new = _softmax_step(s, v_bf, m_s[...], l_s[...], acc_s[...])
        m_s[...] = m_new
        l_s[...] = l_new
        acc_s[...] = acc_new

    @pl.when(st == 0)
    def _():
        qbd[...] = jnp.zeros_like(qbd)
        q = q_ref[0].astype(F32)
        for t in range(n_tok):
            for h in range(FOX_HEADS):
                hs = slice(h * HEAD_DIM, (h + 1) * HEAD_DIM)
                qbd[t * HEAD_PAD + h:t * HEAD_PAD + h + 1, hs] = q[t:t + 1, hs]
        kn_s[...] = jnp.zeros_like(kn_s)
        vn_s[...] = jnp.zeros_like(vn_s)
        kn_s[0:SUBLANES, :] = kn_ref[0].astype(F32)
        vn_s[0:SUBLANES, :] = vn_ref[0].astype(F32)
        m_s[...] = jnp.full_like(m_s, -jnp.inf)
        l_s[...] = jnp.zeros_like(l_s)
        acc_s[...] = jnp.zeros_like(acc_s)
        car_s[...] = jnp.zeros_like(car_s)
        ecum = _prefix_sum_lanes(lfn_ref[0])
        lane = lax.broadcasted_iota(jnp.int32, (HEAD_PAD, LANES), 1)
        bias = jnp.concatenate([jnp.where(lane <= t, -ecum, -jnp.inf) for t in range(n_tok)], axis=0)
        attend(kn_s[...].astype(BF16), vn_s[...].astype(BF16), bias)

    heads = lambda ref: jnp.concatenate([ref[0, h] for h in range(FOX_HEADS)], axis=-1).astype(BF16)
    carry = car_s[...]
    ds = []
    for i in range(G):
        lf = lf_refs[i][0]
        incl = _suffix_sum_lanes(lf)
        ds.append(carry + (incl - lf))
        carry = carry + incl[:, 0:1]
    car_s[...] = carry
    d_all = jnp.concatenate(ds, axis=-1)
    attend(jnp.concatenate([heads(r) for r in k_refs], axis=0),
           jnp.concatenate([heads(r) for r in v_refs], axis=0),
           jnp.concatenate([d_all] * n_tok, axis=0))

    @pl.when(st == pl.num_programs(1) - 1)
    def _():
        res = acc_s[...] / l_s[...]
        o_ref[...] = jnp.zeros_like(o_ref)
        for t in range(n_tok):
            for h in range(FOX_HEADS):
                hs = slice(h * HEAD_DIM, (h + 1) * HEAD_DIM)
                r = t * HEAD_PAD + h
                o_ref[0, t:t + 1, hs] = res[r:r + 1, hs].astype(o_ref.dtype)


def _fox_dec(page_table, fq, fk, fv, lf_new_t, cache_k, cache_v, cache_lf_t, n_tok):
    B, n_pages = page_table.shape
    G = DEC_G
    assert n_pages % G == 0 and n_tok * HEAD_PAD == DEC_ROWS
    n_steps = n_pages // G
    tok = lambda w: pl.BlockSpec((1, SUBLANES, w), lambda b, s, pt: (b, 0, 0))

    def page_spec(shape, i):
        return pl.BlockSpec((1,) + shape,
                            lambda b, s, pt: (pt[b * n_pages + (n_pages - 1 - (s * G + i))],) + (0,) * len(shape))

    kv_page = (FOX_HEADS, LANES, HEAD_DIM)
    in_specs = ([tok(FOX_W), tok(FOX_W), tok(FOX_W), tok(LANES)]
                + [page_spec(kv_page, i) for i in range(G)]
                + [page_spec(kv_page, i) for i in range(G)]
                + [page_spec((HEAD_PAD, LANES), i) for i in range(G)])
    grid_spec = pltpu.PrefetchScalarGridSpec(
        num_scalar_prefetch=1,
        grid=(B, n_steps),
        in_specs=in_specs,
        out_specs=pl.BlockSpec((1, SUBLANES, FOX_W), lambda b, s, pt: (b, 0, 0)),
        scratch_shapes=[pltpu.VMEM((DEC_ROWS, FOX_W), F32),
                        pltpu.VMEM((LANES, FOX_W), F32), pltpu.VMEM((LANES, FOX_W), F32),
                        pltpu.VMEM((DEC_ROWS, 1), F32), pltpu.VMEM((DEC_ROWS, 1), F32),
                        pltpu.VMEM((DEC_ROWS, FOX_W), F32), pltpu.VMEM((HEAD_PAD, 1), F32)],
    )
    return pl.pallas_call(
        functools.partial(_fox_dec_kernel, n_tok=n_tok),
        grid_spec=grid_spec,
        out_shape=jax.ShapeDtypeStruct((B, SUBLANES, FOX_W), BF16),
        compiler_params=_cparams(("arbitrary", "arbitrary")),
        name="fox_dec",
    )(page_table.reshape(-1), fq, fk, fv, lf_new_t, *([cache_k] * G), *([cache_v] * G), *([cache_lf_t] * G))


def _merge_kernel(fo_p, ho_p, mo_p, x_p, fo_s, ho_s, mo_s, x_s, w_ref, gf_ref, wr2_ref, br_ref,
                  h_ref, xn_ref, eid_ref, gate_ref, cat_s, x_sc, *, n_p):
    i = pl.program_id(0)

    def stage(fo_ref, ho_ref, mo_ref, x_ref):
        cat_s[:, 0:FOX_W] = fo_ref[...]
        cat_s[:, FOX_W:FOX_W + HGRN_W] = ho_ref[...]
        cat_s[:, FOX_W + HGRN_W:] = mo_ref[...]
        x_sc[...] = x_ref[...]

    @pl.when(i < n_p)
    def _():
        stage(fo_p, ho_p, mo_p, x_p)

    @pl.when(i >= n_p)
    def _():
        stage(fo_s, ho_s, mo_s, x_s)

    for u in range(x_sc.shape[0] // MERGE_SUB):
        rows = slice(u * MERGE_SUB, (u + 1) * MERGE_SUB)
        _merge_rows(rows, cat_s, x_sc, w_ref, gf_ref, wr2_ref, br_ref, h_ref, xn_ref, eid_ref, gate_ref)


def _merge_rows(rows, cat_s, x_sc, w_ref, gf_ref, wr2_ref, br_ref, h_ref, xn_ref, eid_ref, gate_ref):
    D = x_sc.shape[1]
    ssq = jnp.zeros((MERGE_SUB, 1), F32)
    for c in range(D // PROJ_CHUNK):
        cs = slice(c * PROJ_CHUNK, (c + 1) * PROJ_CHUNK)
        hc = x_sc[rows, cs] + _dot(cat_s[rows, :], w_ref[:, cs])
        h_ref[rows, cs] = hc
        ssq = ssq + jnp.sum(hc * hc, axis=-1, keepdims=True)
    xn = h_ref[rows, :] * lax.rsqrt(ssq * (1.0 / D) + EPS) * gf_ref[...]
    xn_ref[rows, :] = xn
    x_hi = xn.astype(BF16)
    x_lo = (xn - x_hi.astype(F32)).astype(BF16)
    hi2 = _dot(x_hi, wr2_ref[...])
    logits = (hi2[:, :LANES] + (hi2[:, LANES:] + _dot(x_lo, wr2_ref[:, :LANES]))) + br_ref[...]
    lane = lax.broadcasted_iota(jnp.int32, logits.shape, 1)
    big = jnp.int32(LANES)
    ninf = -jnp.inf
    gl = jnp.where(lane < N_GROUPS, logits, ninf)
    gmax = jnp.max(gl, axis=-1, keepdims=True)
    g_sel = jnp.min(jnp.where(gl == gmax, lane, big), axis=-1, keepdims=True)
    g_prob = 1.0 / jnp.sum(jnp.exp(gl - gmax), axis=-1, keepdims=True)
    lo = N_GROUPS + EXPERTS_PER_GROUP * g_sel
    el = jnp.where((lane >= lo) & (lane < lo + EXPERTS_PER_GROUP), logits, ninf)
    v1 = jnp.max(el, axis=-1, keepdims=True)
    i1 = jnp.min(jnp.where(el == v1, lane, big), axis=-1, keepdims=True)
    el2 = jnp.where(lane == i1, ninf, el)
    v2 = jnp.max(el2, axis=-1, keepdims=True)
    i2 = jnp.min(jnp.where(el2 == v2, lane, big), axis=-1, keepdims=True)
    t = jnp.exp(v2 - v1)
    w1 = g_prob / (1.0 + t)
    w2 = g_prob * t / (1.0 + t)
    eid_ref[rows, :] = jnp.where(lane == 0, i1 - N_GROUPS, jnp.where(lane == 1, i2 - N_GROUPS, 0))
    gate_ref[rows, :] = jnp.where(lane == 0, w1, jnp.where(lane == 1, w2, 0.0))


def _merge(prompt, sample, w_out, g_ffn, wr2, b_r, tm):
    Tp, D = prompt[3].shape
    Ts = sample[3].shape[0]
    assert Tp % tm == 0 and Ts % tm == 0 and tm % MERGE_SUB == 0
    n_p, n_s = Tp // tm, Ts // tm
    T = Tp + Ts
    once = pl.Buffered(1)
    p_row = lambda w: pl.BlockSpec((tm, w), lambda i: (jnp.minimum(i, n_p - 1), 0))
    s_row = lambda w: pl.BlockSpec((tm, w), lambda i: (jnp.maximum(i - n_p, 0), 0), pipeline_mode=once)
    row = lambda w: pl.BlockSpec((tm, w), lambda i: (i, 0))
    full = lambda a: pl.BlockSpec(a.shape, lambda i: (0,) * a.ndim, pipeline_mode=once)
    widths = (FOX_W, HGRN_W, MEM_W, D)
    return pl.pallas_call(
        functools.partial(_merge_kernel, n_p=n_p),
        grid=(n_p + n_s,),
        in_specs=[p_row(w) for w in widths] + [s_row(w) for w in widths]
                 + [full(w_out), full(g_ffn), full(wr2), full(b_r)],
        out_specs=[row(D), row(D), row(LANES), row(LANES)],
        out_shape=[jax.ShapeDtypeStruct((T, D), F32), jax.ShapeDtypeStruct((T, D), F32),
                   jax.ShapeDtypeStruct((T, LANES), jnp.int32), jax.ShapeDtypeStruct((T, LANES), F32)],
        scratch_shapes=[pltpu.VMEM((tm, FOX_W + HGRN_W + MEM_W), BF16), pltpu.VMEM((tm, D), F32)],
        compiler_params=_cparams(("arbitrary",)),
        name="merge",
    )(*prompt, *sample, w_out, g_ffn, wr2, b_r)


def _experts_kernel(be_ref, nb_ref, src_ref, tok_ref, x_hbm, wg_ref, wu_ref, wd_ref, y_ref,
                    xbuf, wg_s, wu_s, wd_s, sem):
    b = pl.program_id(0)

    @pl.when((b == 0) | (be_ref[b] != be_ref[jnp.maximum(b - 1, 0)]))
    def _():
        wg_s[...] = wg_ref[0].astype(BF16)
        wu_s[...] = wu_ref[0].astype(BF16)
        wd_s[...] = wd_ref[0].astype(BF16)

    n_used = nb_ref[0]
    BM = MOE_BM

    def gather(blk, slot):
        base = src_ref[blk]
        for r in range(BM):
            pltpu.make_async_copy(x_hbm.at[pl.ds(tok_ref[base + r], 1)], xbuf.at[slot, pl.ds(r, 1)],
                                  sem.at[slot]).start()

    def wait(slot):
        pltpu.make_async_copy(x_hbm.at[pl.ds(0, BM)], xbuf.at[slot], sem.at[slot]).wait()

    slot = lax.rem(b, 2)

    @pl.when(b == 0)
    def _():
        gather(0, 0)

    for s in range(2):
        @pl.when((b + 1 < n_used) & (slot == s))
        def _():
            gather(b + 1, 1 - s)

    @pl.when(b < n_used)
    def _():
        wait(slot)
        x = xbuf[slot].astype(BF16)
        hmid = _dot(x, wg_s[...])
        hmid = hmid * _sigmoid(hmid) * _dot(x, wu_s[...])
        y_ref[...] = _dot(hmid.astype(BF16), wd_s[...])

    @pl.when(b >= n_used)
    def _():
        y_ref[...] = jnp.zeros_like(y_ref)


def _experts(block_e, n_used, block_src, sorted_tok, xn, w_gate, w_up, w_down):
    n_blocks = block_e.shape[0]
    D, FF = w_gate.shape[1:]
    BM = MOE_BM
    grid_spec = pltpu.PrefetchScalarGridSpec(
        num_scalar_prefetch=4,
        grid=(n_blocks,),
        in_specs=[pl.BlockSpec(memory_space=pl.ANY),
                  pl.BlockSpec((1, D, FF), lambda b, be, *_: (be[b], 0, 0)),
                  pl.BlockSpec((1, D, FF), lambda b, be, *_: (be[b], 0, 0)),
                  pl.BlockSpec((1, FF, D), lambda b, be, *_: (be[b], 0, 0))],
        out_specs=pl.BlockSpec((BM, D), lambda b, *_: (b, 0)),
        scratch_shapes=[pltpu.VMEM((2, BM, D), F32), pltpu.VMEM((D, FF), BF16), pltpu.VMEM((D, FF), BF16),
                        pltpu.VMEM((FF, D), BF16), pltpu.SemaphoreType.DMA((2,))],
    )
    return pl.pallas_call(
        _experts_kernel,
        grid_spec=grid_spec,
        out_shape=jax.ShapeDtypeStruct((n_blocks * BM, D), F32),
        compiler_params=_cparams(("arbitrary",)),
        name="experts",
    )(block_e, n_used, block_src, sorted_tok, xn, w_gate, w_up, w_down)


COMB_TM = 128


def _combine_kernel(pos_ref, h_ref, gate_ref, y_hbm, op_ref, os_ref, ybuf, sem, *, n_p):
    i = pl.program_id(0)
    n = pl.num_programs(0)
    TM = COMB_TM

    def gather(blk, slot):
        base = blk * (2 * TM)
        for r in range(2 * TM):
            pltpu.make_async_copy(y_hbm.at[pl.ds(pos_ref[base + r], 1)], ybuf.at[slot, pl.ds(r, 1)],
                                  sem.at[slot]).start()

    def wait(slot):
        pltpu.make_async_copy(y_hbm.at[pl.ds(0, 2 * TM)], ybuf.at[slot], sem.at[slot]).wait()

    slot = lax.rem(i, 2)

    @pl.when(i == 0)
    def _():
        gather(0, 0)

    for s in range(2):
        @pl.when((i + 1 < n) & (slot == s))
        def _():
            gather(i + 1, 1 - s)

    wait(slot)
    g = gate_ref[...]
    res = h_ref[...] + (g[:, 0:1] * ybuf[slot, 0:TM, :] + g[:, 1:2] * ybuf[slot, TM:2 * TM, :])

    @pl.when(i < n_p)
    def _():
        op_ref[...] = res

    @pl.when(i >= n_p)
    def _():
        os_ref[...] = res


def _combine(pos, h, gate, y_slots, t_prompt):
    T, D = h.shape
    TM = COMB_TM
    n_p = t_prompt // TM
    grid_spec = pltpu.PrefetchScalarGridSpec(
        num_scalar_prefetch=1,
        grid=(T // TM,),
        in_specs=[pl.BlockSpec((TM, D), lambda i, p: (i, 0)), pl.BlockSpec((TM, LANES), lambda i, p: (i, 0)),
                  pl.BlockSpec(memory_space=pl.ANY)],
        out_specs=[pl.BlockSpec((TM, D), lambda i, p: (jnp.minimum(i, n_p - 1), 0)),
                   pl.BlockSpec((TM, D), lambda i, p: (jnp.maximum(i - n_p, 0), 0))],
        scratch_shapes=[pltpu.VMEM((2, 2 * TM, D), F32), pltpu.SemaphoreType.DMA((2,))],
    )
    return pl.pallas_call(
        functools.partial(_combine_kernel, n_p=n_p),
        grid_spec=grid_spec,
        out_shape=[jax.ShapeDtypeStruct((t_prompt, D), F32), jax.ShapeDtypeStruct((T - t_prompt, D), F32)],
        compiler_params=_cparams(("arbitrary",)),
        name="combine",
    )(pos, h, gate, y_slots)


def _moe(h, xn, eid, gate, w_gate, w_up, w_down, t_prompt):
    T = h.shape[0]
    A = T * TOP_K
    BM = MOE_BM
    n_blocks = -(-(A + N_EXPERTS * (BM - 1)) // BM)
    e_flat = eid[:, :TOP_K].reshape(A)
    onehot = (e_flat[:, None] == jnp.arange(N_EXPERTS, dtype=jnp.int32)[None, :]).astype(jnp.int32)
    csum = jnp.cumsum(onehot, axis=0)
    counts = csum[-1]
    rank = jnp.take_along_axis(csum, e_flat[:, None], axis=1)[:, 0] - 1
    padded = (counts + BM - 1) // BM * BM
    pad_end = jnp.cumsum(padded)
    pad_start = pad_end - padded
    pos = (pad_start[e_flat] + rank).astype(jnp.int32)
    order = jnp.argsort(e_flat, stable=True)
    sorted_tok = jnp.pad((order // TOP_K).astype(jnp.int32), (0, BM))
    start = jnp.cumsum(counts) - counts
    block_first = jnp.arange(n_blocks, dtype=jnp.int32) * BM
    block_e = jnp.minimum(jnp.sum(pad_end[None, :] <= block_first[:, None], axis=1), N_EXPERTS - 1).astype(jnp.int32)
    block_src = jnp.minimum(start[block_e] + block_first - pad_start[block_e], A).astype(jnp.int32)
    n_used = (pad_end[-1] // BM).astype(jnp.int32).reshape(1)
    y_slots = _experts(block_e, n_used, block_src, sorted_tok, xn, w_gate, w_up, w_down)
    pos_tiles = pos.reshape(T // COMB_TM, COMB_TM, TOP_K).transpose(0, 2, 1).reshape(-1)
    return _combine(pos_tiles, h, gate, y_slots, t_prompt)


def _prep_weights(w_in, b_fox_f, w_router_group, b_router_group, w_router_expert, b_router_expert):
    D = w_in.shape[0]
    c = [0]
    for s in (FOX_W, FOX_W, FOX_W, FOX_HEADS, HGRN_W, HGRN_W, HGRN_W, HGRN_W, MEM_W):
        c.append(c[-1] + s)
    w_t = w_in.T
    w_a = w_t[c[0]:c[3]].astype(BF16)
    w_b = w_t[c[4]:c[8]].astype(BF16)
    w_q = w_t[c[8]:c[9]].astype(BF16)
    w_f = jnp.pad(w_t[c[3]:c[4]], ((0, LANES - FOX_HEADS), (0, 0))).astype(BF16)
    b_f_pad = jnp.zeros((1, LANES), F32).at[0, :FOX_HEADS].set(b_fox_f)
    n_r = N_GROUPS + N_EXPERTS
    w_r = jnp.zeros((D, LANES), F32).at[:, :N_GROUPS].set(w_router_group).at[:, N_GROUPS:n_r].set(w_router_expert)
    b_r = jnp.zeros((1, LANES), F32).at[0, :N_GROUPS].set(b_router_group).at[0, N_GROUPS:n_r].set(b_router_expert)
    wr_hi = w_r.astype(BF16)
    wr_lo = (w_r - wr_hi.astype(F32)).astype(BF16)
    wr2 = jnp.concatenate([wr_hi, wr_lo], axis=1)
    return (w_a, w_q, w_f, w_b), b_f_pad, wr2, b_r


def kernel(x_prompt, x_sample, cache_fox_k, cache_fox_v, cache_fox_logf, cache_mem_k, cache_mem_v, state_hgrn, page_table, mem_prompt, g_attn_norm, w_in, b_fox_f, g_fox_q, g_fox_k, lb_logits, g_hgrn_out, g_mem_norm, w_mem_kv, g_mem_q, g_mem_k, w_out, g_ffn_norm, w_router_group, b_router_group, w_router_expert, b_router_expert, w_gate_e, w_up_e, w_down_e):
    assert w_in.shape[0] == 1, "single-layer step"
    Bp, S, D = x_prompt.shape
    Bd, L, _ = x_sample.shape
    n_pool, page = cache_fox_k.shape[1], cache_fox_k.shape[2]
    assert page == LANES and L <= SUBLANES
    M = mem_prompt.shape[1]
    l = 0
    row = lambda a: a[l].reshape(1, -1)
    head_major = lambda a: jnp.swapaxes(a, -3, -2)

    w_seg, b_f_pad, wr2, b_r = _prep_weights(
        w_in[l], b_fox_f[l], w_router_group[l], b_router_group[l], w_router_expert[l], b_router_expert[l])
    w_out_bf = w_out[l].astype(BF16)
    w_mkv_bf = w_mem_kv[l].astype(BF16)
    experts_w = (w_gate_e[l], w_up_e[l], w_down_e[l])
    proj_args = (row(g_attn_norm), *w_seg, row(g_fox_q), row(g_fox_k), row(g_mem_q), lb_logits, b_f_pad)

    Tp = Bp * S
    xp = x_prompt.reshape(Tp, D)
    fq, fk_p, fkb, fv_p, fvb, hq, hlf, hv, hgs, mq, flf_p = _in_proj(xp, *proj_args, tm=512, seq=S)
    seq = lambda a: a.reshape(Bp, S, a.shape[-1])
    ck = _fox_cum(seq(flf_p))
    fox_o = _fox_attn(seq(fq), seq(fkb), seq(fvb), ck)
    hg_o, s_p = _hgrn(seq(hq), seq(hlf), seq(hv), seq(hgs), row(g_hgrn_out), None, C=128, n_heads=3, n_valid=128)
    mk, mv = _mem_kv(mem_prompt.reshape(Bp * M, D), row(g_mem_norm), w_mkv_bf, row(g_mem_k), tm=256)
    mem4 = lambda a, b: a.reshape(b, M, MEM_HEADS, HEAD_DIM)
    mem_o = _mem_attn(seq(mq), mem4(mk, Bp), mem4(mv, Bp), tq=512)
    prompt = (fox_o.reshape(Tp, FOX_W), hg_o.reshape(Tp, HGRN_W), mem_o.reshape(Tp, MEM_W), xp)

    R = SUBLANES
    Ts = Bd * R
    xs = jnp.pad(x_sample, ((0, 0), (0, R - L), (0, 0))).reshape(Ts, D)
    fq, fk_s, fkb, fv_s, fvb, hq, hlf, hv, hgs, mq, flf_s = _in_proj(xs, *proj_args, tm=Ts, seq=R)
    seqs = lambda a: a.reshape(Bd, R, a.shape[-1])
    lf_new_t = jnp.swapaxes(seqs(flf_s)[:, :, :HEAD_PAD], 1, 2)
    lf_new_t = jnp.pad(lf_new_t, ((0, 0), (0, 0), (0, LANES - R)))
    cache_lf_t = jnp.pad(jnp.swapaxes(cache_fox_logf[l].astype(F32), 1, 2), ((0, 0), (0, HEAD_PAD - FOX_HEADS), (0, 0)))
    fox_o = _fox_dec(page_table, seqs(fq), seqs(fkb), seqs(fvb), lf_new_t, head_major(cache_fox_k[l]), head_major(cache_fox_v[l]),
                     cache_lf_t, n_tok=L)
    hg_o, s_s = _hgrn(seqs(hq), seqs(hlf), seqs(hv), seqs(hgs), row(g_hgrn_out), state_hgrn[l],
                      C=R, n_heads=HGRN_HEADS, n_valid=L)
    mem_o = _mem_attn(seqs(mq), cache_mem_k[l], cache_mem_v[l], tq=R, nb=4 if Bd % 4 == 0 else 1)
    sample = (fox_o.reshape(Ts, FOX_W), hg_o.reshape(Ts, HGRN_W), mem_o.reshape(Ts, MEM_W), xs)

    sample = tuple(jnp.pad(a, ((0, -Ts % MERGE_TM), (0, 0))) for a in sample)
    h, xn, eid, gate = _merge(prompt, sample, w_out_bf, row(g_ffn_norm), wr2, b_r, tm=MERGE_TM)
    y_p, y_s = _moe(h, xn, eid, gate, *experts_w, t_prompt=Tp)

    cut = lambda a: a.reshape((Bd, R) + a.shape[1:])[:, :L]
    return (y_p.reshape(Bp, S, D), cut(y_s[:Ts]),
            head_major(fk_p)[None], head_major(fv_p)[None],
            flf_p[:, :FOX_HEADS].reshape(1, Bp, S, FOX_HEADS), s_p[None],
            mem4(mk, Bp)[None], mem4(mv, Bp)[None],
            head_major(fk_s)[None, :, :L], head_major(fv_s)[None, :, :L], cut(flf_s)[:, :, :FOX_HEADS][None], s_s[None])
```

```python
import functools

import jax
import jax.numpy as jnp
from jax import lax
from jax.experimental import pallas as pl
from jax.experimental.pallas import tpu as pltpu

F32 = jnp.float32
BF16 = jnp.bfloat16

HEAD_DIM = 128
FOX_HEADS = 6
HGRN_HEADS = 6
MEM_HEADS = 4
FOX_W = FOX_HEADS * HEAD_DIM
HGRN_W = HGRN_HEADS * HEAD_DIM
MEM_W = MEM_HEADS * HEAD_DIM
N_GROUPS = 4
EXPERTS_PER_GROUP = 8
N_EXPERTS = N_GROUPS * EXPERTS_PER_GROUP
TOP_K = 2
EPS = 1e-6
ATTN_SCALE = HEAD_DIM ** -0.5
LANES = 128
SUBLANES = 8
SEG_W = FOX_W
PROJ_CHUNK = 256
HEAD_PAD = 8
MOE_BM = 256
MERGE_TM = 512
MERGE_SUB = 256
VMEM_LIMIT = 52 * 1024 * 1024

NT_DIMS = (((1,), (1,)), ((), ()))
TN_DIMS = (((0,), (0,)), ((), ()))


def _cparams(sem, vmem=VMEM_LIMIT):
    return pltpu.CompilerParams(dimension_semantics=sem, vmem_limit_bytes=vmem)


def _split3(x):
    hi = x.astype(BF16)
    r1 = x - hi.astype(F32)
    mid = r1.astype(BF16)
    lo = (r1 - mid.astype(F32)).astype(BF16)
    return hi, mid, lo


def _dot(a, b):
    return jnp.dot(a, b, preferred_element_type=F32)


def _dot_nt(a, b):
    return lax.dot_general(a, b, NT_DIMS, preferred_element_type=F32)


def _sigmoid(x):
    return 1.0 / (1.0 + jnp.exp(-x))


def _log_sigmoid(x):
    return jnp.minimum(x, 0.0) - jnp.log(1.0 + jnp.exp(-jnp.abs(x)))


def _rms_heads(a, g, n_heads, scale):
    outs = []
    for h in range(n_heads):
        ah = a[:, h * HEAD_DIM:(h + 1) * HEAD_DIM]
        ms = jnp.mean(ah * ah, axis=-1, keepdims=True)
        outs.append(ah * lax.rsqrt(ms + EPS) * (g * scale))
    return jnp.concatenate(outs, axis=-1)


def _store_heads(ref, a, n_heads):
    for h in range(n_heads):
        ref[:, h, :] = a[:, h * HEAD_DIM:(h + 1) * HEAD_DIM]


def _store_head_major(ref, a, h0):
    nb, _, rows, _ = ref.shape
    for h in range(a.shape[1] // HEAD_DIM):
        ref[:, h0 + h] = a[:, h * HEAD_DIM:(h + 1) * HEAD_DIM].reshape(nb, rows, HEAD_DIM)


def _static_when(cond):
    def deco(f):
        if cond:
            f()
        return f
    return deco


def _in_proj_kernel(*refs, names, segs):
    r = dict(zip(names, refs))
    xn_s = r["xn_s"]
    x = r["x"][...]
    ms = jnp.mean(x * x, axis=-1, keepdims=True)
    xn_s[...] = (x * lax.rsqrt(ms + EPS) * r["g_norm"][...]).astype(BF16)

    def segment(post, w_ref, row0, width=SEG_W):
        for c in range(width // PROJ_CHUNK):
            cs = slice(c * PROJ_CHUNK, (c + 1) * PROJ_CHUNK)
            post(c, cs, _dot_nt(xn_s[...], w_ref[row0 + c * PROJ_CHUNK:row0 + (c + 1) * PROJ_CHUNK, :]))

    heads_per_chunk = PROJ_CHUNK // HEAD_DIM

    @_static_when("fq" in segs)
    def _():
        def post(c, cs, acc):
            r["fq"][:, cs] = _rms_heads(acc, r["g_fq"][...], heads_per_chunk, ATTN_SCALE).astype(BF16)
        segment(post, r["w_a"], 0)

    @_static_when("fk" in segs)
    def _():
        def post(c, cs, acc):
            fk = _rms_heads(acc, r["g_fk"][...], heads_per_chunk, 1.0)
            _store_head_major(r["fk"], fk, c * heads_per_chunk)
            r["fkb"][:, cs] = fk.astype(BF16)
        segment(post, r["w_a"], SEG_W)

    @_static_when("fv" in segs)
    def _():
        def post(c, cs, acc):
            _store_head_major(r["fv"], acc, c * heads_per_chunk)
            r["fvb"][:, cs] = acc.astype(BF16)
        segment(post, r["w_a"], 2 * SEG_W)

    @_static_when("mq" in segs)
    def _():
        def post(c, cs, acc):
            r["mq"][:, cs] = _rms_heads(acc, r["g_mq"][...], heads_per_chunk, ATTN_SCALE).astype(BF16)
        segment(post, r["w_q"], 0, MEM_W)

    @_static_when("flf" in segs)
    def _():
        r["flf"][...] = _log_sigmoid(_dot_nt(xn_s[...], r["w_f"][...]) + r["b_f"][...])

    @_static_when("hq" in segs)
    def _():
        def post(c, cs, acc):
            r["hq"][:, cs] = acc * _sigmoid(acc) * ATTN_SCALE
        segment(post, r["w_b"], 0)

    @_static_when("hlf" in segs)
    def _():
        l = r["lb_logits"][...]
        mx = jnp.max(l, axis=0, keepdims=True)
        ex = jnp.exp(l - mx)
        lb = ex[0:1, :] / jnp.sum(ex, axis=0, keepdims=True)

        def post(c, cs, acc):
            r["hlf"][:, cs] = jnp.log(lb[:, cs] + (1.0 - lb[:, cs]) * _sigmoid(acc))
        segment(post, r["w_b"], SEG_W)

    @_static_when("hv" in segs)
    def _():
        def post(c, cs, acc):
            r["hv"][:, cs] = acc
        segment(post, r["w_b"], 2 * SEG_W)

    @_static_when("hgs" in segs)
    def _():
        def post(c, cs, acc):
            r["hgs"][:, cs] = acc * _sigmoid(acc)
        segment(post, r["w_b"], 3 * SEG_W)


def _in_proj_call(x, ins, outs, tm, seq, name):
    T, D = x.shape
    nb, rows = max(1, tm // seq), min(tm, seq)
    assert nb * rows == tm and seq % rows == 0
    tiles_per_seq = seq // rows
    resident = lambda a: pl.BlockSpec(a.shape, lambda i: (0,) * a.ndim, pipeline_mode=pl.Buffered(1))

    def out_spec(o):
        if len(o.shape) == 4:
            return pl.BlockSpec((nb, FOX_HEADS, rows, HEAD_DIM), lambda i: (i // tiles_per_seq, 0, i % tiles_per_seq, 0))
        return pl.BlockSpec((tm, o.shape[1]), lambda i: (i, 0))

    names = ("x",) + tuple(ins) + tuple(outs) + ("xn_s",)
    res = pl.pallas_call(
        functools.partial(_in_proj_kernel, names=names, segs=tuple(outs)),
        grid=(T // tm,),
        in_specs=[pl.BlockSpec((tm, D), lambda i: (i, 0))] + [resident(a) for a in ins.values()],
        out_specs=[out_spec(o) for o in outs.values()],
        out_shape=list(outs.values()),
        scratch_shapes=[pltpu.VMEM((tm, D), BF16)],
        compiler_params=_cparams(("arbitrary",)),
        name=name,
    )(x, *ins.values())
    return dict(zip(outs, res))


def _in_proj(x, g_norm, w_a, w_q, w_f, w_b, g_fq, g_fk, g_mq, lb_logits, b_f_pad, tm, seq):
    T = x.shape[0]
    kv_cache = jax.ShapeDtypeStruct((T // seq, FOX_HEADS, seq, HEAD_DIM), F32)
    rows = lambda w, dt: jax.ShapeDtypeStruct((T, w), dt)
    fox = _in_proj_call(
        x, dict(g_norm=g_norm, w_a=w_a, w_q=w_q, w_f=w_f, g_fq=g_fq, g_fk=g_fk, g_mq=g_mq, b_f=b_f_pad),
        dict(fq=rows(FOX_W, BF16),
             fk=kv_cache,
             fkb=rows(FOX_W, BF16),
             fv=kv_cache, fvb=rows(FOX_W, BF16),
             mq=rows(MEM_W, BF16),
             flf=rows(LANES, F32)),
        tm, seq, "in_proj_fox")
    hg = _in_proj_call(
        x, dict(g_norm=g_norm, w_b=w_b, lb_logits=lb_logits),
        dict(hq=rows(HGRN_W, F32),
             hlf=rows(HGRN_W, F32),
             hv=rows(HGRN_W, F32),
             hgs=rows(HGRN_W, F32)),
        tm, seq, "in_proj_hgrn")
    return (fox["fq"], fox["fk"], fox["fkb"], fox["fv"], fox["fvb"], hg["hq"], hg["hlf"], hg["hv"], hg["hgs"],
            fox["mq"], fox["flf"])


CUM_C = 256


def _fox_cum_kernel(lf_ref, ck_ref):
    S = lf_ref.shape[1]
    r = lax.broadcasted_iota(jnp.int32, (CUM_C, CUM_C), 0)
    c = lax.broadcasted_iota(jnp.int32, (CUM_C, CUM_C), 1)
    tri = jnp.where(c <= r, 1.0, 0.0).astype(BF16)
    carry = jnp.zeros((1, LANES), F32)
    for i in range(S // CUM_C):
        sl = slice(i * CUM_C, (i + 1) * CUM_C)
        hi, mid, lo = _split3(lf_ref[0, sl, :])
        cum = (_dot(tri, hi) + _dot(tri, mid)) + _dot(tri, lo) + carry
        carry = cum[CUM_C - 1:CUM_C, :]
        ck_ref[0, :, sl] = cum.T[0:HEAD_PAD, :]


def _fox_cum(flf):
    B, S, _ = flf.shape
    return pl.pallas_call(
        _fox_cum_kernel,
        grid=(B,),
        in_specs=[pl.BlockSpec((1, S, LANES), lambda b: (b, 0, 0))],
        out_specs=pl.BlockSpec((1, HEAD_PAD, S), lambda b: (b, 0, 0)),
        out_shape=jax.ShapeDtypeStruct((B, HEAD_PAD, S), F32),
        compiler_params=_cparams(("arbitrary",)),
        name="fox_cum",
    )(flf)


FOX_T = 256


def _softmax_step(s, v_bf, m, l, acc):
    m_new = jnp.maximum(m, jnp.max(s, axis=-1, keepdims=True))
    alpha = jnp.exp(m - m_new)
    p = jnp.exp(s - m_new)
    l = alpha * l + jnp.sum(p, axis=-1, keepdims=True)
    acc = alpha * acc + _dot(p.astype(BF16), v_bf)
    return m_new, l, acc


def _fox_attn_kernel(q_ref, k_ref, v_ref, ck_ref, o_ref, m_s, l_s, acc_s):
    qi = pl.program_id(1)
    T = FOX_T
    m_s[...] = jnp.full_like(m_s, -jnp.inf)
    l_s[...] = jnp.zeros_like(l_s)
    acc_s[...] = jnp.zeros_like(acc_s)
    row = lax.broadcasted_iota(jnp.int32, (T, T), 0)
    col = lax.broadcasted_iota(jnp.int32, (T, T), 1)
    causal = col <= row

    def tile(ks, width, masked=False):
        for h in range(FOX_HEADS):
            hs = slice(h * HEAD_DIM, (h + 1) * HEAD_DIM)
            s = _dot_nt(q_ref[0, :, hs], k_ref[0, pl.ds(ks, width), hs]) - ck_ref[0, h:h + 1, pl.ds(ks, width)]
            if masked:
                s = jnp.where(causal, s, -jnp.inf)
            m_old = m_s[:, hs]
            m_new = jnp.maximum(m_old, jnp.max(s, axis=-1, keepdims=True))
            alpha = jnp.exp(m_old - m_new)
            p = jnp.exp(s - jnp.concatenate([m_new] * (width // HEAD_DIM), axis=-1))
            m_s[:, hs] = m_new
            l_s[:, hs] = alpha * l_s[:, hs] + jnp.sum(p, axis=-1, keepdims=True)
            acc_s[:, hs] = alpha * acc_s[:, hs] + _dot(p.astype(BF16), v_ref[0, pl.ds(ks, width), hs])

    def body(kt, c):
        tile(pl.multiple_of(kt * (2 * T), 2 * T), 2 * T)
        return c

    lax.fori_loop(0, qi // 2, body, 0)

    @pl.when(qi % 2 == 1)
    def _():
        tile(pl.multiple_of((qi - 1) * T, T), T)

    tile(pl.multiple_of(qi * T, T), T, masked=True)
    o_ref[0] = (acc_s[...] / l_s[...]).astype(o_ref.dtype)


def _fox_attn(fq, fk, fv, ck):
    B, S, _ = fq.shape
    T = FOX_T
    return pl.pallas_call(
        _fox_attn_kernel,
        grid=(B, S // T),
        in_specs=[pl.BlockSpec((1, T, FOX_W), lambda b, i: (b, i, 0)),
                  pl.BlockSpec((1, S, FOX_W), lambda b, i: (b, 0, 0)),
                  pl.BlockSpec((1, S, FOX_W), lambda b, i: (b, 0, 0)),
                  pl.BlockSpec((1, HEAD_PAD, S), lambda b, i: (b, 0, 0))],
        out_specs=pl.BlockSpec((1, T, FOX_W), lambda b, i: (b, i, 0)),
        out_shape=jax.ShapeDtypeStruct((B, S, FOX_W), BF16),
        scratch_shapes=[pltpu.VMEM((T, FOX_W), F32)] * 3,
        compiler_params=_cparams(("arbitrary", "arbitrary")),
        name="fox_attn",
    )(fq, fk, fv, ck)


def _level_ref(b, m):
    C = b.shape[0]
    if 2 * m >= SUBLANES:
        b3 = b.reshape(C // (2 * m), 2 * m, HEAD_DIM)
        r = jnp.broadcast_to(b3[:, m - 1:m, :], b3.shape)
        return r.reshape(C, HEAD_DIM)
    b3 = b.reshape(C // SUBLANES, SUBLANES, HEAD_DIM)
    sub = lax.broadcasted_iota(jnp.int32, b3.shape, 1)
    pick = lambda i: jnp.broadcast_to(b3[:, i:i + 1, :], b3.shape)
    if m == 2:
        r = jnp.where(sub < 4, pick(1), pick(5))
    else:
        r = jnp.where(sub < 2, pick(0), jnp.where(sub < 4, pick(2), jnp.where(sub < 6, pick(4), pick(6))))
    return r.reshape(C, HEAD_DIM)


def _level_index(C):
    t = lax.broadcasted_iota(jnp.int32, (C, C), 0)
    s = lax.broadcasted_iota(jnp.int32, (C, C), 1)
    x = jnp.bitwise_xor(t, s)
    lvl = jnp.full((C, C), -1, jnp.int32)
    j, m = 0, 1
    while m < C:
        lvl = jnp.where(x >= m, j, lvl)
        j, m = j + 1, 2 * m
    return jnp.where(t > s, lvl, -1)


def _neg_abs(x):
    return pltpu.bitcast(jnp.bitwise_or(pltpu.bitcast(x, jnp.uint32), jnp.uint32(0x80000000)), F32)


def _hgrn_chunk(q, g, v, st, tri, lvl, n_valid):
    C = q.shape[0]
    hi, mid, lo = _split3(g)
    b = (_dot(tri, hi) + _dot(tri, mid)) + _dot(tri, lo)
    k = 1.0 - jnp.exp(g)
    a = jnp.zeros((C, C), F32)
    j, m = 0, 1
    while m < C:
        e = jnp.exp(_neg_abs(b - _level_ref(b, m)))
        a_l = _dot_nt((q * e).astype(BF16), (k * e).astype(BF16))
        a = jnp.where(lvl == j, a_l, a)
        j, m = j + 1, 2 * m
    v_bf = v.astype(BF16)
    diag = jnp.sum(q * k, axis=-1, keepdims=True)
    o = _dot_nt((q * jnp.exp(b)).astype(BF16), st.astype(BF16)) + _dot(a.astype(BF16), v_bf) + diag * v
    b_last = b[n_valid - 1:n_valid, :]
    kt = k * jnp.exp(jnp.minimum(b_last - b, 0.0))
    if n_valid < C:
        rows = lax.broadcasted_iota(jnp.int32, (C, HEAD_DIM), 0)
        kt = jnp.where(rows < n_valid, kt, 0.0)
    st_new = st * jnp.exp(b_last) + lax.dot_general(v_bf, kt.astype(BF16), TN_DIMS, preferred_element_type=F32)
    return o, st_new


def _hgrn_kernel(*refs, C, n_chunks, n_heads, n_valid, has_s0):
    if has_s0:
        q_ref, g_ref, v_ref, gs_ref, gn_ref, s0_ref, o_ref, sf_ref = refs
    else:
        q_ref, g_ref, v_ref, gs_ref, gn_ref, o_ref, sf_ref = refs
    r = lax.broadcasted_iota(jnp.int32, (C, C), 0)
    c = lax.broadcasted_iota(jnp.int32, (C, C), 1)
    tri = jnp.where(c <= r, 1.0, 0.0).astype(BF16)
    lvl = _level_index(C)
    gn = gn_ref[...]
    sts0 = tuple(s0_ref[0, h].T if has_s0 else jnp.zeros((HEAD_DIM, HEAD_DIM), F32) for h in range(n_heads))

    def body(ci, sts):
        rs = pl.ds(pl.multiple_of(ci * C, C), C)
        out = []
        for h in range(n_heads):
            hs = slice(h * HEAD_DIM, (h + 1) * HEAD_DIM)
            o, st = _hgrn_chunk(q_ref[0, rs, hs], g_ref[0, rs, hs], v_ref[0, rs, hs], sts[h], tri, lvl, n_valid)
            ms = jnp.mean(o * o, axis=-1, keepdims=True)
            o_ref[0, rs, hs] = (o * lax.rsqrt(ms + EPS) * gn * gs_ref[0, rs, hs]).astype(o_ref.dtype)
            out.append(st)
        return tuple(out)

    sts = lax.fori_loop(0, n_chunks, body, sts0) if n_chunks > 1 else body(0, sts0)
    for h in range(n_heads):
        sf_ref[0, h] = sts[h].T


def _hgrn(hq, hlf, hv, hgs, g_hn, s0, C, n_heads, n_valid):
    B, L, _ = hq.shape
    hp = HGRN_HEADS // n_heads
    w = n_heads * HEAD_DIM
    seq = pl.BlockSpec((1, L, w), lambda b, h: (b, 0, h))
    st_spec = pl.BlockSpec((1, n_heads, HEAD_DIM, HEAD_DIM), lambda b, h: (b, h, 0, 0))
    in_specs = [seq, seq, seq, seq, pl.BlockSpec((1, HEAD_DIM), lambda b, h: (0, 0))]
    args = [hq, hlf, hv, hgs, g_hn]
    if s0 is not None:
        in_specs.append(st_spec)
        args.append(s0)
    kern = functools.partial(_hgrn_kernel, C=C, n_chunks=L // C, n_heads=n_heads, n_valid=n_valid,
                             has_s0=s0 is not None)
    return pl.pallas_call(
        kern,
        grid=(B, hp),
        in_specs=in_specs,
        out_specs=[seq, st_spec],
        out_shape=[jax.ShapeDtypeStruct((B, L, HGRN_W), BF16),
                   jax.ShapeDtypeStruct((B, HGRN_HEADS, HEAD_DIM, HEAD_DIM), F32)],
        compiler_params=_cparams(("arbitrary", "arbitrary")),
        name="hgrn",
    )(*args)


def _mem_kv_kernel(x_ref, gn_ref, w_ref, gk_ref, mk_ref, mv_ref, xn_s):
    j = pl.program_id(1)

    @pl.when(j == 0)
    def _():
        x = x_ref[...]
        ms = jnp.mean(x * x, axis=-1, keepdims=True)
        xn_s[...] = (x * lax.rsqrt(ms + EPS) * gn_ref[...]).astype(BF16)

    acc = _dot(xn_s[...], w_ref[...])

    @pl.when(j == 0)
    def _():
        _store_heads(mk_ref, _rms_heads(acc, gk_ref[...], MEM_HEADS, 1.0), MEM_HEADS)

    @pl.when(j == 1)
    def _():
        _store_heads(mv_ref, acc, MEM_HEADS)


def _mem_kv(mem, g_norm, w, g_mk, tm):
    T, D = mem.shape
    out = jax.ShapeDtypeStruct((T, MEM_HEADS, HEAD_DIM), F32)
    return pl.pallas_call(
        _mem_kv_kernel,
        grid=(T // tm, 2),
        in_specs=[pl.BlockSpec((tm, D), lambda i, j: (i, 0)), pl.BlockSpec((1, D), lambda i, j: (0, 0)),
                  pl.BlockSpec((D, MEM_W), lambda i, j: (0, j)), pl.BlockSpec((1, HEAD_DIM), lambda i, j: (0, 0))],
        out_specs=[pl.BlockSpec((tm, MEM_HEADS, HEAD_DIM), lambda i, j: (i, 0, 0))] * 2,
        out_shape=[out, out],
        scratch_shapes=[pltpu.VMEM((tm, D), BF16)],
        compiler_params=_cparams(("arbitrary", "arbitrary")),
        name="mem_kv",
    )(mem, g_norm, w, g_mk)


def _mem_attn_kernel(q_ref, k_ref, v_ref, o_ref):
    for b in range(q_ref.shape[0]):
        for h in range(MEM_HEADS):
            hs = slice(h * HEAD_DIM, (h + 1) * HEAD_DIM)
            s = _dot_nt(q_ref[b, :, hs], k_ref[b, :, h, :].astype(BF16))
            p = jnp.exp(s - jnp.max(s, axis=-1, keepdims=True))
            l = jnp.sum(p, axis=-1, keepdims=True)
            o_ref[b, :, hs] = (_dot(p.astype(BF16), v_ref[b, :, h, :].astype(BF16)) / l).astype(o_ref.dtype)


def _mem_attn(mq, mk, mv, tq, nb=1):
    B, L, _ = mq.shape
    M = mk.shape[1]
    kv = pl.BlockSpec((nb, M, MEM_HEADS, HEAD_DIM), lambda b, i: (b, 0, 0, 0))
    qo = pl.BlockSpec((nb, tq, MEM_W), lambda b, i: (b, i, 0))
    return pl.pallas_call(
        _mem_attn_kernel,
        grid=(B // nb, L // tq),
        in_specs=[qo, kv, kv],
        out_specs=qo,
        out_shape=jax.ShapeDtypeStruct((B, L, MEM_W), BF16),
        compiler_params=_cparams(("arbitrary", "arbitrary")),
        name="mem_attn",
    )(mq, mk, mv)


DEC_G = 8
DEC_ROWS = 4 * HEAD_PAD


def _suffix_sum_lanes(x):
    lane = lax.broadcasted_iota(jnp.int32, x.shape, 1)
    s = 1
    while s < LANES:
        x = x + jnp.where(lane + s < LANES, pltpu.roll(x, LANES - s, 1), 0.0)
        s *= 2
    return x


def _prefix_sum_lanes(x):
    lane = lax.broadcasted_iota(jnp.int32, x.shape, 1)
    s = 1
    while s < LANES:
        x = x + jnp.where(lane >= s, pltpu.roll(x, s, 1), 0.0)
        s *= 2
    return x


def _fox_dec_kernel(pt_ref, q_ref, kn_ref, vn_ref, lfn_ref, *refs, n_tok):
    G = DEC_G
    k_refs, v_refs, lf_refs = refs[0:G], refs[G:2 * G], refs[2 * G:3 * G]
    o_ref = refs[3 * G]
    qbd, kn_s, vn_s, m_s, l_s, acc_s, car_s = refs[3 * G + 1:]
    st = pl.program_id(1)

    def attend(k_bf, v_bf, bias):
        s = _dot_nt(qbd[...].astype(BF16), k_bf) + bias
        m_new, l_new, acc_new = _softmax_step(s, v_bf, m_s[...], l_s[...], acc_s[...])
        m_s[...] = m_new
        l_s[...] = l_new
        acc_s[...] = acc_new

    @pl.when(st == 0)
    def _():
        qbd[...] = jnp.zeros_like(qbd)
        q = q_ref[0].astype(F32)
        for t in range(n_tok):
            for h in range(FOX_HEADS):
                hs = slice(h * HEAD_DIM, (h + 1) * HEAD_DIM)
                qbd[t * HEAD_PAD + h:t * HEAD_PAD + h + 1, hs] = q[t:t + 1, hs]
        kn_s[...] = jnp.zeros_like(kn_s)
        vn_s[...] = jnp.zeros_like(vn_s)
        kn_s[0:SUBLANES, :] = kn_ref[0].astype(F32)
        vn_s[0:SUBLANES, :] = vn_ref[0].astype(F32)
        m_s[...] = jnp.full_like(m_s, -jnp.inf)
        l_s[...] = jnp.zeros_like(l_s)
        acc_s[...] = jnp.zeros_like(acc_s)
        car_s[...] = jnp.zeros_like(car_s)
        ecum = _prefix_sum_lanes(lfn_ref[0])
        lane = lax.broadcasted_iota(jnp.int32, (HEAD_PAD, LANES), 1)
        bias = jnp.concatenate([jnp.where(lane <= t, -ecum, -jnp.inf) for t in range(n_tok)], axis=0)
        attend(kn_s[...].astype(BF16), vn_s[...].astype(BF16), bias)

    heads = lambda ref: jnp.concatenate([ref[0, h] for h in range(FOX_HEADS)], axis=-1).astype(BF16)
    carry = car_s[...]
    ds = []
    for i in range(G):
        lf = lf_refs[i][0]
        incl = _suffix_sum_lanes(lf)
        ds.append(carry + (incl - lf))
        carry = carry + incl[:, 0:1]
    car_s[...] = carry
    d_all = jnp.concatenate(ds, axis=-1)
    attend(jnp.concatenate([heads(r) for r in k_refs], axis=0),
           jnp.concatenate([heads(r) for r in v_refs], axis=0),
           jnp.concatenate([d_all] * n_tok, axis=0))

    @pl.when(st == pl.num_programs(1) - 1)
    def _():
        res = acc_s[...] / l_s[...]
        o_ref[...] = jnp.zeros_like(o_ref)
        for t in range(n_tok):
            for h in range(FOX_HEADS):
                hs = slice(h * HEAD_DIM, (h + 1) * HEAD_DIM)
                r = t * HEAD_PAD + h
                o_ref[0, t:t + 1, hs] = res[r:r + 1, hs].astype(o_ref.dtype)


def _fox_dec(page_table, fq, fk, fv, lf_new_t, cache_k, cache_v, cache_lf_t, n_tok):
    B, n_pages = page_table.shape
    G = DEC_G
    assert n_pages % G == 0 and n_tok * HEAD_PAD == DEC_ROWS
    n_steps = n_pages // G
    tok = lambda w: pl.BlockSpec((1, SUBLANES, w), lambda b, s, pt: (b, 0, 0))

    def page_spec(shape, i):
        return pl.BlockSpec((1,) + shape,
                            lambda b, s, pt: (pt[b * n_pages + (n_pages - 1 - (s * G + i))],) + (0,) * len(shape))

    kv_page = (FOX_HEADS, LANES, HEAD_DIM)
    in_specs = ([tok(FOX_W), tok(FOX_W), tok(FOX_W), tok(LANES)]
                + [page_spec(kv_page, i) for i in range(G)]
                + [page_spec(kv_page, i) for i in range(G)]
                + [page_spec((HEAD_PAD, LANES), i) for i in range(G)])
    grid_spec = pltpu.PrefetchScalarGridSpec(
        num_scalar_prefetch=1,
        grid=(B, n_steps),
        in_specs=in_specs,
        out_specs=pl.BlockSpec((1, SUBLANES, FOX_W), lambda b, s, pt: (b, 0, 0)),
        scratch_shapes=[pltpu.VMEM((DEC_ROWS, FOX_W), F32),
                        pltpu.VMEM((LANES, FOX_W), F32), pltpu.VMEM((LANES, FOX_W), F32),
                        pltpu.VMEM((DEC_ROWS, 1), F32), pltpu.VMEM((DEC_ROWS, 1), F32),
                        pltpu.VMEM((DEC_ROWS, FOX_W), F32), pltpu.VMEM((HEAD_PAD, 1), F32)],
    )
    return pl.pallas_call(
        functools.partial(_fox_dec_kernel, n_tok=n_tok),
        grid_spec=grid_spec,
        out_shape=jax.ShapeDtypeStruct((B, SUBLANES, FOX_W), BF16),
        compiler_params=_cparams(("arbitrary", "arbitrary")),
        name="fox_dec",
    )(page_table.reshape(-1), fq, fk, fv, lf_new_t, *([cache_k] * G), *([cache_v] * G), *([cache_lf_t] * G))


def _merge_kernel(fo_p, ho_p, mo_p, x_p, fo_s, ho_s, mo_s, x_s, w_ref, gf_ref, wr2_ref, br_ref,
                  h_ref, xn_ref, eid_ref, gate_ref, cat_s, x_sc, *, n_p):
    i = pl.program_id(0)

    def stage(fo_ref, ho_ref, mo_ref, x_ref):
        cat_s[:, 0:FOX_W] = fo_ref[...]
        cat_s[:, FOX_W:FOX_W + HGRN_W] = ho_ref[...]
        cat_s[:, FOX_W + HGRN_W:] = mo_ref[...]
        x_sc[...] = x_ref[...]

    @pl.when(i < n_p)
    def _():
        stage(fo_p, ho_p, mo_p, x_p)

    @pl.when(i >= n_p)
    def _():
        stage(fo_s, ho_s, mo_s, x_s)

    for u in range(x_sc.shape[0] // MERGE_SUB):
        rows = slice(u * MERGE_SUB, (u + 1) * MERGE_SUB)
        _merge_rows(rows, cat_s, x_sc, w_ref, gf_ref, wr2_ref, br_ref, h_ref, xn_ref, eid_ref, gate_ref)


def _merge_rows(rows, cat_s, x_sc, w_ref, gf_ref, wr2_ref, br_ref, h_ref, xn_ref, eid_ref, gate_ref):
    D = x_sc.shape[1]
    ssq = jnp.zeros((MERGE_SUB, 1), F32)
    for c in range(D // PROJ_CHUNK):
        cs = slice(c * PROJ_CHUNK, (c + 1) * PROJ_CHUNK)
        hc = x_sc[rows, cs] + _dot(cat_s[rows, :], w_ref[:, cs])
        h_ref[rows, cs] = hc
        ssq = ssq + jnp.sum(hc * hc, axis=-1, keepdims=True)
    xn = h_ref[rows, :] * lax.rsqrt(ssq * (1.0 / D) + EPS) * gf_ref[...]
    xn_ref[rows, :] = xn
    x_hi = xn.astype(BF16)
    x_lo = (xn - x_hi.astype(F32)).astype(BF16)
    hi2 = _dot(x_hi, wr2_ref[...])
    logits = (hi2[:, :LANES] + (hi2[:, LANES:] + _dot(x_lo, wr2_ref[:, :LANES]))) + br_ref[...]
    lane = lax.broadcasted_iota(jnp.int32, logits.shape, 1)
    big = jnp.int32(LANES)
    ninf = -jnp.inf
    gl = jnp.where(lane < N_GROUPS, logits, ninf)
    gmax = jnp.max(gl, axis=-1, keepdims=True)
    g_sel = jnp.min(jnp.where(gl == gmax, lane, big), axis=-1, keepdims=True)
    g_prob = 1.0 / jnp.sum(jnp.exp(gl - gmax), axis=-1, keepdims=True)
    lo = N_GROUPS + EXPERTS_PER_GROUP * g_sel
    el = jnp.where((lane >= lo) & (lane < lo + EXPERTS_PER_GROUP), logits, ninf)
    v1 = jnp.max(el, axis=-1, keepdims=True)
    i1 = jnp.min(jnp.where(el == v1, lane, big), axis=-1, keepdims=True)
    el2 = jnp.where(lane == i1, ninf, el)
    v2 = jnp.max(el2, axis=-1, keepdims=True)
    i2 = jnp.min(jnp.where(el2 == v2, lane, big), axis=-1, keepdims=True)
    t = jnp.exp(v2 - v1)
    w1 = g_prob / (1.0 + t)
    w2 = g_prob * t / (1.0 + t)
    eid_ref[rows, :] = jnp.where(lane == 0, i1 - N_GROUPS, jnp.where(lane == 1, i2 - N_GROUPS, 0))
    gate_ref[rows, :] = jnp.where(lane == 0, w1, jnp.where(lane == 1, w2, 0.0))


def _merge(prompt, sample, w_out, g_ffn, wr2, b_r, tm):
    Tp, D = prompt[3].shape
    Ts = sample[3].shape[0]
    assert Tp % tm == 0 and Ts % tm == 0 and tm % MERGE_SUB == 0
    n_p, n_s = Tp // tm, Ts // tm
    T = Tp + Ts
    once = pl.Buffered(1)
    p_row = lambda w: pl.BlockSpec((tm, w), lambda i: (jnp.minimum(i, n_p - 1), 0))
    s_row = lambda w: pl.BlockSpec((tm, w), lambda i: (jnp.maximum(i - n_p, 0), 0), pipeline_mode=once)
    row = lambda w: pl.BlockSpec((tm, w), lambda i: (i, 0))
    full = lambda a: pl.BlockSpec(a.shape, lambda i: (0,) * a.ndim, pipeline_mode=once)
    widths = (FOX_W, HGRN_W, MEM_W, D)
    return pl.pallas_call(
        functools.partial(_merge_kernel, n_p=n_p),
        grid=(n_p + n_s,),
        in_specs=[p_row(w) for w in widths] + [s_row(w) for w in widths]
                 + [full(w_out), full(g_ffn), full(wr2), full(b_r)],
        out_specs=[row(D), row(D), row(LANES), row(LANES)],
        out_shape=[jax.ShapeDtypeStruct((T, D), F32), jax.ShapeDtypeStruct((T, D), F32),
                   jax.ShapeDtypeStruct((T, LANES), jnp.int32), jax.ShapeDtypeStruct((T, LANES), F32)],
        scratch_shapes=[pltpu.VMEM((tm, FOX_W + HGRN_W + MEM_W), BF16), pltpu.VMEM((tm, D), F32)],
        compiler_params=_cparams(("arbitrary",)),
        name="merge",
    )(*prompt, *sample, w_out, g_ffn, wr2, b_r)


def _experts_kernel(be_ref, nb_ref, src_ref, tok_ref, x_hbm, wg_ref, wu_ref, wd_ref, y_ref,
                    xbuf, wg_s, wu_s, wd_s, sem):
    b = pl.program_id(0)

    @pl.when((b == 0) | (be_ref[b] != be_ref[jnp.maximum(b - 1, 0)]))
    def _():
        wg_s[...] = wg_ref[0].astype(BF16)
        wu_s[...] = wu_ref[0].astype(BF16)
        wd_s[...] = wd_ref[0].astype(BF16)

    n_used = nb_ref[0]
    BM = MOE_BM

    def gather(blk, slot):
        base = src_ref[blk]
        for r in range(BM):
            pltpu.make_async_copy(x_hbm.at[pl.ds(tok_ref[base + r], 1)], xbuf.at[slot, pl.ds(r, 1)],
                                  sem.at[slot]).start()

    def wait(slot):
        pltpu.make_async_copy(x_hbm.at[pl.ds(0, BM)], xbuf.at[slot], sem.at[slot]).wait()

    slot = lax.rem(b, 2)

    @pl.when(b == 0)
    def _():
        gather(0, 0)

    for s in range(2):
        @pl.when((b + 1 < n_used) & (slot == s))
        def _():
            gather(b + 1, 1 - s)

    @pl.when(b < n_used)
    def _():
        wait(slot)
        x = xbuf[slot].astype(BF16)
        hmid = _dot(x, wg_s[...])
        hmid = hmid * _sigmoid(hmid) * _dot(x, wu_s[...])
        y_ref[...] = _dot(hmid.astype(BF16), wd_s[...])

    @pl.when(b >= n_used)
    def _():
        y_ref[...] = jnp.zeros_like(y_ref)


def _experts(block_e, n_used, block_src, sorted_tok, xn, w_gate, w_up, w_down):
    n_blocks = block_e.shape[0]
    D, FF = w_gate.shape[1:]
    BM = MOE_BM
    grid_spec = pltpu.PrefetchScalarGridSpec(
        num_scalar_prefetch=4,
        grid=(n_blocks,),
        in_specs=[pl.BlockSpec(memory_space=pl.ANY),
                  pl.BlockSpec((1, D, FF), lambda b, be, *_: (be[b], 0, 0)),
                  pl.BlockSpec((1, D, FF), lambda b, be, *_: (be[b], 0, 0)),
                  pl.BlockSpec((1, FF, D), lambda b, be, *_: (be[b], 0, 0))],
        out_specs=pl.BlockSpec((BM, D), lambda b, *_: (b, 0)),
        scratch_shapes=[pltpu.VMEM((2, BM, D), F32), pltpu.VMEM((D, FF), BF16), pltpu.VMEM((D, FF), BF16),
                        pltpu.VMEM((FF, D), BF16), pltpu.SemaphoreType.DMA((2,))],
    )
    return pl.pallas_call(
        _experts_kernel,
        grid_spec=grid_spec,
        out_shape=jax.ShapeDtypeStruct((n_blocks * BM, D), F32),
        compiler_params=_cparams(("arbitrary",)),
        name="experts",
    )(block_e, n_used, block_src, sorted_tok, xn, w_gate, w_up, w_down)


COMB_TM = 256


def _combine_kernel(pos_ref, h_ref, gate_ref, y_hbm, op_ref, os_ref, ybuf, sem, *, n_p):
    i = pl.program_id(0)
    n = pl.num_programs(0)
    TM = COMB_TM

    def gather(blk, slot):
        base = blk * (2 * TM)
        for r in range(2 * TM):
            pltpu.make_async_copy(y_hbm.at[pl.ds(pos_ref[base + r], 1)], ybuf.at[slot, pl.ds(r, 1)],
                                  sem.at[slot]).start()

    def wait(slot):
        pltpu.make_async_copy(y_hbm.at[pl.ds(0, 2 * TM)], ybuf.at[slot], sem.at[slot]).wait()

    slot = lax.rem(i, 2)

    @pl.when(i == 0)
    def _():
        gather(0, 0)

    for s in range(2):
        @pl.when((i + 1 < n) & (slot == s))
        def _():
            gather(i + 1, 1 - s)

    wait(slot)
    g = gate_ref[...]
    res = h_ref[...] + (g[:, 0:1] * ybuf[slot, 0:TM, :] + g[:, 1:2] * ybuf[slot, TM:2 * TM, :])

    @pl.when(i < n_p)
    def _():
        op_ref[...] = res

    @pl.when(i >= n_p)
    def _():
        os_ref[...] = res


def _combine(pos, h, gate, y_slots, t_prompt):
    T, D = h.shape
    TM = COMB_TM
    n_p = t_prompt // TM
    grid_spec = pltpu.PrefetchScalarGridSpec(
        num_scalar_prefetch=1,
        grid=(T // TM,),
        in_specs=[pl.BlockSpec((TM, D), lambda i, p: (i, 0)), pl.BlockSpec((TM, LANES), lambda i, p: (i, 0)),
                  pl.BlockSpec(memory_space=pl.ANY)],
        out_specs=[pl.BlockSpec((TM, D), lambda i, p: (jnp.minimum(i, n_p - 1), 0)),
                   pl.BlockSpec((TM, D), lambda i, p: (jnp.maximum(i - n_p, 0), 0))],
        scratch_shapes=[pltpu.VMEM((2, 2 * TM, D), F32), pltpu.SemaphoreType.DMA((2,))],
    )
    return pl.pallas_call(
        functools.partial(_combine_kernel, n_p=n_p),
        grid_spec=grid_spec,
        out_shape=[jax.ShapeDtypeStruct((t_prompt, D), F32), jax.ShapeDtypeStruct((T - t_prompt, D), F32)],
        compiler_params=_cparams(("arbitrary",)),
        name="combine",
    )(pos, h, gate, y_slots)


def _moe(h, xn, eid, gate, w_gate, w_up, w_down, t_prompt):
    T = h.shape[0]
    A = T * TOP_K
    BM = MOE_BM
    n_blocks = -(-(A + N_EXPERTS * (BM - 1)) // BM)
    e_flat = eid[:, :TOP_K].reshape(A)
    onehot = (e_flat[:, None] == jnp.arange(N_EXPERTS, dtype=jnp.int32)[None, :]).astype(jnp.int32)
    csum = jnp.cumsum(onehot, axis=0)
    counts = csum[-1]
    rank = jnp.take_along_axis(csum, e_flat[:, None], axis=1)[:, 0] - 1
    padded = (counts + BM - 1) // BM * BM
    pad_end = jnp.cumsum(padded)
    pad_start = pad_end - padded
    pos = (pad_start[e_flat] + rank).astype(jnp.int32)
    order = jnp.argsort(e_flat, stable=True)
    sorted_tok = jnp.pad((order // TOP_K).astype(jnp.int32), (0, BM))
    start = jnp.cumsum(counts) - counts
    block_first = jnp.arange(n_blocks, dtype=jnp.int32) * BM
    block_e = jnp.minimum(jnp.sum(pad_end[None, :] <= block_first[:, None], axis=1), N_EXPERTS - 1).astype(jnp.int32)
    block_src = jnp.minimum(start[block_e] + block_first - pad_start[block_e], A).astype(jnp.int32)
    n_used = (pad_end[-1] // BM).astype(jnp.int32).reshape(1)
    y_slots = _experts(block_e, n_used, block_src, sorted_tok, xn, w_gate, w_up, w_down)
    pos_tiles = pos.reshape(T // COMB_TM, COMB_TM, TOP_K).transpose(0, 2, 1).reshape(-1)
    return _combine(pos_tiles, h, gate, y_slots, t_prompt)


def _prep_weights(w_in, b_fox_f, w_router_group, b_router_group, w_router_expert, b_router_expert):
    D = w_in.shape[0]
    c = [0]
    for s in (FOX_W, FOX_W, FOX_W, FOX_HEADS, HGRN_W, HGRN_W, HGRN_W, HGRN_W, MEM_W):
        c.append(c[-1] + s)
    w_t = w_in.T
    w_a = w_t[c[0]:c[3]].astype(BF16)
    w_b = w_t[c[4]:c[8]].astype(BF16)
    w_q = w_t[c[8]:c[9]].astype(BF16)
    w_f = jnp.pad(w_t[c[3]:c[4]], ((0, LANES - FOX_HEADS), (0, 0))).astype(BF16)
    b_f_pad = jnp.zeros((1, LANES), F32).at[0, :FOX_HEADS].set(b_fox_f)
    n_r = N_GROUPS + N_EXPERTS
    w_r = jnp.zeros((D, LANES), F32).at[:, :N_GROUPS].set(w_router_group).at[:, N_GROUPS:n_r].set(w_router_expert)
    b_r = jnp.zeros((1, LANES), F32).at[0, :N_GROUPS].set(b_router_group).at[0, N_GROUPS:n_r].set(b_router_expert)
    wr_hi = w_r.astype(BF16)
    wr_lo = (w_r - wr_hi.astype(F32)).astype(BF16)
    wr2 = jnp.concatenate([wr_hi, wr_lo], axis=1)
    return (w_a, w_q, w_f, w_b), b_f_pad, wr2, b_r


def kernel(x_prompt, x_sample, cache_fox_k, cache_fox_v, cache_fox_logf, cache_mem_k, cache_mem_v, state_hgrn, page_table, mem_prompt, g_attn_norm, w_in, b_fox_f, g_fox_q, g_fox_k, lb_logits, g_hgrn_out, g_mem_norm, w_mem_kv, g_mem_q, g_mem_k, w_out, g_ffn_norm, w_router_group, b_router_group, w_router_expert, b_router_expert, w_gate_e, w_up_e, w_down_e):
    assert w_in.shape[0] == 1, "single-layer step"
    Bp, S, D = x_prompt.shape
    Bd, L, _ = x_sample.shape
    n_pool, page = cache_fox_k.shape[1], cache_fox_k.shape[2]
    assert page == LANES and L <= SUBLANES
    M = mem_prompt.shape[1]
    l = 0
    row = lambda a: a[l].reshape(1, -1)
    head_major = lambda a: jnp.swapaxes(a, -3, -2)

    w_seg, b_f_pad, wr2, b_r = _prep_weights(
        w_in[l], b_fox_f[l], w_router_group[l], b_router_group[l], w_router_expert[l], b_router_expert[l])
    w_out_bf = w_out[l].astype(BF16)
    w_mkv_bf = w_mem_kv[l].astype(BF16)
    experts_w = (w_gate_e[l], w_up_e[l], w_down_e[l])
    proj_args = (row(g_attn_norm), *w_seg, row(g_fox_q), row(g_fox_k), row(g_mem_q), lb_logits, b_f_pad)

    Tp = Bp * S
    xp = x_prompt.reshape(Tp, D)
    fq, fk_p, fkb, fv_p, fvb, hq, hlf, hv, hgs, mq, flf_p = _in_proj(xp, *proj_args, tm=512, seq=S)
    seq = lambda a: a.reshape(Bp, S, a.shape[-1])
    ck = _fox_cum(seq(flf_p))
    fox_o = _fox_attn(seq(fq), seq(fkb), seq(fvb), ck)
    hg_o, s_p = _hgrn(seq(hq), seq(hlf), seq(hv), seq(hgs), row(g_hgrn_out), None, C=256, n_heads=3, n_valid=256)
    mk, mv = _mem_kv(mem_prompt.reshape(Bp * M, D), row(g_mem_norm), w_mkv_bf, row(g_mem_k), tm=256)
    mem4 = lambda a, b: a.reshape(b, M, MEM_HEADS, HEAD_DIM)
    mem_o = _mem_attn(seq(mq), mem4(mk, Bp), mem4(mv, Bp), tq=512)
    prompt = (fox_o.reshape(Tp, FOX_W), hg_o.reshape(Tp, HGRN_W), mem_o.reshape(Tp, MEM_W), xp)

    R = SUBLANES
    Ts = Bd * R
    xs = jnp.pad(x_sample, ((0, 0), (0, R - L), (0, 0))).reshape(Ts, D)
    fq, fk_s, fkb, fv_s, fvb, hq, hlf, hv, hgs, mq, flf_s = _in_proj(xs, *proj_args, tm=Ts, seq=R)
    seqs = lambda a: a.reshape(Bd, R, a.shape[-1])
    lf_new_t = jnp.swapaxes(seqs(flf_s)[:, :, :HEAD_PAD], 1, 2)
    lf_new_t = jnp.pad(lf_new_t, ((0, 0), (0, 0), (0, LANES - R)))
    cache_lf_t = jnp.pad(jnp.swapaxes(cache_fox_logf[l].astype(F32), 1, 2), ((0, 0), (0, HEAD_PAD - FOX_HEADS), (0, 0)))
    fox_o = _fox_dec(page_table, seqs(fq), seqs(fkb), seqs(fvb), lf_new_t, head_major(cache_fox_k[l]), head_major(cache_fox_v[l]),
                     cache_lf_t, n_tok=L)
    hg_o, s_s = _hgrn(seqs(hq), seqs(hlf), seqs(hv), seqs(hgs), row(g_hgrn_out), state_hgrn[l],
                      C=R, n_heads=HGRN_HEADS, n_valid=L)
    mem_o = _mem_attn(seqs(mq), cache_mem_k[l], cache_mem_v[l], tq=R, nb=4 if Bd % 4 == 0 else 1)
    sample = (fox_o.reshape(Ts, FOX_W), hg_o.reshape(Ts, HGRN_W), mem_o.reshape(Ts, MEM_W), xs)

    sample = tuple(jnp.pad(a, ((0, -Ts % MERGE_TM), (0, 0))) for a in sample)
    h, xn, eid, gate = _merge(prompt, sample, w_out_bf, row(g_ffn_norm), wr2, b_r, tm=MERGE_TM)
    y_p, y_s = _moe(h, xn, eid, gate, *experts_w, t_prompt=Tp)

    cut = lambda a: a.reshape((Bd, R) + a.shape[1:])[:, :L]
    return (y_p.reshape(Bp, S, D), cut(y_s[:Ts]),
            head_major(fk_p)[None], head_major(fv_p)[None],
            flf_p[:, :FOX_HEADS].reshape(1, Bp, S, FOX_HEADS), s_p[None],
            mem4(mk, Bp)[None], mem4(mv, Bp)[None],
            head_major(fk_s)[None, :, :L], head_major(fv_s)[None, :, :L], cut(flf_s)[:, :, :FOX_HEADS][None], s_s[None])
```

```python
import functools

import jax
import jax.numpy as jnp
from jax import lax
from jax.experimental import pallas as pl
from jax.experimental.pallas import tpu as pltpu

F32 = jnp.float32
BF16 = jnp.bfloat16

HEAD_DIM = 128
FOX_HEADS = 6
HGRN_HEADS = 6
MEM_HEADS = 4
FOX_W = FOX_HEADS * HEAD_DIM
HGRN_W = HGRN_HEADS * HEAD_DIM
MEM_W = MEM_HEADS * HEAD_DIM
N_GROUPS = 4
EXPERTS_PER_GROUP = 8
N_EXPERTS = N_GROUPS * EXPERTS_PER_GROUP
TOP_K = 2
EPS = 1e-6
ATTN_SCALE = HEAD_DIM ** -0.5
LANES = 128
SUBLANES = 8
SEG_W = FOX_W
PROJ_CHUNK = 256
HEAD_PAD = 8
MOE_BM = 256
MERGE_TM = 512
MERGE_SUB = 256
VMEM_LIMIT = 52 * 1024 * 1024

NT_DIMS = (((1,), (1,)), ((), ()))
TN_DIMS = (((0,), (0,)), ((), ()))


def _cparams(sem, vmem=VMEM_LIMIT):
    return pltpu.CompilerParams(dimension_semantics=sem, vmem_limit_bytes=vmem)


def _split3(x):
    hi = x.astype(BF16)
    r1 = x - hi.astype(F32)
    mid = r1.astype(BF16)
    lo = (r1 - mid.astype(F32)).astype(BF16)
    return hi, mid, lo


def _dot(a, b):
    return jnp.dot(a, b, preferred_element_type=F32)


def _dot_nt(a, b):
    return lax.dot_general(a, b, NT_DIMS, preferred_element_type=F32)


def _sigmoid(x):
    return 1.0 / (1.0 + jnp.exp(-x))


def _log_sigmoid(x):
    return jnp.minimum(x, 0.0) - jnp.log(1.0 + jnp.exp(-jnp.abs(x)))


def _rms_heads(a, g, n_heads, scale):
    outs = []
    for h in range(n_heads):
        ah = a[:, h * HEAD_DIM:(h + 1) * HEAD_DIM]
        ms = jnp.mean(ah * ah, axis=-1, keepdims=True)
        outs.append(ah * lax.rsqrt(ms + EPS) * (g * scale))
    return jnp.concatenate(outs, axis=-1)


def _store_heads(ref, a, n_heads):
    for h in range(n_heads):
        ref[:, h, :] = a[:, h * HEAD_DIM:(h + 1) * HEAD_DIM]


def _store_head_major(ref, a, h0):
    nb, _, rows, _ = ref.shape
    for h in range(a.shape[1] // HEAD_DIM):
        ref[:, h0 + h] = a[:, h * HEAD_DIM:(h + 1) * HEAD_DIM].reshape(nb, rows, HEAD_DIM)


def _static_when(cond):
    def deco(f):
        if cond:
            f()
        return f
    return deco


def _in_proj_kernel(*refs, names, segs):
    r = dict(zip(names, refs))
    xn_s = r["xn_s"]
    x = r["x"][...]
    ms = jnp.mean(x * x, axis=-1, keepdims=True)
    xn_s[...] = (x * lax.rsqrt(ms + EPS) * r["g_norm"][...]).astype(BF16)

    def segment(post, w_ref, row0, width=SEG_W):
        for c in range(width // PROJ_CHUNK):
            cs = slice(c * PROJ_CHUNK, (c + 1) * PROJ_CHUNK)
            post(c, cs, _dot_nt(xn_s[...], w_ref[row0 + c * PROJ_CHUNK:row0 + (c + 1) * PROJ_CHUNK, :]))

    heads_per_chunk = PROJ_CHUNK // HEAD_DIM

    @_static_when("fq" in segs)
    def _():
        def post(c, cs, acc):
            r["fq"][:, cs] = _rms_heads(acc, r["g_fq"][...], heads_per_chunk, ATTN_SCALE).astype(BF16)
        segment(post, r["w_a"], 0)

    @_static_when("fk" in segs)
    def _():
        def post(c, cs, acc):
            fk = _rms_heads(acc, r["g_fk"][...], heads_per_chunk, 1.0)
            _store_head_major(r["fk"], fk, c * heads_per_chunk)
            r["fkb"][:, cs] = fk.astype(BF16)
        segment(post, r["w_a"], SEG_W)

    @_static_when("fv" in segs)
    def _():
        def post(c, cs, acc):
            _store_head_major(r["fv"], acc, c * heads_per_chunk)
            r["fvb"][:, cs] = acc.astype(BF16)
        segment(post, r["w_a"], 2 * SEG_W)

    @_static_when("mq" in segs)
    def _():
        def post(c, cs, acc):
            r["mq"][:, cs] = _rms_heads(acc, r["g_mq"][...], heads_per_chunk, ATTN_SCALE).astype(BF16)
        segment(post, r["w_q"], 0, MEM_W)

    @_static_when("flf" in segs)
    def _():
        r["flf"][...] = _log_sigmoid(_dot_nt(xn_s[...], r["w_f"][...]) + r["b_f"][...])

    @_static_when("hq" in segs)
    def _():
        def post(c, cs, acc):
            r["hq"][:, cs] = acc * _sigmoid(acc) * ATTN_SCALE
        segment(post, r["w_b"], 0)

    @_static_when("hlf" in segs)
    def _():
        l = r["lb_logits"][...]
        mx = jnp.max(l, axis=0, keepdims=True)
        ex = jnp.exp(l - mx)
        lb = ex[0:1, :] / jnp.sum(ex, axis=0, keepdims=True)

        def post(c, cs, acc):
            r["hlf"][:, cs] = jnp.log(lb[:, cs] + (1.0 - lb[:, cs]) * _sigmoid(acc))
        segment(post, r["w_b"], SEG_W)

    @_static_when("hv" in segs)
    def _():
        def post(c, cs, acc):
            r["hv"][:, cs] = acc
        segment(post, r["w_b"], 2 * SEG_W)

    @_static_when("hgs" in segs)
    def _():
        def post(c, cs, acc):
            r["hgs"][:, cs] = acc * _sigmoid(acc)
        segment(post, r["w_b"], 3 * SEG_W)


def _in_proj_call(x, ins, outs, tm, seq, name):
    T, D = x.shape
    nb, rows = max(1, tm // seq), min(tm, seq)
    assert nb * rows == tm and seq % rows == 0
    tiles_per_seq = seq // rows
    resident = lambda a: pl.BlockSpec(a.shape, lambda i: (0,) * a.ndim, pipeline_mode=pl.Buffered(1))

    def out_spec(o):
        if len(o.shape) == 4:
            return pl.BlockSpec((nb, FOX_HEADS, rows, HEAD_DIM), lambda i: (i // tiles_per_seq, 0, i % tiles_per_seq, 0))
        return pl.BlockSpec((tm, o.shape[1]), lambda i: (i, 0))

    names = ("x",) + tuple(ins) + tuple(outs) + ("xn_s",)
    res = pl.pallas_call(
        functools.partial(_in_proj_kernel, names=names, segs=tuple(outs)),
        grid=(T // tm,),
        in_specs=[pl.BlockSpec((tm, D), lambda i: (i, 0))] + [resident(a) for a in ins.values()],
        out_specs=[out_spec(o) for o in outs.values()],
        out_shape=list(outs.values()),
        scratch_shapes=[pltpu.VMEM((tm, D), BF16)],
        compiler_params=_cparams(("arbitrary",)),
        name=name,
    )(x, *ins.values())
    return dict(zip(outs, res))


def _in_proj(x, g_norm, w_a, w_q, w_f, w_b, g_fq, g_fk, g_mq, lb_logits, b_f_pad, tm, seq):
    T = x.shape[0]
    kv_cache = jax.ShapeDtypeStruct((T // seq, FOX_HEADS, seq, HEAD_DIM), F32)
    rows = lambda w, dt: jax.ShapeDtypeStruct((T, w), dt)
    fox = _in_proj_call(
        x, dict(g_norm=g_norm, w_a=w_a, w_q=w_q, w_f=w_f, g_fq=g_fq, g_fk=g_fk, g_mq=g_mq, b_f=b_f_pad),
        dict(fq=rows(FOX_W, BF16),
             fk=kv_cache,
             fkb=rows(FOX_W, BF16),
             fv=kv_cache, fvb=rows(FOX_W, BF16),
             mq=rows(MEM_W, BF16),
             flf=rows(LANES, F32)),
        tm, seq, "in_proj_fox")
    hg = _in_proj_call(
        x, dict(g_norm=g_norm, w_b=w_b, lb_logits=lb_logits),
        dict(hq=rows(HGRN_W, F32),
             hlf=rows(HGRN_W, F32),
             hv=rows(HGRN_W, F32),
             hgs=rows(HGRN_W, F32)),
        tm, seq, "in_proj_hgrn")
    return (fox["fq"], fox["fk"], fox["fkb"], fox["fv"], fox["fvb"], hg["hq"], hg["hlf"], hg["hv"], hg["hgs"],
            fox["mq"], fox["flf"])


CUM_C = 256


def _fox_cum_kernel(lf_ref, ck_ref):
    S = lf_ref.shape[1]
    r = lax.broadcasted_iota(jnp.int32, (CUM_C, CUM_C), 0)
    c = lax.broadcasted_iota(jnp.int32, (CUM_C, CUM_C), 1)
    tri = jnp.where(c <= r, 1.0, 0.0).astype(BF16)
    carry = jnp.zeros((1, LANES), F32)
    for i in range(S // CUM_C):
        sl = slice(i * CUM_C, (i + 1) * CUM_C)
        hi, mid, lo = _split3(lf_ref[0, sl, :])
        cum = (_dot(tri, hi) + _dot(tri, mid)) + _dot(tri, lo) + carry
        carry = cum[CUM_C - 1:CUM_C, :]
        ck_ref[0, :, sl] = cum.T[0:HEAD_PAD, :]


def _fox_cum(flf):
    B, S, _ = flf.shape
    return pl.pallas_call(
        _fox_cum_kernel,
        grid=(B,),
        in_specs=[pl.BlockSpec((1, S, LANES), lambda b: (b, 0, 0))],
        out_specs=pl.BlockSpec((1, HEAD_PAD, S), lambda b: (b, 0, 0)),
        out_shape=jax.ShapeDtypeStruct((B, HEAD_PAD, S), F32),
        compiler_params=_cparams(("arbitrary",)),
        name="fox_cum",
    )(flf)


FOX_T = 256


def _softmax_step(s, v_bf, m, l, acc):
    m_new = jnp.maximum(m, jnp.max(s, axis=-1, keepdims=True))
    alpha = jnp.exp(m - m_new)
    p = jnp.exp(s - m_new)
    l = alpha * l + jnp.sum(p, axis=-1, keepdims=True)
    acc = alpha * acc + _dot(p.astype(BF16), v_bf)
    return m_new, l, acc


def _fox_attn_kernel(q_ref, k_ref, v_ref, ck_ref, o_ref, m_s, l_s, acc_s):
    qi = pl.program_id(1)
    T = FOX_T
    m_s[...] = jnp.full_like(m_s, -jnp.inf)
    l_s[...] = jnp.zeros_like(l_s)
    acc_s[...] = jnp.zeros_like(acc_s)
    row = lax.broadcasted_iota(jnp.int32, (T, T), 0)
    col = lax.broadcasted_iota(jnp.int32, (T, T), 1)
    causal = col <= row

    def tile(ks, width, masked=False):
        for h in range(FOX_HEADS):
            hs = slice(h * HEAD_DIM, (h + 1) * HEAD_DIM)
            s = _dot_nt(q_ref[0, :, hs], k_ref[0, pl.ds(ks, width), hs]) - ck_ref[0, h:h + 1, pl.ds(ks, width)]
            if masked:
                s = jnp.where(causal, s, -jnp.inf)
            m_old = m_s[:, hs]
            m_new = jnp.maximum(m_old, jnp.max(s, axis=-1, keepdims=True))
            alpha = jnp.exp(m_old - m_new)
            p = jnp.exp(s - jnp.concatenate([m_new] * (width // HEAD_DIM), axis=-1))
            m_s[:, hs] = m_new
            l_s[:, hs] = alpha * l_s[:, hs] + jnp.sum(p, axis=-1, keepdims=True)
            acc_s[:, hs] = alpha * acc_s[:, hs] + _dot(p.astype(BF16), v_ref[0, pl.ds(ks, width), hs])

    def body(kt, c):
        tile(pl.multiple_of(kt * T, T), T)
        return c

    lax.fori_loop(0, qi, body, 0)
    tile(pl.multiple_of(qi * T, T), T, masked=True)
    o_ref[0] = (acc_s[...] / l_s[...]).astype(o_ref.dtype)


def _fox_attn(fq, fk, fv, ck):
    B, S, _ = fq.shape
    T = FOX_T
    return pl.pallas_call(
        _fox_attn_kernel,
        grid=(B, S // T),
        in_specs=[pl.BlockSpec((1, T, FOX_W), lambda b, i: (b, i, 0)),
                  pl.BlockSpec((1, S, FOX_W), lambda b, i: (b, 0, 0)),
                  pl.BlockSpec((1, S, FOX_W), lambda b, i: (b, 0, 0)),
                  pl.BlockSpec((1, HEAD_PAD, S), lambda b, i: (b, 0, 0))],
        out_specs=pl.BlockSpec((1, T, FOX_W), lambda b, i: (b, i, 0)),
        out_shape=jax.ShapeDtypeStruct((B, S, FOX_W), BF16),
        scratch_shapes=[pltpu.VMEM((T, FOX_W), F32)] * 3,
        compiler_params=_cparams(("arbitrary", "arbitrary")),
        name="fox_attn",
    )(fq, fk, fv, ck)


def _level_ref(b, m):
    C = b.shape[0]
    if 2 * m >= SUBLANES:
        b3 = b.reshape(C // (2 * m), 2 * m, HEAD_DIM)
        r = jnp.broadcast_to(b3[:, m - 1:m, :], b3.shape)
        return r.reshape(C, HEAD_DIM)
    b3 = b.reshape(C // SUBLANES, SUBLANES, HEAD_DIM)
    sub = lax.broadcasted_iota(jnp.int32, b3.shape, 1)
    pick = lambda i: jnp.broadcast_to(b3[:, i:i + 1, :], b3.shape)
    if m == 2:
        r = jnp.where(sub < 4, pick(1), pick(5))
    else:
        r = jnp.where(sub < 2, pick(0), jnp.where(sub < 4, pick(2), jnp.where(sub < 6, pick(4), pick(6))))
    return r.reshape(C, HEAD_DIM)


def _level_index(C):
    t = lax.broadcasted_iota(jnp.int32, (C, C), 0)
    s = lax.broadcasted_iota(jnp.int32, (C, C), 1)
    x = jnp.bitwise_xor(t, s)
    lvl = jnp.full((C, C), -1, jnp.int32)
    j, m = 0, 1
    while m < C:
        lvl = jnp.where(x >= m, j, lvl)
        j, m = j + 1, 2 * m
    return jnp.where(t > s, lvl, -1)


def _neg_abs(x):
    return pltpu.bitcast(jnp.bitwise_or(pltpu.bitcast(x, jnp.uint32), jnp.uint32(0x80000000)), F32)


def _hgrn_chunk(q, g, v, st, tri, lvl, n_valid):
    C = q.shape[0]
    hi, mid, lo = _split3(g)
    b = (_dot(tri, hi) + _dot(tri, mid)) + _dot(tri, lo)
    k = 1.0 - jnp.exp(g)
    a = jnp.zeros((C, C), F32)
    j, m = 0, 1
    while m < C:
        e = jnp.exp(_neg_abs(b - _level_ref(b, m)))
        a_l = _dot_nt((q * e).astype(BF16), (k * e).astype(BF16))
        a = jnp.where(lvl == j, a_l, a)
        j, m = j + 1, 2 * m
    v_bf = v.astype(BF16)
    diag = jnp.sum(q * k, axis=-1, keepdims=True)
    o = _dot_nt((q * jnp.exp(b)).astype(BF16), st.astype(BF16)) + _dot(a.astype(BF16), v_bf) + diag * v
    b_last = b[n_valid - 1:n_valid, :]
    kt = k * jnp.exp(jnp.minimum(b_last - b, 0.0))
    if n_valid < C:
        rows = lax.broadcasted_iota(jnp.int32, (C, HEAD_DIM), 0)
        kt = jnp.where(rows < n_valid, kt, 0.0)
    st_new = st * jnp.exp(b_last) + lax.dot_general(v_bf, kt.astype(BF16), TN_DIMS, preferred_element_type=F32)
    return o, st_new


def _hgrn_kernel(*refs, C, n_chunks, n_heads, n_valid, has_s0):
    if has_s0:
        q_ref, g_ref, v_ref, gs_ref, gn_ref, s0_ref, o_ref, sf_ref = refs
    else:
        q_ref, g_ref, v_ref, gs_ref, gn_ref, o_ref, sf_ref = refs
    r = lax.broadcasted_iota(jnp.int32, (C, C), 0)
    c = lax.broadcasted_iota(jnp.int32, (C, C), 1)
    tri = jnp.where(c <= r, 1.0, 0.0).astype(BF16)
    lvl = _level_index(C)
    gn = gn_ref[...]
    sts0 = tuple(s0_ref[0, h].T if has_s0 else jnp.zeros((HEAD_DIM, HEAD_DIM), F32) for h in range(n_heads))

    def body(ci, sts):
        rs = pl.ds(pl.multiple_of(ci * C, C), C)
        out = []
        for h in range(n_heads):
            hs = slice(h * HEAD_DIM, (h + 1) * HEAD_DIM)
            o, st = _hgrn_chunk(q_ref[0, rs, hs], g_ref[0, rs, hs], v_ref[0, rs, hs], sts[h], tri, lvl, n_valid)
            ms = jnp.mean(o * o, axis=-1, keepdims=True)
            o_ref[0, rs, hs] = (o * lax.rsqrt(ms + EPS) * gn * gs_ref[0, rs, hs]).astype(o_ref.dtype)
            out.append(st)
        return tuple(out)

    sts = lax.fori_loop(0, n_chunks, body, sts0) if n_chunks > 1 else body(0, sts0)
    for h in range(n_heads):
        sf_ref[0, h] = sts[h].T


def _hgrn(hq, hlf, hv, hgs, g_hn, s0, C, n_heads, n_valid):
    B, L, _ = hq.shape
    hp = HGRN_HEADS // n_heads
    w = n_heads * HEAD_DIM
    seq = pl.BlockSpec((1, L, w), lambda b, h: (b, 0, h))
    st_spec = pl.BlockSpec((1, n_heads, HEAD_DIM, HEAD_DIM), lambda b, h: (b, h, 0, 0))
    in_specs = [seq, seq, seq, seq, pl.BlockSpec((1, HEAD_DIM), lambda b, h: (0, 0))]
    args = [hq, hlf, hv, hgs, g_hn]
    if s0 is not None:
        in_specs.append(st_spec)
        args.append(s0)
    kern = functools.partial(_hgrn_kernel, C=C, n_chunks=L // C, n_heads=n_heads, n_valid=n_valid,
                             has_s0=s0 is not None)
    return pl.pallas_call(
        kern,
        grid=(B, hp),
        in_specs=in_specs,
        out_specs=[seq, st_spec],
        out_shape=[jax.ShapeDtypeStruct((B, L, HGRN_W), BF16),
                   jax.ShapeDtypeStruct((B, HGRN_HEADS, HEAD_DIM, HEAD_DIM), F32)],
        compiler_params=_cparams(("arbitrary", "arbitrary")),
        name="hgrn",
    )(*args)


def _mem_kv_kernel(x_ref, gn_ref, w_ref, gk_ref, mk_ref, mv_ref, xn_s):
    j = pl.program_id(1)

    @pl.when(j == 0)
    def _():
        x = x_ref[...]
        ms = jnp.mean(x * x, axis=-1, keepdims=True)
        xn_s[...] = (x * lax.rsqrt(ms + EPS) * gn_ref[...]).astype(BF16)

    acc = _dot(xn_s[...], w_ref[...])

    @pl.when(j == 0)
    def _():
        _store_heads(mk_ref, _rms_heads(acc, gk_ref[...], MEM_HEADS, 1.0), MEM_HEADS)

    @pl.when(j == 1)
    def _():
        _store_heads(mv_ref, acc, MEM_HEADS)


def _mem_kv(mem, g_norm, w, g_mk, tm):
    T, D = mem.shape
    out = jax.ShapeDtypeStruct((T, MEM_HEADS, HEAD_DIM), F32)
    return pl.pallas_call(
        _mem_kv_kernel,
        grid=(T // tm, 2),
        in_specs=[pl.BlockSpec((tm, D), lambda i, j: (i, 0)), pl.BlockSpec((1, D), lambda i, j: (0, 0)),
                  pl.BlockSpec((D, MEM_W), lambda i, j: (0, j)), pl.BlockSpec((1, HEAD_DIM), lambda i, j: (0, 0))],
        out_specs=[pl.BlockSpec((tm, MEM_HEADS, HEAD_DIM), lambda i, j: (i, 0, 0))] * 2,
        out_shape=[out, out],
        scratch_shapes=[pltpu.VMEM((tm, D), BF16)],
        compiler_params=_cparams(("arbitrary", "arbitrary")),
        name="mem_kv",
    )(mem, g_norm, w, g_mk)


def _mem_attn_kernel(q_ref, k_ref, v_ref, o_ref):
    for b in range(q_ref.shape[0]):
        for h in range(MEM_HEADS):
            hs = slice(h * HEAD_DIM, (h + 1) * HEAD_DIM)
            s = _dot_nt(q_ref[b, :, hs], k_ref[b, :, h, :].astype(BF16))
            p = jnp.exp(s - jnp.max(s, axis=-1, keepdims=True))
            l = jnp.sum(p, axis=-1, keepdims=True)
            o_ref[b, :, hs] = (_dot(p.astype(BF16), v_ref[b, :, h, :].astype(BF16)) / l).astype(o_ref.dtype)


def _mem_attn(mq, mk, mv, tq, nb=1):
    B, L, _ = mq.shape
    M = mk.shape[1]
    kv = pl.BlockSpec((nb, M, MEM_HEADS, HEAD_DIM), lambda b, i: (b, 0, 0, 0))
    qo = pl.BlockSpec((nb, tq, MEM_W), lambda b, i: (b, i, 0))
    return pl.pallas_call(
        _mem_attn_kernel,
        grid=(B // nb, L // tq),
        in_specs=[qo, kv, kv],
        out_specs=qo,
        out_shape=jax.ShapeDtypeStruct((B, L, MEM_W), BF16),
        compiler_params=_cparams(("arbitrary", "arbitrary")),
        name="mem_attn",
    )(mq, mk, mv)


DEC_G = 8
DEC_ROWS = 4 * HEAD_PAD


def _suffix_sum_lanes(x):
    lane = lax.broadcasted_iota(jnp.int32, x.shape, 1)
    s = 1
    while s < LANES:
        x = x + jnp.where(lane + s < LANES, pltpu.roll(x, LANES - s, 1), 0.0)
        s *= 2
    return x


def _prefix_sum_lanes(x):
    lane = lax.broadcasted_iota(jnp.int32, x.shape, 1)
    s = 1
    while s < LANES:
        x = x + jnp.where(lane >= s, pltpu.roll(x, s, 1), 0.0)
        s *= 2
    return x


def _fox_dec_kernel(pt_ref, q_ref, kn_ref, vn_ref, lfn_ref, *refs, n_tok):
    G = DEC_G
    k_refs, v_refs, lf_refs = refs[0:G], refs[G:2 * G], refs[2 * G:3 * G]
    o_ref = refs[3 * G]
    qbd, kn_s, vn_s, m_s, l_s, acc_s, car_s = refs[3 * G + 1:]
    st = pl.program_id(1)

    def attend(k_bf, v_bf, bias):
        s = _dot_nt(qbd[...].astype(BF16), k_bf) + bias
        m_new, l_new, acc_new = _softmax_step(s, v_bf, m_s[...], l_s[...], acc_s[...])
        m_s[...] = m_new
        l_s[...] = l_new
        acc_s[...] = acc_new

    @pl.when(st == 0)
    def _():
        qbd[...] = jnp.zeros_like(qbd)
        q = q_ref[0].astype(F32)
        for t in range(n_tok):
            for h in range(FOX_HEADS):
                hs = slice(h * HEAD_DIM, (h + 1) * HEAD_DIM)
                qbd[t * HEAD_PAD + h:t * HEAD_PAD + h + 1, hs] = q[t:t + 1, hs]
        kn_s[...] = jnp.zeros_like(kn_s)
        vn_s[...] = jnp.zeros_like(vn_s)
        kn_s[0:SUBLANES, :] = kn_ref[0].astype(F32)
        vn_s[0:SUBLANES, :] = vn_ref[0].astype(F32)
        m_s[...] = jnp.full_like(m_s, -jnp.inf)
        l_s[...] = jnp.zeros_like(l_s)
        acc_s[...] = jnp.zeros_like(acc_s)
        car_s[...] = jnp.zeros_like(car_s)
        ecum = _prefix_sum_lanes(lfn_ref[0])
        lane = lax.broadcasted_iota(jnp.int32, (HEAD_PAD, LANES), 1)
        bias = jnp.concatenate([jnp.where(lane <= t, -ecum, -jnp.inf) for t in range(n_tok)], axis=0)
        attend(kn_s[...].astype(BF16), vn_s[...].astype(BF16), bias)

    heads = lambda ref: jnp.concatenate([ref[0, h] for h in range(FOX_HEADS)], axis=-1).astype(BF16)
    carry = car_s[...]
    ds = []
    for i in range(G):
        lf = lf_refs[i][0]
        incl = _suffix_sum_lanes(lf)
        ds.append(carry + (incl - lf))
        carry = carry + incl[:, 0:1]
    car_s[...] = carry
    d_all = jnp.concatenate(ds, axis=-1)
    attend(jnp.concatenate([heads(r) for r in k_refs], axis=0),
           jnp.concatenate([heads(r) for r in v_refs], axis=0),
           jnp.concatenate([d_all] * n_tok, axis=0))

    @pl.when(st == pl.num_programs(1) - 1)
    def _():
        res = acc_s[...] / l_s[...]
        o_ref[...] = jnp.zeros_like(o_ref)
        for t in range(n_tok):
            for h in range(FOX_HEADS):
                hs = slice(h * HEAD_DIM, (h + 1) * HEAD_DIM)
                r = t * HEAD_PAD + h
                o_ref[0, t:t + 1, hs] = res[r:r + 1, hs].astype(o_ref.dtype)


def _fox_dec(page_table, fq, fk, fv, lf_new_t, cache_k, cache_v, cache_lf_t, n_tok):
    B, n_pages = page_table.shape
    G = DEC_G
    assert n_pages % G == 0 and n_tok * HEAD_PAD == DEC_ROWS
    n_steps = n_pages // G
    tok = lambda w: pl.BlockSpec((1, SUBLANES, w), lambda b, s, pt: (b, 0, 0))

    def page_spec(shape, i):
        return pl.BlockSpec((1,) + shape,
                            lambda b, s, pt: (pt[b * n_pages + (n_pages - 1 - (s * G + i))],) + (0,) * len(shape))

    kv_page = (FOX_HEADS, LANES, HEAD_DIM)
    in_specs = ([tok(FOX_W), tok(FOX_W), tok(FOX_W), tok(LANES)]
                + [page_spec(kv_page, i) for i in range(G)]
                + [page_spec(kv_page, i) for i in range(G)]
                + [page_spec((HEAD_PAD, LANES), i) for i in range(G)])
    grid_spec = pltpu.PrefetchScalarGridSpec(
        num_scalar_prefetch=1,
        grid=(B, n_steps),
        in_specs=in_specs,
        out_specs=pl.BlockSpec((1, SUBLANES, FOX_W), lambda b, s, pt: (b, 0, 0)),
        scratch_shapes=[pltpu.VMEM((DEC_ROWS, FOX_W), F32),
                        pltpu.VMEM((LANES, FOX_W), F32), pltpu.VMEM((LANES, FOX_W), F32),
                        pltpu.VMEM((DEC_ROWS, 1), F32), pltpu.VMEM((DEC_ROWS, 1), F32),
                        pltpu.VMEM((DEC_ROWS, FOX_W), F32), pltpu.VMEM((HEAD_PAD, 1), F32)],
    )
    return pl.pallas_call(
        functools.partial(_fox_dec_kernel, n_tok=n_tok),
        grid_spec=grid_spec,
        out_shape=jax.ShapeDtypeStruct((B, SUBLANES, FOX_W), BF16),
        compiler_params=_cparams(("arbitrary", "arbitrary")),
        name="fox_dec",
    )(page_table.reshape(-1), fq, fk, fv, lf_new_t, *([cache_k] * G), *([cache_v] * G), *([cache_lf_t] * G))


def _merge_kernel(fo_p, ho_p, mo_p, x_p, fo_s, ho_s, mo_s, x_s, w_ref, gf_ref, wr2_ref, br_ref,
                  h_ref, xn_ref, eid_ref, gate_ref, cat_s, x_sc, *, n_p):
    i = pl.program_id(0)

    def stage(fo_ref, ho_ref, mo_ref, x_ref):
        cat_s[:, 0:FOX_W] = fo_ref[...]
        cat_s[:, FOX_W:FOX_W + HGRN_W] = ho_ref[...]
        cat_s[:, FOX_W + HGRN_W:] = mo_ref[...]
        x_sc[...] = x_ref[...]

    @pl.when(i < n_p)
    def _():
        stage(fo_p, ho_p, mo_p, x_p)

    @pl.when(i >= n_p)
    def _():
        stage(fo_s, ho_s, mo_s, x_s)

    for u in range(x_sc.shape[0] // MERGE_SUB):
        rows = slice(u * MERGE_SUB, (u + 1) * MERGE_SUB)
        _merge_rows(rows, cat_s, x_sc, w_ref, gf_ref, wr2_ref, br_ref, h_ref, xn_ref, eid_ref, gate_ref)


def _merge_rows(rows, cat_s, x_sc, w_ref, gf_ref, wr2_ref, br_ref, h_ref, xn_ref, eid_ref, gate_ref):
    D = x_sc.shape[1]
    ssq = jnp.zeros((MERGE_SUB, 1), F32)
    for c in range(D // PROJ_CHUNK):
        cs = slice(c * PROJ_CHUNK, (c + 1) * PROJ_CHUNK)
        hc = x_sc[rows, cs] + _dot(cat_s[rows, :], w_ref[:, cs])
        h_ref[rows, cs] = hc
        ssq = ssq + jnp.sum(hc * hc, axis=-1, keepdims=True)
    xn = h_ref[rows, :] * lax.rsqrt(ssq * (1.0 / D) + EPS) * gf_ref[...]
    xn_ref[rows, :] = xn
    x_hi = xn.astype(BF16)
    x_lo = (xn - x_hi.astype(F32)).astype(BF16)
    hi2 = _dot(x_hi, wr2_ref[...])
    logits = (hi2[:, :LANES] + (hi2[:, LANES:] + _dot(x_lo, wr2_ref[:, :LANES]))) + br_ref[...]
    lane = lax.broadcasted_iota(jnp.int32, logits.shape, 1)
    big = jnp.int32(LANES)
    ninf = -jnp.inf
    gl = jnp.where(lane < N_GROUPS, logits, ninf)
    gmax = jnp.max(gl, axis=-1, keepdims=True)
    g_sel = jnp.min(jnp.where(gl == gmax, lane, big), axis=-1, keepdims=True)
    g_prob = 1.0 / jnp.sum(jnp.exp(gl - gmax), axis=-1, keepdims=True)
    lo = N_GROUPS + EXPERTS_PER_GROUP * g_sel
    el = jnp.where((lane >= lo) & (lane < lo + EXPERTS_PER_GROUP), logits, ninf)
    v1 = jnp.max(el, axis=-1, keepdims=True)
    i1 = jnp.min(jnp.where(el == v1, lane, big), axis=-1, keepdims=True)
    el2 = jnp.where(lane == i1, ninf, el)
    v2 = jnp.max(el2, axis=-1, keepdims=True)
    i2 = jnp.min(jnp.where(el2 == v2, lane, big), axis=-1, keepdims=True)
    t = jnp.exp(v2 - v1)
    w1 = g_prob / (1.0 + t)
    w2 = g_prob * t / (1.0 + t)
    eid_ref[rows, :] = jnp.where(lane == 0, i1 - N_GROUPS, jnp.where(lane == 1, i2 - N_GROUPS, 0))
    gate_ref[rows, :] = jnp.where(lane == 0, w1, jnp.where(lane == 1, w2, 0.0))


def _merge(prompt, sample, w_out, g_ffn, wr2, b_r, tm):
    Tp, D = prompt[3].shape
    Ts = sample[3].shape[0]
    assert Tp % tm == 0 and Ts % tm == 0 and tm % MERGE_SUB == 0
    n_p, n_s = Tp // tm, Ts // tm
    T = Tp + Ts
    once = pl.Buffered(1)
    p_row = lambda w: pl.BlockSpec((tm, w), lambda i: (jnp.minimum(i, n_p - 1), 0))
    s_row = lambda w: pl.BlockSpec((tm, w), lambda i: (jnp.maximum(i - n_p, 0), 0), pipeline_mode=once)
    row = lambda w: pl.BlockSpec((tm, w), lambda i: (i, 0))
    full = lambda a: pl.BlockSpec(a.shape, lambda i: (0,) * a.ndim, pipeline_mode=once)
    widths = (FOX_W, HGRN_W, MEM_W, D)
    return pl.pallas_call(
        functools.partial(_merge_kernel, n_p=n_p),
        grid=(n_p + n_s,),
        in_specs=[p_row(w) for w in widths] + [s_row(w) for w in widths]
                 + [full(w_out), full(g_ffn), full(wr2), full(b_r)],
        out_specs=[row(D), row(D), row(LANES), row(LANES)],
        out_shape=[jax.ShapeDtypeStruct((T, D), F32), jax.ShapeDtypeStruct((T, D), F32),
                   jax.ShapeDtypeStruct((T, LANES), jnp.int32), jax.ShapeDtypeStruct((T, LANES), F32)],
        scratch_shapes=[pltpu.VMEM((tm, FOX_W + HGRN_W + MEM_W), BF16), pltpu.VMEM((tm, D), F32)],
        compiler_params=_cparams(("arbitrary",)),
        name="merge",
    )(*prompt, *sample, w_out, g_ffn, wr2, b_r)


def _experts_kernel(be_ref, nb_ref, src_ref, tok_ref, x_hbm, wg_ref, wu_ref, wd_ref, y_ref,
                    xbuf, wg_s, wu_s, wd_s, sem):
    b = pl.program_id(0)

    @pl.when((b == 0) | (be_ref[b] != be_ref[jnp.maximum(b - 1, 0)]))
    def _():
        wg_s[...] = wg_ref[0].astype(BF16)
        wu_s[...] = wu_ref[0].astype(BF16)
        wd_s[...] = wd_ref[0].astype(BF16)

    n_used = nb_ref[0]
    BM = MOE_BM

    def gather(blk, slot):
        base = src_ref[blk]
        for r in range(BM):
            pltpu.make_async_copy(x_hbm.at[pl.ds(tok_ref[base + r], 1)], xbuf.at[slot, pl.ds(r, 1)],
                                  sem.at[slot]).start()

    def wait(slot):
        pltpu.make_async_copy(x_hbm.at[pl.ds(0, BM)], xbuf.at[slot], sem.at[slot]).wait()

    slot = lax.rem(b, 2)

    @pl.when(b == 0)
    def _():
        gather(0, 0)

    for s in range(2):
        @pl.when((b + 1 < n_used) & (slot == s))
        def _():
            gather(b + 1, 1 - s)

    @pl.when(b < n_used)
    def _():
        wait(slot)
        x = xbuf[slot].astype(BF16)
        hmid = _dot(x, wg_s[...])
        hmid = hmid * _sigmoid(hmid) * _dot(x, wu_s[...])
        y_ref[...] = _dot(hmid.astype(BF16), wd_s[...])

    @pl.when(b >= n_used)
    def _():
        y_ref[...] = jnp.zeros_like(y_ref)


def _experts(block_e, n_used, block_src, sorted_tok, xn, w_gate, w_up, w_down):
    n_blocks = block_e.shape[0]
    D, FF = w_gate.shape[1:]
    BM = MOE_BM
    grid_spec = pltpu.PrefetchScalarGridSpec(
        num_scalar_prefetch=4,
        grid=(n_blocks,),
        in_specs=[pl.BlockSpec(memory_space=pl.ANY),
                  pl.BlockSpec((1, D, FF), lambda b, be, *_: (be[b], 0, 0)),
                  pl.BlockSpec((1, D, FF), lambda b, be, *_: (be[b], 0, 0)),
                  pl.BlockSpec((1, FF, D), lambda b, be, *_: (be[b], 0, 0))],
        out_specs=pl.BlockSpec((BM, D), lambda b, *_: (b, 0)),
        scratch_shapes=[pltpu.VMEM((2, BM, D), F32), pltpu.VMEM((D, FF), BF16), pltpu.VMEM((D, FF), BF16),
                        pltpu.VMEM((FF, D), BF16), pltpu.SemaphoreType.DMA((2,))],
    )
    return pl.pallas_call(
        _experts_kernel,
        grid_spec=grid_spec,
        out_shape=jax.ShapeDtypeStruct((n_blocks * BM, D), F32),
        compiler_params=_cparams(("arbitrary",)),
        name="experts",
    )(block_e, n_used, block_src, sorted_tok, xn, w_gate, w_up, w_down)


COMB_TM = 256


def _combine_kernel(pos_ref, h_ref, gate_ref, y_hbm, op_ref, os_ref, ybuf, sem, *, n_p):
    i = pl.program_id(0)
    n = pl.num_programs(0)
    TM = COMB_TM

    def gather(blk, slot):
        base = blk * (2 * TM)
        for r in range(2 * TM):
            pltpu.make_async_copy(y_hbm.at[pl.ds(pos_ref[base + r], 1)], ybuf.at[slot, pl.ds(r, 1)],
                                  sem.at[slot]).start()

    def wait(slot):
        pltpu.make_async_copy(y_hbm.at[pl.ds(0, 2 * TM)], ybuf.at[slot], sem.at[slot]).wait()

    slot = lax.rem(i, 2)

    @pl.when(i == 0)
    def _():
        gather(0, 0)

    for s in range(2):
        @pl.when((i + 1 < n) & (slot == s))
        def _():
            gather(i + 1, 1 - s)

    wait(slot)
    g = gate_ref[...]
    res = h_ref[...] + (g[:, 0:1] * ybuf[slot, 0:TM, :] + g[:, 1:2] * ybuf[slot, TM:2 * TM, :])

    @pl.when(i < n_p)
    def _():
        op_ref[...] = res

    @pl.when(i >= n_p)
    def _():
        os_ref[...] = res


def _combine(pos, h, gate, y_slots, t_prompt):
    T, D = h.shape
    TM = COMB_TM
    n_p = t_prompt // TM
    grid_spec = pltpu.PrefetchScalarGridSpec(
        num_scalar_prefetch=1,
        grid=(T // TM,),
        in_specs=[pl.BlockSpec((TM, D), lambda i, p: (i, 0)), pl.BlockSpec((TM, LANES), lambda i, p: (i, 0)),
                  pl.BlockSpec(memory_space=pl.ANY)],
        out_specs=[pl.BlockSpec((TM, D), lambda i, p: (jnp.minimum(i, n_p - 1), 0)),
                   pl.BlockSpec((TM, D), lambda i, p: (jnp.maximum(i - n_p, 0), 0))],
        scratch_shapes=[pltpu.VMEM((2, 2 * TM, D), F32), pltpu.SemaphoreType.DMA((2,))],
    )
    return pl.pallas_call(
        functools.partial(_combine_kernel, n_p=n_p),
        grid_spec=grid_spec,
        out_shape=[jax.ShapeDtypeStruct((t_prompt, D), F32), jax.ShapeDtypeStruct((T - t_prompt, D), F32)],
        compiler_params=_cparams(("arbitrary",)),
        name="combine",
    )(pos, h, gate, y_slots)


def _moe(h, xn, eid, gate, w_gate, w_up, w_down, t_prompt):
    T = h.shape[0]
    A = T * TOP_K
    BM = MOE_BM
    n_blocks = -(-(A + N_EXPERTS * (BM - 1)) // BM)
    e_flat = eid[:, :TOP_K].reshape(A)
    onehot = (e_flat[:, None] == jnp.arange(N_EXPERTS, dtype=jnp.int32)[None, :]).astype(jnp.int32)
    csum = jnp.cumsum(onehot, axis=0)
    counts = csum[-1]
    rank = jnp.take_along_axis(csum, e_flat[:, None], axis=1)[:, 0] - 1
    padded = (counts + BM - 1) // BM * BM
    pad_end = jnp.cumsum(padded)
    pad_start = pad_end - padded
    pos = (pad_start[e_flat] + rank).astype(jnp.int32)
    order = jnp.argsort(e_flat, stable=True)
    sorted_tok = jnp.pad((order // TOP_K).astype(jnp.int32), (0, BM))
    start = jnp.cumsum(counts) - counts
    block_first = jnp.arange(n_blocks, dtype=jnp.int32) * BM
    block_e = jnp.minimum(jnp.sum(pad_end[None, :] <= block_first[:, None], axis=1), N_EXPERTS - 1).astype(jnp.int32)
    block_src = jnp.minimum(start[block_e] + block_first - pad_start[block_e], A).astype(jnp.int32)
    n_used = (pad_end[-1] // BM).astype(jnp.int32).reshape(1)
    y_slots = _experts(block_e, n_used, block_src, sorted_tok, xn, w_gate, w_up, w_down)
    pos_tiles = pos.reshape(T // COMB_TM, COMB_TM, TOP_K).transpose(0, 2, 1).reshape(-1)
    return _combine(pos_tiles, h, gate, y_slots, t_prompt)


def _prep_weights(w_in, b_fox_f, w_router_group, b_router_group, w_router_expert, b_router_expert):
    D = w_in.shape[0]
    c = [0]
    for s in (FOX_W, FOX_W, FOX_W, FOX_HEADS, HGRN_W, HGRN_W, HGRN_W, HGRN_W, MEM_W):
        c.append(c[-1] + s)
    w_t = w_in.T
    w_a = w_t[c[0]:c[3]].astype(BF16)
    w_b = w_t[c[4]:c[8]].astype(BF16)
    w_q = w_t[c[8]:c[9]].astype(BF16)
    w_f = jnp.pad(w_t[c[3]:c[4]], ((0, LANES - FOX_HEADS), (0, 0))).astype(BF16)
    b_f_pad = jnp.zeros((1, LANES), F32).at[0, :FOX_HEADS].set(b_fox_f)
    n_r = N_GROUPS + N_EXPERTS
    w_r = jnp.zeros((D, LANES), F32).at[:, :N_GROUPS].set(w_router_group).at[:, N_GROUPS:n_r].set(w_router_expert)
    b_r = jnp.zeros((1, LANES), F32).at[0, :N_GROUPS].set(b_router_group).at[0, N_GROUPS:n_r].set(b_router_expert)
    wr_hi = w_r.astype(BF16)
    wr_lo = (w_r - wr_hi.astype(F32)).astype(BF16)
    wr2 = jnp.concatenate([wr_hi, wr_lo], axis=1)
    return (w_a, w_q, w_f, w_b), b_f_pad, wr2, b_r


def kernel(x_prompt, x_sample, cache_fox_k, cache_fox_v, cache_fox_logf, cache_mem_k, cache_mem_v, state_hgrn, page_table, mem_prompt, g_attn_norm, w_in, b_fox_f, g_fox_q, g_fox_k, lb_logits, g_hgrn_out, g_mem_norm, w_mem_kv, g_mem_q, g_mem_k, w_out, g_ffn_norm, w_router_group, b_router_group, w_router_expert, b_router_expert, w_gate_e, w_up_e, w_down_e):
    assert w_in.shape[0] == 1, "single-layer step"
    Bp, S, D = x_prompt.shape
    Bd, L, _ = x_sample.shape
    n_pool, page = cache_fox_k.shape[1], cache_fox_k.shape[2]
    assert page == LANES and L <= SUBLANES
    M = mem_prompt.shape[1]
    l = 0
    row = lambda a: a[l].reshape(1, -1)
    head_major = lambda a: jnp.swapaxes(a, -3, -2)

    w_seg, b_f_pad, wr2, b_r = _prep_weights(
        w_in[l], b_fox_f[l], w_router_group[l], b_router_group[l], w_router_expert[l], b_router_expert[l])
    w_out_bf = w_out[l].astype(BF16)
    w_mkv_bf = w_mem_kv[l].astype(BF16)
    experts_w = (w_gate_e[l], w_up_e[l], w_down_e[l])
    proj_args = (row(g_attn_norm), *w_seg, row(g_fox_q), row(g_fox_k), row(g_mem_q), lb_logits, b_f_pad)

    Tp = Bp * S
    xp = x_prompt.reshape(Tp, D)
    fq, fk_p, fkb, fv_p, fvb, hq, hlf, hv, hgs, mq, flf_p = _in_proj(xp, *proj_args, tm=512, seq=S)
    seq = lambda a: a.reshape(Bp, S, a.shape[-1])
    ck = _fox_cum(seq(flf_p))
    fox_o = _fox_attn(seq(fq), seq(fkb), seq(fvb), ck)
    hg_o, s_p = _hgrn(seq(hq), seq(hlf), seq(hv), seq(hgs), row(g_hgrn_out), None, C=256, n_heads=3, n_valid=256)
    mk, mv = _mem_kv(mem_prompt.reshape(Bp * M, D), row(g_mem_norm), w_mkv_bf, row(g_mem_k), tm=256)
    mem4 = lambda a, b: a.reshape(b, M, MEM_HEADS, HEAD_DIM)
    mem_o = _mem_attn(seq(mq), mem4(mk, Bp), mem4(mv, Bp), tq=512)
    prompt = (fox_o.reshape(Tp, FOX_W), hg_o.reshape(Tp, HGRN_W), mem_o.reshape(Tp, MEM_W), xp)

    R = SUBLANES
    Ts = Bd * R
    xs = jnp.pad(x_sample, ((0, 0), (0, R - L), (0, 0))).reshape(Ts, D)
    fq, fk_s, fkb, fv_s, fvb, hq, hlf, hv, hgs, mq, flf_s = _in_proj(xs, *proj_args, tm=Ts, seq=R)
    seqs = lambda a: a.reshape(Bd, R, a.shape[-1])
    lf_new_t = jnp.swapaxes(seqs(flf_s)[:, :, :HEAD_PAD], 1, 2)
    lf_new_t = jnp.pad(lf_new_t, ((0, 0), (0, 0), (0, LANES - R)))
    cache_lf_t = jnp.pad(jnp.swapaxes(cache_fox_logf[l].astype(F32), 1, 2), ((0, 0), (0, HEAD_PAD - FOX_HEADS), (0, 0)))
    fox_o = _fox_dec(page_table, seqs(fq), seqs(fkb), seqs(fvb), lf_new_t, head_major(cache_fox_k[l]), head_major(cache_fox_v[l]),
                     cache_lf_t, n_tok=L)
    hg_o, s_s = _hgrn(seqs(hq), seqs(hlf), seqs(hv), seqs(hgs), row(g_hgrn_out), state_hgrn[l],
                      C=R, n_heads=HGRN_HEADS, n_valid=L)
    mem_o = _mem_attn(seqs(mq), cache_mem_k[l], cache_mem_v[l], tq=R, nb=4 if Bd % 4 == 0 else 1)
    sample = (fox_o.reshape(Ts, FOX_W), hg_o.reshape(Ts, HGRN_W), mem_o.reshape(Ts, MEM_W), xs)

    sample = tuple(jnp.pad(a, ((0, -Ts % MERGE_TM), (0, 0))) for a in sample)
    h, xn, eid, gate = _merge(prompt, sample, w_out_bf, row(g_ffn_norm), wr2, b_r, tm=MERGE_TM)
    y_p, y_s = _moe(h, xn, eid, gate, *experts_w, t_prompt=Tp)

    cut = lambda a: a.reshape((Bd, R) + a.shape[1:])[:, :L]
    return (y_p.reshape(Bp, S, D), cut(y_s[:Ts]),
            head_major(fk_p)[None], head_major(fv_p)[None],
            flf_p[:, :FOX_HEADS].reshape(1, Bp, S, FOX_HEADS), s_p[None],
            mem4(mk, Bp)[None], mem4(mv, Bp)[None],
            head_major(fk_s)[None, :, :L], head_major(fv_s)[None, :, :L], cut(flf_s)[:, :, :FOX_HEADS][None], s_s[None])
```

```python
import functools

import jax
import jax.numpy as jnp
from jax import lax
from jax.experimental import pallas as pl
from jax.experimental.pallas import tpu as pltpu

F32 = jnp.float32
BF16 = jnp.bfloat16

HEAD_DIM = 128
FOX_HEADS = 6
HGRN_HEADS = 6
MEM_HEADS = 4
FOX_W = FOX_HEADS * HEAD_DIM
HGRN_W = HGRN_HEADS * HEAD_DIM
MEM_W = MEM_HEADS * HEAD_DIM
N_GROUPS = 4
EXPERTS_PER_GROUP = 8
N_EXPERTS = N_GROUPS * EXPERTS_PER_GROUP
TOP_K = 2
EPS = 1e-6
ATTN_SCALE = HEAD_DIM ** -0.5
LANES = 128
SUBLANES = 8
SEG_W = FOX_W
PROJ_CHUNK = 256
HEAD_PAD = 8
MOE_BM = 256
MERGE_TM = 512
MERGE_SUB = 256
VMEM_LIMIT = 52 * 1024 * 1024

NT_DIMS = (((1,), (1,)), ((), ()))
TN_DIMS = (((0,), (0,)), ((), ()))


def _cparams(sem, vmem=VMEM_LIMIT):
    return pltpu.CompilerParams(dimension_semantics=sem, vmem_limit_bytes=vmem)


def _split3(x):
    hi = x.astype(BF16)
    r1 = x - hi.astype(F32)
    mid = r1.astype(BF16)
    lo = (r1 - mid.astype(F32)).astype(BF16)
    return hi, mid, lo


def _dot(a, b):
    return jnp.dot(a, b, preferred_element_type=F32)


def _dot_nt(a, b):
    return lax.dot_general(a, b, NT_DIMS, preferred_element_type=F32)


def _sigmoid(x):
    return 1.0 / (1.0 + jnp.exp(-x))


def _log_sigmoid(x):
    return jnp.minimum(x, 0.0) - jnp.log(1.0 + jnp.exp(-jnp.abs(x)))


def _rms_heads(a, g, n_heads, scale):
    outs = []
    for h in range(n_heads):
        ah = a[:, h * HEAD_DIM:(h + 1) * HEAD_DIM]
        ms = jnp.mean(ah * ah, axis=-1, keepdims=True)
        outs.append(ah * lax.rsqrt(ms + EPS) * (g * scale))
    return jnp.concatenate(outs, axis=-1)


def _store_heads(ref, a, n_heads):
    for h in range(n_heads):
        ref[:, h, :] = a[:, h * HEAD_DIM:(h + 1) * HEAD_DIM]


def _store_head_major(ref, a, h0):
    nb, _, rows, _ = ref.shape
    for h in range(a.shape[1] // HEAD_DIM):
        ref[:, h0 + h] = a[:, h * HEAD_DIM:(h + 1) * HEAD_DIM].reshape(nb, rows, HEAD_DIM)


def _static_when(cond):
    def deco(f):
        if cond:
            f()
        return f
    return deco


def _in_proj_kernel(*refs, names, segs):
    r = dict(zip(names, refs))
    xn_s = r["xn_s"]
    x = r["x"][...]
    ms = jnp.mean(x * x, axis=-1, keepdims=True)
    xn_s[...] = (x * lax.rsqrt(ms + EPS) * r["g_norm"][...]).astype(BF16)

    def segment(post, w_ref, row0, width=SEG_W):
        for c in range(width // PROJ_CHUNK):
            cs = slice(c * PROJ_CHUNK, (c + 1) * PROJ_CHUNK)
            post(c, cs, _dot_nt(xn_s[...], w_ref[row0 + c * PROJ_CHUNK:row0 + (c + 1) * PROJ_CHUNK, :]))

    heads_per_chunk = PROJ_CHUNK // HEAD_DIM

    @_static_when("fq" in segs)
    def _():
        def post(c, cs, acc):
            r["fq"][:, cs] = _rms_heads(acc, r["g_fq"][...], heads_per_chunk, ATTN_SCALE).astype(BF16)
        segment(post, r["w_a"], 0)

    @_static_when("fk" in segs)
    def _():
        def post(c, cs, acc):
            fk = _rms_heads(acc, r["g_fk"][...], heads_per_chunk, 1.0)
            _store_head_major(r["fk"], fk, c * heads_per_chunk)
            r["fkb"][:, cs] = fk.astype(BF16)
        segment(post, r["w_a"], SEG_W)

    @_static_when("fv" in segs)
    def _():
        def post(c, cs, acc):
            _store_head_major(r["fv"], acc, c * heads_per_chunk)
            r["fvb"][:, cs] = acc.astype(BF16)
        segment(post, r["w_a"], 2 * SEG_W)

    @_static_when("mq" in segs)
    def _():
        def post(c, cs, acc):
            r["mq"][:, cs] = _rms_heads(acc, r["g_mq"][...], heads_per_chunk, ATTN_SCALE).astype(BF16)
        segment(post, r["w_q"], 0, MEM_W)

    @_static_when("flf" in segs)
    def _():
        r["flf"][...] = _log_sigmoid(_dot_nt(xn_s[...], r["w_f"][...]) + r["b_f"][...])

    @_static_when("hq" in segs)
    def _():
        def post(c, cs, acc):
            r["hq"][:, cs] = acc * _sigmoid(acc) * ATTN_SCALE
        segment(post, r["w_b"], 0)

    @_static_when("hlf" in segs)
    def _():
        l = r["lb_logits"][...]
        mx = jnp.max(l, axis=0, keepdims=True)
        ex = jnp.exp(l - mx)
        lb = ex[0:1, :] / jnp.sum(ex, axis=0, keepdims=True)

        def post(c, cs, acc):
            r["hlf"][:, cs] = jnp.log(lb[:, cs] + (1.0 - lb[:, cs]) * _sigmoid(acc))
        segment(post, r["w_b"], SEG_W)

    @_static_when("hv" in segs)
    def _():
        def post(c, cs, acc):
            r["hv"][:, cs] = acc
        segment(post, r["w_b"], 2 * SEG_W)

    @_static_when("hgs" in segs)
    def _():
        def post(c, cs, acc):
            r["hgs"][:, cs] = acc * _sigmoid(acc)
        segment(post, r["w_b"], 3 * SEG_W)


def _in_proj_call(x, ins, outs, tm, seq, name):
    T, D = x.shape
    nb, rows = max(1, tm // seq), min(tm, seq)
    assert nb * rows == tm and seq % rows == 0
    tiles_per_seq = seq // rows
    resident = lambda a: pl.BlockSpec(a.shape, lambda i: (0,) * a.ndim, pipeline_mode=pl.Buffered(1))

    def out_spec(o):
        if len(o.shape) == 4:
            return pl.BlockSpec((nb, FOX_HEADS, rows, HEAD_DIM), lambda i: (i // tiles_per_seq, 0, i % tiles_per_seq, 0))
        return pl.BlockSpec((tm, o.shape[1]), lambda i: (i, 0))

    names = ("x",) + tuple(ins) + tuple(outs) + ("xn_s",)
    res = pl.pallas_call(
        functools.partial(_in_proj_kernel, names=names, segs=tuple(outs)),
        grid=(T // tm,),
        in_specs=[pl.BlockSpec((tm, D), lambda i: (i, 0))] + [resident(a) for a in ins.values()],
        out_specs=[out_spec(o) for o in outs.values()],
        out_shape=list(outs.values()),
        scratch_shapes=[pltpu.VMEM((tm, D), BF16)],
        compiler_params=_cparams(("arbitrary",)),
        name=name,
    )(x, *ins.values())
    return dict(zip(outs, res))


def _in_proj(x, g_norm, w_a, w_q, w_f, w_b, g_fq, g_fk, g_mq, lb_logits, b_f_pad, tm, seq):
    T = x.shape[0]
    kv_cache = jax.ShapeDtypeStruct((T // seq, FOX_HEADS, seq, HEAD_DIM), F32)
    rows = lambda w, dt: jax.ShapeDtypeStruct((T, w), dt)
    fox = _in_proj_call(
        x, dict(g_norm=g_norm, w_a=w_a, w_q=w_q, w_f=w_f, g_fq=g_fq, g_fk=g_fk, g_mq=g_mq, b_f=b_f_pad),
        dict(fq=rows(FOX_W, BF16),
             fk=kv_cache,
             fkb=rows(FOX_W, BF16),
             fv=kv_cache, fvb=rows(FOX_W, BF16),
             mq=rows(MEM_W, BF16),
             flf=rows(LANES, F32)),
        tm, seq, "in_proj_fox")
    hg = _in_proj_call(
        x, dict(g_norm=g_norm, w_b=w_b, lb_logits=lb_logits),
        dict(hq=rows(HGRN_W, F32),
             hlf=rows(HGRN_W, F32),
             hv=rows(HGRN_W, F32),
             hgs=rows(HGRN_W, F32)),
        tm, seq, "in_proj_hgrn")
    return (fox["fq"], fox["fk"], fox["fkb"], fox["fv"], fox["fvb"], hg["hq"], hg["hlf"], hg["hv"], hg["hgs"],
            fox["mq"], fox["flf"])


CUM_C = 256


def _fox_cum_kernel(lf_ref, ck_ref):
    S = lf_ref.shape[1]
    r = lax.broadcasted_iota(jnp.int32, (CUM_C, CUM_C), 0)
    c = lax.broadcasted_iota(jnp.int32, (CUM_C, CUM_C), 1)
    tri = jnp.where(c <= r, 1.0, 0.0).astype(BF16)
    carry = jnp.zeros((1, LANES), F32)
    for i in range(S // CUM_C):
        sl = slice(i * CUM_C, (i + 1) * CUM_C)
        hi, mid, lo = _split3(lf_ref[0, sl, :])
        cum = (_dot(tri, hi) + _dot(tri, mid)) + _dot(tri, lo) + carry
        carry = cum[CUM_C - 1:CUM_C, :]
        ck_ref[0, :, sl] = cum.T[0:HEAD_PAD, :]


def _fox_cum(flf):
    B, S, _ = flf.shape
    return pl.pallas_call(
        _fox_cum_kernel,
        grid=(B,),
        in_specs=[pl.BlockSpec((1, S, LANES), lambda b: (b, 0, 0))],
        out_specs=pl.BlockSpec((1, HEAD_PAD, S), lambda b: (b, 0, 0)),
        out_shape=jax.ShapeDtypeStruct((B, HEAD_PAD, S), F32),
        compiler_params=_cparams(("arbitrary",)),
        name="fox_cum",
    )(flf)


FOX_T = 256


def _softmax_step(s, v_bf, m, l, acc):
    m_new = jnp.maximum(m, jnp.max(s, axis=-1, keepdims=True))
    alpha = jnp.exp(m - m_new)
    p = jnp.exp(s - m_new)
    l = alpha * l + jnp.sum(p, axis=-1, keepdims=True)
    acc = alpha * acc + _dot(p.astype(BF16), v_bf)
    return m_new, l, acc


def _fox_attn_kernel(q_ref, k_ref, v_ref, ck_ref, o_ref, m_s, l_s, acc_s):
    qi = pl.program_id(1)
    T = FOX_T
    m_s[...] = jnp.full_like(m_s, -jnp.inf)
    l_s[...] = jnp.zeros_like(l_s)
    acc_s[...] = jnp.zeros_like(acc_s)
    row = lax.broadcasted_iota(jnp.int32, (T, T), 0)
    col = lax.broadcasted_iota(jnp.int32, (T, T), 1)
    causal = col <= row

    def tile(ks, width, masked=False):
        for h in range(FOX_HEADS):
            hs = slice(h * HEAD_DIM, (h + 1) * HEAD_DIM)
            s = _dot_nt(q_ref[0, :, hs], k_ref[0, pl.ds(ks, width), hs]) - ck_ref[0, h:h + 1, pl.ds(ks, width)]
            if masked:
                s = jnp.where(causal, s, -jnp.inf)
            m_old = m_s[:, hs]
            m_new = jnp.maximum(m_old, jnp.max(s, axis=-1, keepdims=True))
            alpha = jnp.exp(m_old - m_new)
            p = jnp.exp(s - jnp.concatenate([m_new] * (width // HEAD_DIM), axis=-1))
            m_s[:, hs] = m_new
            l_s[:, hs] = alpha * l_s[:, hs] + jnp.sum(p, axis=-1, keepdims=True)
            acc_s[:, hs] = alpha * acc_s[:, hs] + _dot(p.astype(BF16), v_ref[0, pl.ds(ks, width), hs])

    def body(kt, c):
        tile(pl.multiple_of(kt * T, T), T)
        return c

    lax.fori_loop(0, qi, body, 0)
    tile(pl.multiple_of(qi * T, T), T, masked=True)
    o_ref[0] = (acc_s[...] / l_s[...]).astype(o_ref.dtype)


def _fox_attn(fq, fk, fv, ck):
    B, S, _ = fq.shape
    T = FOX_T
    return pl.pallas_call(
        _fox_attn_kernel,
        grid=(B, S // T),
        in_specs=[pl.BlockSpec((1, T, FOX_W), lambda b, i: (b, i, 0)),
                  pl.BlockSpec((1, S, FOX_W), lambda b, i: (b, 0, 0)),
                  pl.BlockSpec((1, S, FOX_W), lambda b, i: (b, 0, 0)),
                  pl.BlockSpec((1, HEAD_PAD, S), lambda b, i: (b, 0, 0))],
        out_specs=pl.BlockSpec((1, T, FOX_W), lambda b, i: (b, i, 0)),
        out_shape=jax.ShapeDtypeStruct((B, S, FOX_W), BF16),
        scratch_shapes=[pltpu.VMEM((T, FOX_W), F32)] * 3,
        compiler_params=_cparams(("arbitrary", "arbitrary")),
        name="fox_attn",
    )(fq, fk, fv, ck)


def _level_ref(b, m):
    C = b.shape[0]
    if 2 * m >= SUBLANES:
        b3 = b.reshape(C // (2 * m), 2 * m, HEAD_DIM)
        r = jnp.broadcast_to(b3[:, m - 1:m, :], b3.shape)
        return r.reshape(C, HEAD_DIM)
    b3 = b.reshape(C // SUBLANES, SUBLANES, HEAD_DIM)
    sub = lax.broadcasted_iota(jnp.int32, b3.shape, 1)
    pick = lambda i: jnp.broadcast_to(b3[:, i:i + 1, :], b3.shape)
    if m == 2:
        r = jnp.where(sub < 4, pick(1), pick(5))
    else:
        r = jnp.where(sub < 2, pick(0), jnp.where(sub < 4, pick(2), jnp.where(sub < 6, pick(4), pick(6))))
    return r.reshape(C, HEAD_DIM)


def _level_index(C):
    t = lax.broadcasted_iota(jnp.int32, (C, C), 0)
    s = lax.broadcasted_iota(jnp.int32, (C, C), 1)
    x = jnp.bitwise_xor(t, s)
    lvl = jnp.full((C, C), -1, jnp.int32)
    j, m = 0, 1
    while m < C:
        lvl = jnp.where(x >= m, j, lvl)
        j, m = j + 1, 2 * m
    return jnp.where(t > s, lvl, -1)


def _neg_abs(x):
    return pltpu.bitcast(jnp.bitwise_or(pltpu.bitcast(x, jnp.uint32), jnp.uint32(0x80000000)), F32)


def _hgrn_chunk(q, g, v, st, tri, lvl, n_valid):
    C = q.shape[0]
    hi, mid, lo = _split3(g)
    b = (_dot(tri, hi) + _dot(tri, mid)) + _dot(tri, lo)
    k = 1.0 - jnp.exp(g)
    a = jnp.zeros((C, C), F32)
    j, m = 0, 1
    while m < C:
        e = jnp.exp(_neg_abs(b - _level_ref(b, m)))
        a_l = _dot_nt((q * e).astype(BF16), (k * e).astype(BF16))
        a = jnp.where(lvl == j, a_l, a)
        j, m = j + 1, 2 * m
    v_bf = v.astype(BF16)
    diag = jnp.sum(q * k, axis=-1, keepdims=True)
    o = _dot_nt((q * jnp.exp(b)).astype(BF16), st.astype(BF16)) + _dot(a.astype(BF16), v_bf) + diag * v
    b_last = b[n_valid - 1:n_valid, :]
    kt = k * jnp.exp(jnp.minimum(b_last - b, 0.0))
    if n_valid < C:
        rows = lax.broadcasted_iota(jnp.int32, (C, HEAD_DIM), 0)
        kt = jnp.where(rows < n_valid, kt, 0.0)
    st_new = st * jnp.exp(b_last) + lax.dot_general(v_bf, kt.astype(BF16), TN_DIMS, preferred_element_type=F32)
    return o, st_new


def _hgrn_kernel(*refs, C, n_chunks, n_heads, n_valid, has_s0):
    if has_s0:
        q_ref, g_ref, v_ref, gs_ref, gn_ref, s0_ref, o_ref, sf_ref = refs
    else:
        q_ref, g_ref, v_ref, gs_ref, gn_ref, o_ref, sf_ref = refs
    r = lax.broadcasted_iota(jnp.int32, (C, C), 0)
    c = lax.broadcasted_iota(jnp.int32, (C, C), 1)
    tri = jnp.where(c <= r, 1.0, 0.0).astype(BF16)
    lvl = _level_index(C)
    gn = gn_ref[...]
    sts0 = tuple(s0_ref[0, h].T if has_s0 else jnp.zeros((HEAD_DIM, HEAD_DIM), F32) for h in range(n_heads))

    def body(ci, sts):
        rs = pl.ds(pl.multiple_of(ci * C, C), C)
        out = []
        for h in range(n_heads):
            hs = slice(h * HEAD_DIM, (h + 1) * HEAD_DIM)
            o, st = _hgrn_chunk(q_ref[0, rs, hs], g_ref[0, rs, hs], v_ref[0, rs, hs], sts[h], tri, lvl, n_valid)
            ms = jnp.mean(o * o, axis=-1, keepdims=True)
            o_ref[0, rs, hs] = (o * lax.rsqrt(ms + EPS) * gn * gs_ref[0, rs, hs]).astype(o_ref.dtype)
            out.append(st)
        return tuple(out)

    sts = lax.fori_loop(0, n_chunks, body, sts0) if n_chunks > 1 else body(0, sts0)
    for h in range(n_heads):
        sf_ref[0, h] = sts[h].T


def _hgrn(hq, hlf, hv, hgs, g_hn, s0, C, n_heads, n_valid):
    B, L, _ = hq.shape
    hp = HGRN_HEADS // n_heads
    w = n_heads * HEAD_DIM
    seq = pl.BlockSpec((1, L, w), lambda b, h: (b, 0, h))
    st_spec = pl.BlockSpec((1, n_heads, HEAD_DIM, HEAD_DIM), lambda b, h: (b, h, 0, 0))
    in_specs = [seq, seq, seq, seq, pl.BlockSpec((1, HEAD_DIM), lambda b, h: (0, 0))]
    args = [hq, hlf, hv, hgs, g_hn]
    if s0 is not None:
        in_specs.append(st_spec)
        args.append(s0)
    kern = functools.partial(_hgrn_kernel, C=C, n_chunks=L // C, n_heads=n_heads, n_valid=n_valid,
                             has_s0=s0 is not None)
    return pl.pallas_call(
        kern,
        grid=(B, hp),
        in_specs=in_specs,
        out_specs=[seq, st_spec],
        out_shape=[jax.ShapeDtypeStruct((B, L, HGRN_W), BF16),
                   jax.ShapeDtypeStruct((B, HGRN_HEADS, HEAD_DIM, HEAD_DIM), F32)],
        compiler_params=_cparams(("arbitrary", "arbitrary")),
        name="hgrn",
    )(*args)


def _mem_kv_kernel(x_ref, gn_ref, w_ref, gk_ref, mk_ref, mv_ref, xn_s):
    j = pl.program_id(1)

    @pl.when(j == 0)
    def _():
        x = x_ref[...]
        ms = jnp.mean(x * x, axis=-1, keepdims=True)
        xn_s[...] = (x * lax.rsqrt(ms + EPS) * gn_ref[...]).astype(BF16)

    acc = _dot(xn_s[...], w_ref[...])

    @pl.when(j == 0)
    def _():
        _store_heads(mk_ref, _rms_heads(acc, gk_ref[...], MEM_HEADS, 1.0), MEM_HEADS)

    @pl.when(j == 1)
    def _():
        _store_heads(mv_ref, acc, MEM_HEADS)


def _mem_kv(mem, g_norm, w, g_mk, tm):
    T, D = mem.shape
    out = jax.ShapeDtypeStruct((T, MEM_HEADS, HEAD_DIM), F32)
    return pl.pallas_call(
        _mem_kv_kernel,
        grid=(T // tm, 2),
        in_specs=[pl.BlockSpec((tm, D), lambda i, j: (i, 0)), pl.BlockSpec((1, D), lambda i, j: (0, 0)),
                  pl.BlockSpec((D, MEM_W), lambda i, j: (0, j)), pl.BlockSpec((1, HEAD_DIM), lambda i, j: (0, 0))],
        out_specs=[pl.BlockSpec((tm, MEM_HEADS, HEAD_DIM), lambda i, j: (i, 0, 0))] * 2,
        out_shape=[out, out],
        scratch_shapes=[pltpu.VMEM((tm, D), BF16)],
        compiler_params=_cparams(("arbitrary", "arbitrary")),
        name="mem_kv",
    )(mem, g_norm, w, g_mk)


def _mem_attn_kernel(q_ref, k_ref, v_ref, o_ref):
    for b in range(q_ref.shape[0]):
        for h in range(MEM_HEADS):
            hs = slice(h * HEAD_DIM, (h + 1) * HEAD_DIM)
            s = _dot_nt(q_ref[b, :, hs], k_ref[b, :, h, :].astype(BF16))
            p = jnp.exp(s - jnp.max(s, axis=-1, keepdims=True))
            l = jnp.sum(p, axis=-1, keepdims=True)
            o_ref[b, :, hs] = (_dot(p.astype(BF16), v_ref[b, :, h, :].astype(BF16)) / l).astype(o_ref.dtype)


def _mem_attn(mq, mk, mv, tq, nb=1):
    B, L, _ = mq.shape
    M = mk.shape[1]
    kv = pl.BlockSpec((nb, M, MEM_HEADS, HEAD_DIM), lambda b, i: (b, 0, 0, 0))
    qo = pl.BlockSpec((nb, tq, MEM_W), lambda b, i: (b, i, 0))
    return pl.pallas_call(
        _mem_attn_kernel,
        grid=(B // nb, L // tq),
        in_specs=[qo, kv, kv],
        out_specs=qo,
        out_shape=jax.ShapeDtypeStruct((B, L, MEM_W), BF16),
        compiler_params=_cparams(("arbitrary", "arbitrary")),
        name="mem_attn",
    )(mq, mk, mv)


DEC_G = 8
DEC_ROWS = 4 * HEAD_PAD


def _suffix_sum_lanes(x):
    lane = lax.broadcasted_iota(jnp.int32, x.shape, 1)
    s = 1
    while s < LANES:
        x = x + jnp.where(lane + s < LANES, pltpu.roll(x, LANES - s, 1), 0.0)
        s *= 2
    return x


def _prefix_sum_lanes(x):
    lane = lax.broadcasted_iota(jnp.int32, x.shape, 1)
    s = 1
    while s < LANES:
        x = x + jnp.where(lane >= s, pltpu.roll(x, s, 1), 0.0)
        s *= 2
    return x


def _fox_dec_kernel(pt_ref, q_ref, kn_ref, vn_ref, lfn_ref, *refs, n_tok):
    G = DEC_G
    k_refs, v_refs, lf_refs = refs[0:G], refs[G:2 * G], refs[2 * G:3 * G]
    o_ref = refs[3 * G]
    qbd, kn_s, vn_s, m_s, l_s, acc_s, car_s = refs[3 * G + 1:]
    st = pl.program_id(1)

    def attend(k_bf, v_bf, bias):
        s = _dot_nt(qbd[...].astype(BF16), k_bf) + bias
        m_new, l_new, acc_new = _softmax_step(s, v_bf, m_s[...], l_s[...], acc_s[...])
        m_s[...] = m_new
        l_s[...] = l_new
        acc_s[...] = acc_new

    @pl.when(st == 0)
    def _():
        qbd[...] = jnp.zeros_like(qbd)
        q = q_ref[0].astype(F32)
        for t in range(n_tok):
            for h in range(FOX_HEADS):
                hs = slice(h * HEAD_DIM, (h + 1) * HEAD_DIM)
                qbd[t * HEAD_PAD + h:t * HEAD_PAD + h + 1, hs] = q[t:t + 1, hs]
        kn_s[...] = jnp.zeros_like(kn_s)
        vn_s[...] = jnp.zeros_like(vn_s)
        kn_s[0:SUBLANES, :] = kn_ref[0].astype(F32)
        vn_s[0:SUBLANES, :] = vn_ref[0].astype(F32)
        m_s[...] = jnp.full_like(m_s, -jnp.inf)
        l_s[...] = jnp.zeros_like(l_s)
        acc_s[...] = jnp.zeros_like(acc_s)
        car_s[...] = jnp.zeros_like(car_s)
        ecum = _prefix_sum_lanes(lfn_ref[0])
        lane = lax.broadcasted_iota(jnp.int32, (HEAD_PAD, LANES), 1)
        bias = jnp.concatenate([jnp.where(lane <= t, -ecum, -jnp.inf) for t in range(n_tok)], axis=0)
        attend(kn_s[...].astype(BF16), vn_s[...].astype(BF16), bias)

    heads = lambda ref: jnp.concatenate([ref[0, h] for h in range(FOX_HEADS)], axis=-1).astype(BF16)
    carry = car_s[...]
    ds = []
    for i in range(G):
        lf = lf_refs[i][0]
        incl = _suffix_sum_lanes(lf)
        ds.append(carry + (incl - lf))
        carry = carry + incl[:, 0:1]
    car_s[...] = carry
    d_all = jnp.concatenate(ds, axis=-1)
    attend(jnp.concatenate([heads(r) for r in k_refs], axis=0),
           jnp.concatenate([heads(r) for r in v_refs], axis=0),
           jnp.concatenate([d_all] * n_tok, axis=0))

    @pl.when(st == pl.num_programs(1) - 1)
    def _():
        res = acc_s[...] / l_s[...]
        o_ref[...] = jnp.zeros_like(o_ref)
        for t in range(n_tok):
            for h in range(FOX_HEADS):
                hs = slice(h * HEAD_DIM, (h + 1) * HEAD_DIM)
                r = t * HEAD_PAD + h
                o_ref[0, t:t + 1, hs] = res[r:r + 1, hs].astype(o_ref.dtype)


def _fox_dec(page_table, fq, fk, fv, lf_new_t, cache_k, cache_v, cache_lf_t, n_tok):
    B, n_pages = page_table.shape
    G = DEC_G
    assert n_pages % G == 0 and n_tok * HEAD_PAD == DEC_ROWS
    n_steps = n_pages // G
    tok = lambda w: pl.BlockSpec((1, SUBLANES, w), lambda b, s, pt: (b, 0, 0))

    def page_spec(shape, i):
        return pl.BlockSpec((1,) + shape,
                            lambda b, s, pt: (pt[b * n_pages + (n_pages - 1 - (s * G + i))],) + (0,) * len(shape))

    kv_page = (FOX_HEADS, LANES, HEAD_DIM)
    in_specs = ([tok(FOX_W), tok(FOX_W), tok(FOX_W), tok(LANES)]
                + [page_spec(kv_page, i) for i in range(G)]
                + [page_spec(kv_page, i) for i in range(G)]
                + [page_spec((HEAD_PAD, LANES), i) for i in range(G)])
    grid_spec = pltpu.PrefetchScalarGridSpec(
        num_scalar_prefetch=1,
        grid=(B, n_steps),
        in_specs=in_specs,
        out_specs=pl.BlockSpec((1, SUBLANES, FOX_W), lambda b, s, pt: (b, 0, 0)),
        scratch_shapes=[pltpu.VMEM((DEC_ROWS, FOX_W), F32),
                        pltpu.VMEM((LANES, FOX_W), F32), pltpu.VMEM((LANES, FOX_W), F32),
                        pltpu.VMEM((DEC_ROWS, 1), F32), pltpu.VMEM((DEC_ROWS, 1), F32),
                        pltpu.VMEM((DEC_ROWS, FOX_W), F32), pltpu.VMEM((HEAD_PAD, 1), F32)],
    )
    return pl.pallas_call(
        functools.partial(_fox_dec_kernel, n_tok=n_tok),
        grid_spec=grid_spec,
        out_shape=jax.ShapeDtypeStruct((B, SUBLANES, FOX_W), BF16),
        compiler_params=_cparams(("arbitrary", "arbitrary")),
        name="fox_dec",
    )(page_table.reshape(-1), fq, fk, fv, lf_new_t, *([cache_k] * G), *([cache_v] * G), *([cache_lf_t] * G))


def _merge_kernel(fo_p, ho_p, mo_p, x_p, fo_s, ho_s, mo_s, x_s, w_ref, gf_ref, wr2_ref, br_ref,
                  h_ref, xn_ref, eid_ref, gate_ref, cat_s, x_sc, *, n_p):
    i = pl.program_id(0)

    def stage(fo_ref, ho_ref, mo_ref, x_ref):
        cat_s[:, 0:FOX_W] = fo_ref[...]
        cat_s[:, FOX_W:FOX_W + HGRN_W] = ho_ref[...]
        cat_s[:, FOX_W + HGRN_W:] = mo_ref[...]
        x_sc[...] = x_ref[...]

    @pl.when(i < n_p)
    def _():
        stage(fo_p, ho_p, mo_p, x_p)

    @pl.when(i >= n_p)
    def _():
        stage(fo_s, ho_s, mo_s, x_s)

    for u in range(x_sc.shape[0] // MERGE_SUB):
        rows = slice(u * MERGE_SUB, (u + 1) * MERGE_SUB)
        _merge_rows(rows, cat_s, x_sc, w_ref, gf_ref, wr2_ref, br_ref, h_ref, xn_ref, eid_ref, gate_ref)


def _merge_rows(rows, cat_s, x_sc, w_ref, gf_ref, wr2_ref, br_ref, h_ref, xn_ref, eid_ref, gate_ref):
    D = x_sc.shape[1]
    ssq = jnp.zeros((MERGE_SUB, 1), F32)
    for c in range(D // PROJ_CHUNK):
        cs = slice(c * PROJ_CHUNK, (c + 1) * PROJ_CHUNK)
        hc = x_sc[rows, cs] + _dot(cat_s[rows, :], w_ref[:, cs])
        h_ref[rows, cs] = hc
        ssq = ssq + jnp.sum(hc * hc, axis=-1, keepdims=True)
    xn = h_ref[rows, :] * lax.rsqrt(ssq * (1.0 / D) + EPS) * gf_ref[...]
    xn_ref[rows, :] = xn
    x_hi = xn.astype(BF16)
    x_lo = (xn - x_hi.astype(F32)).astype(BF16)
    hi2 = _dot(x_hi, wr2_ref[...])
    logits = (hi2[:, :LANES] + (hi2[:, LANES:] + _dot(x_lo, wr2_ref[:, :LANES]))) + br_ref[...]
    lane = lax.broadcasted_iota(jnp.int32, logits.shape, 1)
    big = jnp.int32(LANES)
    ninf = -jnp.inf
    gl = jnp.where(lane < N_GROUPS, logits, ninf)
    gmax = jnp.max(gl, axis=-1, keepdims=True)
    g_sel = jnp.min(jnp.where(gl == gmax, lane, big), axis=-1, keepdims=True)
    g_prob = 1.0 / jnp.sum(jnp.exp(gl - gmax), axis=-1, keepdims=True)
    lo = N_GROUPS + EXPERTS_PER_GROUP * g_sel
    el = jnp.where((lane >= lo) & (lane < lo + EXPERTS_PER_GROUP), logits, ninf)
    v1 = jnp.max(el, axis=-1, keepdims=True)
    i1 = jnp.min(jnp.where(el == v1, lane, big), axis=-1, keepdims=True)
    el2 = jnp.where(lane == i1, ninf, el)
    v2 = jnp.max(el2, axis=-1, keepdims=True)
    i2 = jnp.min(jnp.where(el2 == v2, lane, big), axis=-1, keepdims=True)
    t = jnp.exp(v2 - v1)
    w1 = g_prob / (1.0 + t)
    w2 = g_prob * t / (1.0 + t)
    eid_ref[rows, :] = jnp.where(lane == 0, i1 - N_GROUPS, jnp.where(lane == 1, i2 - N_GROUPS, 0))
    gate_ref[rows, :] = jnp.where(lane == 0, w1, jnp.where(lane == 1, w2, 0.0))


def _merge(prompt, sample, w_out, g_ffn, wr2, b_r, tm):
    Tp, D = prompt[3].shape
    Ts = sample[3].shape[0]
    assert Tp % tm == 0 and Ts % tm == 0 and tm % MERGE_SUB == 0
    n_p, n_s = Tp // tm, Ts // tm
    T = Tp + Ts
    once = pl.Buffered(1)
    p_row = lambda w: pl.BlockSpec((tm, w), lambda i: (jnp.minimum(i, n_p - 1), 0))
    s_row = lambda w: pl.BlockSpec((tm, w), lambda i: (jnp.maximum(i - n_p, 0), 0), pipeline_mode=once)
    row = lambda w: pl.BlockSpec((tm, w), lambda i: (i, 0))
    full = lambda a: pl.BlockSpec(a.shape, lambda i: (0,) * a.ndim, pipeline_mode=once)
    widths = (FOX_W, HGRN_W, MEM_W, D)
    return pl.pallas_call(
        functools.partial(_merge_kernel, n_p=n_p),
        grid=(n_p + n_s,),
        in_specs=[p_row(w) for w in widths] + [s_row(w) for w in widths]
                 + [full(w_out), full(g_ffn), full(wr2), full(b_r)],
        out_specs=[row(D), row(D), row(LANES), row(LANES)],
        out_shape=[jax.ShapeDtypeStruct((T, D), F32), jax.ShapeDtypeStruct((T, D), F32),
                   jax.ShapeDtypeStruct((T, LANES), jnp.int32), jax.ShapeDtypeStruct((T, LANES), F32)],
        scratch_shapes=[pltpu.VMEM((tm, FOX_W + HGRN_W + MEM_W), BF16), pltpu.VMEM((tm, D), F32)],
        compiler_params=_cparams(("arbitrary",)),
        name="merge",
    )(*prompt, *sample, w_out, g_ffn, wr2, b_r)


def _experts_kernel(be_ref, nb_ref, src_ref, tok_ref, x_hbm, wg_ref, wu_ref, wd_ref, y_ref,
                    xbuf, wg_s, wu_s, wd_s, sem):
    b = pl.program_id(0)

    @pl.when((b == 0) | (be_ref[b] != be_ref[jnp.maximum(b - 1, 0)]))
    def _():
        wg_s[...] = wg_ref[0].astype(BF16)
        wu_s[...] = wu_ref[0].astype(BF16)
        wd_s[...] = wd_ref[0].astype(BF16)

    n_used = nb_ref[0]
    BM = MOE_BM

    def gather(blk, slot):
        base = src_ref[blk]
        for r in range(BM):
            pltpu.make_async_copy(x_hbm.at[pl.ds(tok_ref[base + r], 1)], xbuf.at[slot, pl.ds(r, 1)],
                                  sem.at[slot]).start(priority=r % 2)

    def wait(slot):
        pltpu.make_async_copy(x_hbm.at[pl.ds(0, BM)], xbuf.at[slot], sem.at[slot]).wait()

    slot = lax.rem(b, 2)

    @pl.when(b == 0)
    def _():
        gather(0, 0)

    for s in range(2):
        @pl.when((b + 1 < n_used) & (slot == s))
        def _():
            gather(b + 1, 1 - s)

    @pl.when(b < n_used)
    def _():
        wait(slot)
        x = xbuf[slot].astype(BF16)
        hmid = _dot(x, wg_s[...])
        hmid = hmid * _sigmoid(hmid) * _dot(x, wu_s[...])
        y_ref[...] = _dot(hmid.astype(BF16), wd_s[...])

    @pl.when(b >= n_used)
    def _():
        y_ref[...] = jnp.zeros_like(y_ref)


def _experts(block_e, n_used, block_src, sorted_tok, xn, w_gate, w_up, w_down):
    n_blocks = block_e.shape[0]
    D, FF = w_gate.shape[1:]
    BM = MOE_BM
    grid_spec = pltpu.PrefetchScalarGridSpec(
        num_scalar_prefetch=4,
        grid=(n_blocks,),
        in_specs=[pl.BlockSpec(memory_space=pl.ANY),
                  pl.BlockSpec((1, D, FF), lambda b, be, *_: (be[b], 0, 0)),
                  pl.BlockSpec((1, D, FF), lambda b, be, *_: (be[b], 0, 0)),
                  pl.BlockSpec((1, FF, D), lambda b, be, *_: (be[b], 0, 0))],
        out_specs=pl.BlockSpec((BM, D), lambda b, *_: (b, 0)),
        scratch_shapes=[pltpu.VMEM((2, BM, D), F32), pltpu.VMEM((D, FF), BF16), pltpu.VMEM((D, FF), BF16),
                        pltpu.VMEM((FF, D), BF16), pltpu.SemaphoreType.DMA((2,))],
    )
    return pl.pallas_call(
        _experts_kernel,
        grid_spec=grid_spec,
        out_shape=jax.ShapeDtypeStruct((n_blocks * BM, D), F32),
        compiler_params=_cparams(("arbitrary",)),
        name="experts",
    )(block_e, n_used, block_src, sorted_tok, xn, w_gate, w_up, w_down)


COMB_TM = 256


def _combine_kernel(pos_ref, h_ref, gate_ref, y_hbm, op_ref, os_ref, ybuf, sem, *, n_p):
    i = pl.program_id(0)
    n = pl.num_programs(0)
    TM = COMB_TM

    def gather(blk, slot):
        base = blk * (2 * TM)
        for r in range(2 * TM):
            pltpu.make_async_copy(y_hbm.at[pl.ds(pos_ref[base + r], 1)], ybuf.at[slot, pl.ds(r, 1)],
                                  sem.at[slot]).start(priority=r % 2)

    def wait(slot):
        pltpu.make_async_copy(y_hbm.at[pl.ds(0, 2 * TM)], ybuf.at[slot], sem.at[slot]).wait()

    slot = lax.rem(i, 2)

    @pl.when(i == 0)
    def _():
        gather(0, 0)

    for s in range(2):
        @pl.when((i + 1 < n) & (slot == s))
        def _():
            gather(i + 1, 1 - s)

    wait(slot)
    g = gate_ref[...]
    res = h_ref[...] + (g[:, 0:1] * ybuf[slot, 0:TM, :] + g[:, 1:2] * ybuf[slot, TM:2 * TM, :])

    @pl.when(i < n_p)
    def _():
        op_ref[...] = res

    @pl.when(i >= n_p)
    def _():
        os_ref[...] = res


def _combine(pos, h, gate, y_slots, t_prompt):
    T, D = h.shape
    TM = COMB_TM
    n_p = t_prompt // TM
    grid_spec = pltpu.PrefetchScalarGridSpec(
        num_scalar_prefetch=1,
        grid=(T // TM,),
        in_specs=[pl.BlockSpec((TM, D), lambda i, p: (i, 0)), pl.BlockSpec((TM, LANES), lambda i, p: (i, 0)),
                  pl.BlockSpec(memory_space=pl.ANY)],
        out_specs=[pl.BlockSpec((TM, D), lambda i, p: (jnp.minimum(i, n_p - 1), 0)),
                   pl.BlockSpec((TM, D), lambda i, p: (jnp.maximum(i - n_p, 0), 0))],
        scratch_shapes=[pltpu.VMEM((2, 2 * TM, D), F32), pltpu.SemaphoreType.DMA((2,))],
    )
    return pl.pallas_call(
        functools.partial(_combine_kernel, n_p=n_p),
        grid_spec=grid_spec,
        out_shape=[jax.ShapeDtypeStruct((t_prompt, D), F32), jax.ShapeDtypeStruct((T - t_prompt, D), F32)],
        compiler_params=_cparams(("arbitrary",)),
        name="combine",
    )(pos, h, gate, y_slots)


def _moe(h, xn, eid, gate, w_gate, w_up, w_down, t_prompt):
    T = h.shape[0]
    A = T * TOP_K
    BM = MOE_BM
    n_blocks = -(-(A + N_EXPERTS * (BM - 1)) // BM)
    e_flat = eid[:, :TOP_K].reshape(A)
    onehot = (e_flat[:, None] == jnp.arange(N_EXPERTS, dtype=jnp.int32)[None, :]).astype(jnp.int32)
    csum = jnp.cumsum(onehot, axis=0)
    counts = csum[-1]
    rank = jnp.take_along_axis(csum, e_flat[:, None], axis=1)[:, 0] - 1
    padded = (counts + BM - 1) // BM * BM
    pad_end = jnp.cumsum(padded)
    pad_start = pad_end - padded
    pos = (pad_start[e_flat] + rank).astype(jnp.int32)
    order = jnp.argsort(e_flat, stable=True)
    sorted_tok = jnp.pad((order // TOP_K).astype(jnp.int32), (0, BM))
    start = jnp.cumsum(counts) - counts
    block_first = jnp.arange(n_blocks, dtype=jnp.int32) * BM
    block_e = jnp.minimum(jnp.sum(pad_end[None, :] <= block_first[:, None], axis=1), N_EXPERTS - 1).astype(jnp.int32)
    block_src = jnp.minimum(start[block_e] + block_first - pad_start[block_e], A).astype(jnp.int32)
    n_used = (pad_end[-1] // BM).astype(jnp.int32).reshape(1)
    y_slots = _experts(block_e, n_used, block_src, sorted_tok, xn, w_gate, w_up, w_down)
    pos_tiles = pos.reshape(T // COMB_TM, COMB_TM, TOP_K).transpose(0, 2, 1).reshape(-1)
    return _combine(pos_tiles, h, gate, y_slots, t_prompt)


def _prep_weights(w_in, b_fox_f, w_router_group, b_router_group, w_router_expert, b_router_expert):
    D = w_in.shape[0]
    c = [0]
    for s in (FOX_W, FOX_W, FOX_W, FOX_HEADS, HGRN_W, HGRN_W, HGRN_W, HGRN_W, MEM_W):
        c.append(c[-1] + s)
    w_t = w_in.T
    w_a = w_t[c[0]:c[3]].astype(BF16)
    w_b = w_t[c[4]:c[8]].astype(BF16)
    w_q = w_t[c[8]:c[9]].astype(BF16)
    w_f = jnp.pad(w_t[c[3]:c[4]], ((0, LANES - FOX_HEADS), (0, 0))).astype(BF16)
    b_f_pad = jnp.zeros((1, LANES), F32).at[0, :FOX_HEADS].set(b_fox_f)
    n_r = N_GROUPS + N_EXPERTS
    w_r = jnp.zeros((D, LANES), F32).at[:, :N_GROUPS].set(w_router_group).at[:, N_GROUPS:n_r].set(w_router_expert)
    b_r = jnp.zeros((1, LANES), F32).at[0, :N_GROUPS].set(b_router_group).at[0, N_GROUPS:n_r].set(b_router_expert)
    wr_hi = w_r.astype(BF16)
    wr_lo = (w_r - wr_hi.astype(F32)).astype(BF16)
    wr2 = jnp.concatenate([wr_hi, wr_lo], axis=1)
    return (w_a, w_q, w_f, w_b), b_f_pad, wr2, b_r


def kernel(x_prompt, x_sample, cache_fox_k, cache_fox_v, cache_fox_logf, cache_mem_k, cache_mem_v, state_hgrn, page_table, mem_prompt, g_attn_norm, w_in, b_fox_f, g_fox_q, g_fox_k, lb_logits, g_hgrn_out, g_mem_norm, w_mem_kv, g_mem_q, g_mem_k, w_out, g_ffn_norm, w_router_group, b_router_group, w_router_expert, b_router_expert, w_gate_e, w_up_e, w_down_e):
    assert w_in.shape[0] == 1, "single-layer step"
    Bp, S, D = x_prompt.shape
    Bd, L, _ = x_sample.shape
    n_pool, page = cache_fox_k.shape[1], cache_fox_k.shape[2]
    assert page == LANES and L <= SUBLANES
    M = mem_prompt.shape[1]
    l = 0
    row = lambda a: a[l].reshape(1, -1)
    head_major = lambda a: jnp.swapaxes(a, -3, -2)

    w_seg, b_f_pad, wr2, b_r = _prep_weights(
        w_in[l], b_fox_f[l], w_router_group[l], b_router_group[l], w_router_expert[l], b_router_expert[l])
    w_out_bf = w_out[l].astype(BF16)
    w_mkv_bf = w_mem_kv[l].astype(BF16)
    experts_w = (w_gate_e[l], w_up_e[l], w_down_e[l])
    proj_args = (row(g_attn_norm), *w_seg, row(g_fox_q), row(g_fox_k), row(g_mem_q), lb_logits, b_f_pad)

    Tp = Bp * S
    xp = x_prompt.reshape(Tp, D)
    fq, fk_p, fkb, fv_p, fvb, hq, hlf, hv, hgs, mq, flf_p = _in_proj(xp, *proj_args, tm=512, seq=S)
    seq = lambda a: a.reshape(Bp, S, a.shape[-1])
    ck = _fox_cum(seq(flf_p))
    fox_o = _fox_attn(seq(fq), seq(fkb), seq(fvb), ck)
    hg_o, s_p = _hgrn(seq(hq), seq(hlf), seq(hv), seq(hgs), row(g_hgrn_out), None, C=256, n_heads=3, n_valid=256)
    mk, mv = _mem_kv(mem_prompt.reshape(Bp * M, D), row(g_mem_norm), w_mkv_bf, row(g_mem_k), tm=256)
    mem4 = lambda a, b: a.reshape(b, M, MEM_HEADS, HEAD_DIM)
    mem_o = _mem_attn(seq(mq), mem4(mk, Bp), mem4(mv, Bp), tq=512)
    prompt = (fox_o.reshape(Tp, FOX_W), hg_o.reshape(Tp, HGRN_W), mem_o.reshape(Tp, MEM_W), xp)

    R = SUBLANES
    Ts = Bd * R
    xs = jnp.pad(x_sample, ((0, 0), (0, R - L), (0, 0))).reshape(Ts, D)
    fq, fk_s, fkb, fv_s, fvb, hq, hlf, hv, hgs, mq, flf_s = _in_proj(xs, *proj_args, tm=Ts, seq=R)
    seqs = lambda a: a.reshape(Bd, R, a.shape[-1])
    lf_new_t = jnp.swapaxes(seqs(flf_s)[:, :, :HEAD_PAD], 1, 2)
    lf_new_t = jnp.pad(lf_new_t, ((0, 0), (0, 0), (0, LANES - R)))
    cache_lf_t = jnp.pad(jnp.swapaxes(cache_fox_logf[l].astype(F32), 1, 2), ((0, 0), (0, HEAD_PAD - FOX_HEADS), (0, 0)))
    fox_o = _fox_dec(page_table, seqs(fq), seqs(fkb), seqs(fvb), lf_new_t, head_major(cache_fox_k[l]), head_major(cache_fox_v[l]),
                     cache_lf_t, n_tok=L)
    hg_o, s_s = _hgrn(seqs(hq), seqs(hlf), seqs(hv), seqs(hgs), row(g_hgrn_out), state_hgrn[l],
                      C=R, n_heads=HGRN_HEADS, n_valid=L)
    mem_o = _mem_attn(seqs(mq), cache_mem_k[l], cache_mem_v[l], tq=R, nb=4 if Bd % 4 == 0 else 1)
    sample = (fox_o.reshape(Ts, FOX_W), hg_o.reshape(Ts, HGRN_W), mem_o.reshape(Ts, MEM_W), xs)

    sample = tuple(jnp.pad(a, ((0, -Ts % MERGE_TM), (0, 0))) for a in sample)
    h, xn, eid, gate = _merge(prompt, sample, w_out_bf, row(g_ffn_norm), wr2, b_r, tm=MERGE_TM)
    y_p, y_s = _moe(h, xn, eid, gate, *experts_w, t_prompt=Tp)

    cut = lambda a: a.reshape((Bd, R) + a.shape[1:])[:, :L]
    return (y_p.reshape(Bp, S, D), cut(y_s[:Ts]),
            head_major(fk_p)[None], head_major(fv_p)[None],
            flf_p[:, :FOX_HEADS].reshape(1, Bp, S, FOX_HEADS), s_p[None],
            mem4(mk, Bp)[None], mem4(mv, Bp)[None],
            head_major(fk_s)[None, :, :L], head_major(fv_s)[None, :, :L], cut(flf_s)[:, :, :FOX_HEADS][None], s_s[None])
```
